```python
import math
import jax, jax.numpy as jnp
from jax import lax
import numpy as np

D_MODEL = 2048
BATCH = 32
SEQ = 256
DEPTH = 2
DEC_BATCH = 8
DEC_SEQ = 4096
PAST_LEN = 256

GRID_W = 64
NORM_EPS = 1e-6
SSD_HEADS = 16
SSD_HEAD_DIM = 64
SSD_DIM = SSD_HEADS * SSD_HEAD_DIM
SSD_STATE = 64
SSD_GROUPS = 4
SSD_CONV = 5
SSD_CHUNK = 64
SSD_CONV_DIM = SSD_DIM + 2 * SSD_GROUPS * SSD_STATE
HG_HEADS = 8
HG_KDIM = 128
HG_VDIM = 128
HG_FDIM = HG_HEADS * HG_KDIM
HG_DIM = HG_HEADS * HG_VDIM
HG_CHUNK = 32
ATT_HEADS = 8
ATT_HEAD_DIM = 128
ATT_DIM = ATT_HEADS * ATT_HEAD_DIM
WIN_ROWS = 8
WIN_COLS = 16
Q_BLOCK = 128
N_BRANCH = 3
N_EXPERTS = 16
N_EXPERT_GROUPS = 4
EXPERTS_PER_GROUP = N_EXPERTS // N_EXPERT_GROUPS
TOP_K = 2
D_EXPERT = 1024
MOE_BLOCK = 256
IN_SIZES = (SSD_DIM, SSD_CONV_DIM, 2 * SSD_HEADS,
            HG_FDIM, 2 * HG_FDIM, HG_DIM, HG_DIM,
            ATT_DIM, ATT_DIM, ATT_DIM,
            N_BRANCH * D_MODEL)
IN_DIM = (SSD_DIM + SSD_CONV_DIM + 2 * SSD_HEADS + HG_FDIM + 2 * HG_FDIM + HG_DIM + HG_DIM
          + 3 * ATT_DIM + N_BRANCH * D_MODEL)

kernel_name = 'hybrid_diffusion_ssd_hgrn2_natten_moe_step'


def _rmsnorm(x, g):
    xf = x.astype(jnp.float32)
    y = xf * lax.rsqrt(jnp.mean(xf * xf, axis=-1, keepdims=True) + NORM_EPS)
    return (y * g.astype(jnp.float32)).astype(x.dtype)


def _flip(t):
    return jnp.flip(t, axis=1)


def _segsum(a):
    t = a.shape[-1]
    cs = jnp.cumsum(a, axis=-1)
    diff = cs[..., :, None] - cs[..., None, :]
    return jnp.where(jnp.tril(jnp.ones((t, t), dtype=bool)), diff, -jnp.inf)


def _centred_dwconv(x, w, b):
    length = x.shape[1]
    pad = SSD_CONV // 2
    xp = jnp.pad(x, ((0, 0), (pad, pad), (0, 0)))
    y = b
    for j in range(SSD_CONV):
        y = y + xp[:, j:j + length] * w[j]
    return y


def _ssd_scan(x, dt, a_neg, bm, cm, init):
    b, length, h, p = x.shape
    n = bm.shape[-1]
    nc = length // SSD_CHUNK
    xdt = (x.astype(jnp.float32) * dt[..., None]).reshape(b, nc, SSD_CHUNK, h, p)
    a = (dt * a_neg).reshape(b, nc, SSD_CHUNK, h).transpose(0, 3, 1, 2)
    bc = bm.astype(jnp.float32).reshape(b, nc, SSD_CHUNK, h, n)
    cc = cm.astype(jnp.float32).reshape(b, nc, SSD_CHUNK, h, n)
    a_cs = jnp.cumsum(a, axis=-1)
    scores = jnp.einsum('bclhn,bcshn->bhcls', cc, bc) * jnp.exp(_segsum(a))
    y_diag = jnp.einsum('bhcls,bcshp->bclhp', scores, xdt)
    decay_states = jnp.exp(a_cs[..., -1:] - a_cs)
    states = jnp.einsum('bclhn,bhcl,bclhp->bchpn', bc, decay_states, xdt)
    states = jnp.concatenate([init[:, None], states], axis=1)
    chunk_decay = jnp.exp(_segsum(jnp.pad(a_cs[..., -1], ((0, 0), (0, 0), (1, 0)))))
    carried = jnp.einsum('bhzc,bchpn->bzhpn', chunk_decay, states)
    y_off = jnp.einsum('bclhn,bchpn,bhcl->bclhp', cc, carried[:, :-1], jnp.exp(a_cs))
    return (y_diag + y_off).reshape(b, length, h, p), carried[:, -1]


def _hgrn_scan(q, k, v, log_f, init):
    b, length, h, _ = q.shape
    nc = length // HG_CHUNK

    def to_chunks(t):
        t = t.astype(jnp.float32)
        return t.reshape(b, nc, HG_CHUNK, h, t.shape[-1]).transpose(1, 0, 3, 2, 4)

    mask = jnp.tril(jnp.ones((HG_CHUNK, HG_CHUNK), dtype=bool))

    def step(s, inp):
        qc, kc, vc, gc = inp
        bcum = jnp.cumsum(gc, axis=-2)
        o_inter = jnp.einsum('bhtk,bhkv->bhtv', qc * jnp.exp(bcum), s)
        diff = bcum[:, :, :, None, :] - bcum[:, :, None, :, :]
        dec = jnp.exp(jnp.where(mask[:, :, None], diff, -jnp.inf))
        att = jnp.einsum('bhtk,bhsk,bhtsk->bhts', qc, kc, dec)
        o_intra = jnp.einsum('bhts,bhsv->bhtv', att, vc)
        last = bcum[:, :, -1:, :]
        s_new = jnp.exp(last[:, :, 0, :])[..., None] * s + jnp.einsum(
            'bhsk,bhsv->bhkv', kc * jnp.exp(last - bcum), vc)
        return s_new, o_inter + o_intra

    final, o = lax.scan(step, init.astype(jnp.float32),
                        (to_chunks(q), to_chunks(k), to_chunks(v), to_chunks(log_f)))
    return o.transpose(1, 0, 3, 2, 4).reshape(b, length, h, -1), final


def _context_attention(q, k, v):
    b, length, h, dh = q.shape
    nb = length // Q_BLOCK
    scale = dh ** -0.5
    qb = q.reshape(b, nb, Q_BLOCK, h, dh).transpose(1, 0, 2, 3, 4)

    def block(qi):
        s = jnp.einsum('bqhd,bkhd->bhqk', qi, k).astype(jnp.float32) * scale
        p = jax.nn.softmax(s, axis=-1).astype(v.dtype)
        return jnp.einsum('bhqk,bkhd->bqhd', p, v)

    o = lax.map(block, qb)
    return o.transpose(1, 0, 2, 3, 4).reshape(b, length, h, dh)


def _natten_latent(q, k, v, k_ctx, v_ctx, rpb):
    b, length, h, dh = q.shape
    rows = length // GRID_W
    kr = min(WIN_ROWS, rows)
    scale = dh ** -0.5
    qg = q.reshape(b, rows, GRID_W, h, dh)
    kg = k.reshape(b, rows, GRID_W, h, dh)
    vg = v.reshape(b, rows, GRID_W, h, dh)
    col = jnp.arange(GRID_W)
    cs = jnp.clip(col - WIN_COLS // 2, 0, GRID_W - WIN_COLS)
    col_mask = (col[None, :] >= cs[:, None]) & (col[None, :] < cs[:, None] + WIN_COLS)
    dc_idx = jnp.clip(col[None, :] - col[:, None] + WIN_COLS - 1, 0, 2 * WIN_COLS - 2)
    bias_cols = rpb[:, :, dc_idx]

    def row_block(r):
        rs = jnp.clip(r - kr // 2, 0, rows - kr)
        qr = lax.dynamic_index_in_dim(qg, r, axis=1, keepdims=False)
        kw = lax.dynamic_slice_in_dim(kg, rs, kr, axis=1)
        vw = lax.dynamic_slice_in_dim(vg, rs, kr, axis=1)
        dr = rs + jnp.arange(kr) - r + WIN_ROWS - 1
        bias = bias_cols[:, dr].transpose(0, 2, 1, 3)
        s_lat = jnp.einsum('bqhd,brkhd->bhqrk', qr, kw).astype(jnp.float32) * scale + bias.astype(jnp.float32)
        s_lat = jnp.where(col_mask[:, None, :], s_lat, -jnp.inf).reshape(b, h, GRID_W, kr * GRID_W)
        s_ctx = jnp.einsum('bqhd,bchd->bhqc', qr, k_ctx).astype(jnp.float32) * scale
        p = jax.nn.softmax(jnp.concatenate([s_lat, s_ctx], axis=-1), axis=-1).astype(v.dtype)
        o = jnp.einsum('bhqj,bjhd->bqhd', p[..., :kr * GRID_W], vw.reshape(b, kr * GRID_W, h, dh))
        return o + jnp.einsum('bhqc,bchd->bqhd', p[..., kr * GRID_W:], v_ctx)

    out = lax.map(row_block, jnp.arange(rows))
    return out.transpose(1, 0, 2, 3, 4).reshape(b, length, h, dh)


def _mixer(h, lp, lb, ctx):
    b, length, _ = h.shape
    proj = h @ lp['w_in']
    parts, start = [], 0
    for size in IN_SIZES:
        parts.append(proj[..., start:start + size])
        start += size
    z, xbc, dt_raw, hq, hf, hi, hg, aq, ak, av, gates = parts
    if ctx is None:
        ssd_init = jnp.zeros((b, 2, SSD_HEADS, SSD_HEAD_DIM, SSD_STATE), jnp.float32)
        hg_init = jnp.zeros((b, 2, HG_HEADS, HG_KDIM, HG_VDIM), jnp.float32)
    else:
        k_ctx, v_ctx, ssd_init, hg_init = ctx
        ssd_init = ssd_init.astype(jnp.float32)
        hg_init = hg_init.astype(jnp.float32)

    xbc = jax.nn.silu(_centred_dwconv(xbc, lp['ssd_conv_w'], lp['ssd_conv_b']))
    gn = SSD_GROUPS * SSD_STATE
    rep = SSD_HEADS // SSD_GROUPS
    xs = xbc[..., :SSD_DIM].reshape(b, length, SSD_HEADS, SSD_HEAD_DIM)
    bm = jnp.repeat(xbc[..., SSD_DIM:SSD_DIM + gn].reshape(b, length, SSD_GROUPS, SSD_STATE), rep, axis=2)
    cm = jnp.repeat(xbc[..., SSD_DIM + gn:].reshape(b, length, SSD_GROUPS, SSD_STATE), rep, axis=2)
    dt = jax.nn.softplus(dt_raw.astype(jnp.float32).reshape(b, length, 2, SSD_HEADS)
                         + lp['ssd_dt_bias'].astype(jnp.float32))
    a_neg = -jnp.exp(lp['ssd_a_log'].astype(jnp.float32))
    y_f, s_f = _ssd_scan(xs, dt[:, :, 0], a_neg[0], bm, cm, ssd_init[:, 0])
    y_b, s_b = _ssd_scan(_flip(xs), _flip(dt[:, :, 1]), a_neg[1], _flip(bm), _flip(cm), ssd_init[:, 1])
    y = (y_f + _flip(y_b)).astype(h.dtype) + lp['ssd_d'][:, None] * xs
    y = _rmsnorm(y.reshape(b, length, SSD_DIM) * jax.nn.silu(z), lp['ssd_norm'])
    br_ssd = y @ lp['w_br_ssd']

    q_h = jax.nn.silu(hq).reshape(b, length, HG_HEADS, HG_KDIM)
    v_h = hi.reshape(b, length, HG_HEADS, HG_VDIM)
    lbh = lb.reshape(2, HG_HEADS, HG_KDIM)
    f = lbh + (1.0 - lbh) * jax.nn.sigmoid(hf.astype(jnp.float32).reshape(b, length, 2, HG_HEADS, HG_KDIM))
    log_f = jnp.log(f)
    k_in = 1.0 - f
    o_f, g_f = _hgrn_scan(q_h, k_in[:, :, 0], v_h, log_f[:, :, 0], hg_init[:, 0])
    o_b, g_b = _hgrn_scan(_flip(q_h), _flip(k_in[:, :, 1]), _flip(v_h), _flip(log_f[:, :, 1]), hg_init[:, 1])
    o = _rmsnorm(o_f + _flip(o_b), lp['hg_norm']).astype(h.dtype).reshape(b, length, HG_DIM)
    br_hg = (o * jax.nn.silu(hg)) @ lp['w_br_hg']

    aq = aq.reshape(b, length, ATT_HEADS, ATT_HEAD_DIM)
    ak = ak.reshape(b, length, ATT_HEADS, ATT_HEAD_DIM)
    av = av.reshape(b, length, ATT_HEADS, ATT_HEAD_DIM)
    if ctx is None:
        o_att = _context_attention(aq, ak, av)
    else:
        o_att = _natten_latent(aq, ak, av, k_ctx, v_ctx, lp['att_rpb'])
    br_att = o_att.reshape(b, length, ATT_DIM) @ lp['w_br_att']

    g = jax.nn.sigmoid(gates.reshape(b, length, N_BRANCH, D_MODEL))
    out = (g[:, :, 0] * br_ssd + g[:, :, 1] * br_hg + g[:, :, 2] * br_att) @ lp['w_out']
    if ctx is None:
        return out, (ak, av, jnp.stack([s_f, s_b], axis=1), jnp.stack([g_f, g_b], axis=1))
    return out, None


def _routed_experts(t, idx, wts, w1, w3, w2):
    n, d = t.shape
    nk = n * TOP_K
    n_blocks = (nk + N_EXPERTS * (MOE_BLOCK - 1) + MOE_BLOCK - 1) // MOE_BLOCK
    flat_e = idx.reshape(-1)
    order = jnp.argsort(flat_e)
    sorted_e = flat_e[order]
    counts = jnp.bincount(flat_e, length=N_EXPERTS)
    padded = (counts + MOE_BLOCK - 1) // MOE_BLOCK * MOE_BLOCK
    pad_end = jnp.cumsum(padded)
    pad_start = pad_end - padded
    raw_start = jnp.cumsum(counts) - counts
    dest = pad_start[sorted_e] + jnp.arange(nk) - raw_start[sorted_e]
    tok = order // TOP_K
    buf = jnp.zeros((n_blocks * MOE_BLOCK, d), t.dtype).at[dest].set(t[tok])
    block_e = jnp.minimum(jnp.searchsorted(pad_end, jnp.arange(n_blocks) * MOE_BLOCK, side='right'),
                          N_EXPERTS - 1)

    def expert_block(args):
        xb, e = args
        hdn = jax.nn.silu(xb @ w1[e]) * (xb @ w3[e])
        return hdn @ w2[e]

    out = lax.map(expert_block, (buf.reshape(n_blocks, MOE_BLOCK, d), block_e)).reshape(-1, d)
    contrib = out[dest] * wts.reshape(-1)[order][:, None].astype(out.dtype)
    return jnp.zeros_like(t).at[tok].add(contrib)


def _moe(h, w_router, router_bias, w1, w3, w2):
    b, length, d = h.shape
    t = h.reshape(-1, d)
    n = t.shape[0]
    scores = jax.nn.sigmoid((t @ w_router).astype(jnp.float32))
    sel = scores + router_bias.astype(jnp.float32)
    group_score = lax.top_k(sel.reshape(n, N_EXPERT_GROUPS, EXPERTS_PER_GROUP), 2)[0].sum(-1)
    best_group = jnp.argmax(group_score, axis=-1)
    in_group = (jnp.arange(N_EXPERTS) // EXPERTS_PER_GROUP)[None, :] == best_group[:, None]
    _, idx = lax.top_k(jnp.where(in_group, sel, -jnp.inf), TOP_K)
    wts = jnp.take_along_axis(scores, idx, axis=-1)
    wts = wts / jnp.sum(wts, axis=-1, keepdims=True)
    return _routed_experts(t, idx, wts, w1, w3, w2).reshape(b, length, d)


def _layer(x, mod, lp, lb, moe_p, ctx):
    shift_m, scale_m, gate_m, shift_f, scale_f, gate_f = jnp.split(mod, 6, axis=-1)
    h = _rmsnorm(x, lp['norm_mix']) * (1.0 + scale_m) + shift_m
    mix, ctx_tensors = _mixer(h, lp, lb, ctx)
    x = x + gate_m * mix
    h = _rmsnorm(x, lp['norm_moe']) * (1.0 + scale_f) + shift_f
    x = x + gate_f * _moe(h, *moe_p)
    return x, ctx_tensors


def setup_inputs(seed: int = 0) -> dict:
    key = jax.random.key(seed)
    ks = jax.random.split(key, 32)

    def nrm(i, shape, s):
        return jax.random.normal(ks[i], shape, jnp.float32) * s

    dt0 = jnp.exp(jax.random.uniform(ks[15], (DEPTH, 2, SSD_HEADS), jnp.float32,
                                     minval=math.log(1e-3), maxval=math.log(1e-1)))
    return {
        'x_prompt': nrm(0, (BATCH, SEQ, D_MODEL), 1.0),
        'x_sample': nrm(1, (DEC_BATCH, DEC_SEQ, D_MODEL), 1.0),
        'cache_k': nrm(2, (DEC_BATCH, DEPTH, PAST_LEN, ATT_HEADS, ATT_HEAD_DIM), 1.0),
        'cache_v': nrm(3, (DEC_BATCH, DEPTH, PAST_LEN, ATT_HEADS, ATT_HEAD_DIM), 1.0),
        'state_ssd': nrm(4, (DEC_BATCH, DEPTH, 2, SSD_HEADS, SSD_HEAD_DIM, SSD_STATE), 0.5),
        'state_hgrn': nrm(5, (DEC_BATCH, DEPTH, 2, HG_HEADS, HG_KDIM, HG_VDIM), 0.5),
        'c': nrm(6, (DEC_BATCH, D_MODEL), 1.0),
        'c_ctx': nrm(7, (D_MODEL,), 1.0),
        'w_ada': nrm(8, (DEPTH, D_MODEL, 6 * D_MODEL), 0.5 * D_MODEL ** -0.5),
        'b_ada': nrm(9, (DEPTH, 6 * D_MODEL), 0.02),
        'norm_mix': 1.0 + nrm(10, (DEPTH, D_MODEL), 0.02),
        'norm_moe': 1.0 + nrm(11, (DEPTH, D_MODEL), 0.02),
        'w_in': nrm(12, (DEPTH, D_MODEL, IN_DIM), D_MODEL ** -0.5),
        'ssd_conv_w': nrm(13, (DEPTH, SSD_CONV, SSD_CONV_DIM), SSD_CONV ** -0.5),
        'ssd_conv_b': nrm(14, (DEPTH, SSD_CONV_DIM), 0.02),
        'ssd_dt_bias': dt0 + jnp.log(-jnp.expm1(-dt0)),
        'ssd_a_log': jnp.log(jax.random.uniform(ks[16], (DEPTH, 2, SSD_HEADS), jnp.float32, minval=1.0, maxval=16.0)),
        'ssd_d': 1.0 + nrm(17, (DEPTH, SSD_HEADS), 0.1),
        'ssd_norm': 1.0 + nrm(18, (DEPTH, SSD_DIM), 0.02),
        'hg_lb_logits': nrm(19, (2, DEPTH, HG_FDIM), 0.5),
        'hg_norm': 1.0 + nrm(20, (DEPTH, HG_HEADS, HG_VDIM), 0.02),
        'att_rpb': nrm(21, (DEPTH, ATT_HEADS, 2 * WIN_ROWS - 1, 2 * WIN_COLS - 1), 0.1),
        'w_br_ssd': nrm(22, (DEPTH, SSD_DIM, D_MODEL), SSD_DIM ** -0.5),
        'w_br_hg': nrm(23, (DEPTH, HG_DIM, D_MODEL), HG_DIM ** -0.5),
        'w_br_att': nrm(24, (DEPTH, ATT_DIM, D_MODEL), ATT_DIM ** -0.5),
        'w_out': nrm(25, (DEPTH, D_MODEL, D_MODEL), D_MODEL ** -0.5),
        'w_router': nrm(26, (D_MODEL, N_EXPERTS), D_MODEL ** -0.5),
        'router_bias': nrm(27, (N_EXPERTS,), 0.01),
        'moe_w1': nrm(28, (DEPTH, N_EXPERTS, D_MODEL, D_EXPERT), D_MODEL ** -0.5),
        'moe_w3': nrm(29, (DEPTH, N_EXPERTS, D_MODEL, D_EXPERT), D_MODEL ** -0.5),
        'moe_w2': nrm(30, (DEPTH, N_EXPERTS, D_EXPERT, D_MODEL), D_EXPERT ** -0.5),
        'final_norm': 1.0 + nrm(31, (D_MODEL,), 0.02),
    }


def reference(x_prompt, x_sample, cache_k, cache_v, state_ssd, state_hgrn, c, c_ctx,
              w_ada, b_ada, norm_mix, norm_moe, w_in, ssd_conv_w, ssd_conv_b, ssd_dt_bias,
              ssd_a_log, ssd_d, ssd_norm, hg_lb_logits, hg_norm, att_rpb, w_br_ssd, w_br_hg,
              w_br_att, w_out, w_router, router_bias, moe_w1, moe_w3, moe_w2, final_norm):
    lb_cum = jnp.cumsum(jax.nn.softmax(hg_lb_logits.astype(jnp.float32), axis=1), axis=1)
    lower_bounds = lb_cum - lb_cum[:, :1]
    xp = x_prompt
    xs = x_sample
    new_k, new_v, new_ssd, new_hg = [], [], [], []
    for l in range(DEPTH):
        lp = {'w_in': w_in[l], 'ssd_conv_w': ssd_conv_w[l], 'ssd_conv_b': ssd_conv_b[l],
              'ssd_dt_bias': ssd_dt_bias[l], 'ssd_a_log': ssd_a_log[l], 'ssd_d': ssd_d[l],
              'ssd_norm': ssd_norm[l], 'hg_norm': hg_norm[l], 'att_rpb': att_rpb[l],
              'w_br_ssd': w_br_ssd[l], 'w_br_hg': w_br_hg[l], 'w_br_att': w_br_att[l],
              'w_out': w_out[l], 'norm_mix': norm_mix[l], 'norm_moe': norm_moe[l]}
        lb = lower_bounds[:, l]
        moe_p = (w_router, router_bias, moe_w1[l], moe_w3[l], moe_w2[l])
        mod_ctx = jax.nn.silu(c_ctx) @ w_ada[l] + b_ada[l]
        xp, ctx_t = _layer(xp, mod_ctx, lp, lb, moe_p, None)
        new_k.append(ctx_t[0])
        new_v.append(ctx_t[1])
        new_ssd.append(ctx_t[2])
        new_hg.append(ctx_t[3])
        mod_lat = (jax.nn.silu(c) @ w_ada[l] + b_ada[l])[:, None, :]
        xs, _ = _layer(xs, mod_lat, lp, lb, moe_p,
                       (cache_k[:, l], cache_v[:, l], state_ssd[:, l], state_hgrn[:, l]))
    y_prompt = _rmsnorm(xp, final_norm)
    y_sample = _rmsnorm(xs, final_norm)
    return (y_prompt, y_sample, jnp.stack(new_k, axis=1), jnp.stack(new_v, axis=1),
            jnp.stack(new_ssd, axis=1), jnp.stack(new_hg, axis=1))
```

```python
import functools
import math

import jax
import jax.numpy as jnp
from jax import lax
from jax.experimental import pallas as pl
from jax.experimental.pallas import tpu as pltpu

D_MODEL = 2048
BATCH = 32
SEQ = 256
DEPTH = 2
DEC_BATCH = 8
DEC_SEQ = 4096
GRID_W = 64
NORM_EPS = 1e-6
SSD_HEADS = 16
SSD_HEAD_DIM = 64
SSD_DIM = SSD_HEADS * SSD_HEAD_DIM
SSD_STATE = 64
SSD_GROUPS = 4
SSD_CONV = 5
SSD_CHUNK = 64
SSD_CONV_DIM = SSD_DIM + 2 * SSD_GROUPS * SSD_STATE
HG_HEADS = 8
HG_KDIM = 128
HG_VDIM = 128
HG_FDIM = HG_HEADS * HG_KDIM
HG_DIM = HG_HEADS * HG_VDIM
HG_CHUNK = 32
ATT_HEADS = 8
ATT_HEAD_DIM = 128
ATT_DIM = ATT_HEADS * ATT_HEAD_DIM
WIN_ROWS = 8
WIN_COLS = 16
Q_BLOCK = 128
N_BRANCH = 3
N_EXPERTS = 16
N_EXPERT_GROUPS = 4
EXPERTS_PER_GROUP = N_EXPERTS // N_EXPERT_GROUPS
TOP_K = 2
D_EXPERT = 1024

N_CTX = BATCH * SEQ
N_LAT = DEC_BATCH * DEC_SEQ
N_TOK = N_CTX + N_LAT

VMEM_LIMIT_BYTES = 56 * 1024 * 1024
LANES = 128

OFF_Z = 0
OFF_XBC = OFF_Z + SSD_DIM
OFF_HQ = OFF_XBC + SSD_CONV_DIM
OFF_HF = OFF_HQ + HG_FDIM
OFF_HI = OFF_HF + 2 * HG_FDIM
OFF_HG = OFF_HI + HG_DIM
OFF_AQ = OFF_HG + HG_DIM
OFF_AK = OFF_AQ + ATT_DIM
OFF_AV = OFF_AK + ATT_DIM
OFF_GATES = OFF_AV + ATT_DIM
OFF_DT = OFF_GATES + N_BRANCH * D_MODEL
DT_PAD = 512
PROJ_DIM = OFF_DT + DT_PAD

ROW_TILE = 1024
ROUTER_PAD = LANES
MOE_BLOCK = 512


def _mod_row(i):
    ctx_tiles = N_CTX // ROW_TILE
    tiles_per_req = DEC_SEQ // ROW_TILE
    return jnp.where(i < ctx_tiles, 0, 1 + (i - ctx_tiles) // tiles_per_req)


def _mm_kernel(x_ref, w_ref, o_ref):
    o_ref[...] = jnp.dot(x_ref[...].astype(jnp.bfloat16), w_ref[...],
                         preferred_element_type=jnp.float32).astype(o_ref.dtype)


def _matmul(x, w, tm, tn, out_dtype=jnp.float32):
    m, k = x.shape
    n = w.shape[1]
    return pl.pallas_call(
        _mm_kernel,
        grid=(m // tm, n // tn),
        in_specs=[pl.BlockSpec((tm, k), lambda i, j: (i, 0)),
                  pl.BlockSpec((k, tn), lambda i, j: (0, j))],
        out_specs=pl.BlockSpec((tm, tn), lambda i, j: (i, j)),
        out_shape=jax.ShapeDtypeStruct((m, n), out_dtype),
        compiler_params=pltpu.CompilerParams(
            dimension_semantics=("arbitrary", "arbitrary"), vmem_limit_bytes=VMEM_LIMIT_BYTES),
        name="matmul",
    )(x, w)


NORM_ROWS = 64


def _modulated_norm(x_ref, g_ref, sc_ref, sh_ref, store):
    g = g_ref[...]
    sc = 1.0 + sc_ref[...]
    sh = sh_ref[...]

    def body(r, carry):
        rows = pl.ds(pl.multiple_of(r * NORM_ROWS, NORM_ROWS), NORM_ROWS)
        x = x_ref[rows, :]
        y = x * lax.rsqrt(jnp.mean(x * x, axis=-1, keepdims=True) + NORM_EPS)
        store(rows, (y * g) * sc + sh)
        return carry

    lax.fori_loop(0, x_ref.shape[0] // NORM_ROWS, body, 0)


def _norm_mm_kernel(x_ref, g_ref, sc_ref, sh_ref, w_ref, o_ref, h_ref):
    @pl.when(pl.program_id(1) == 0)
    def _():
        def store(rows, h):
            h_ref[rows, :] = h.astype(jnp.bfloat16)
        _modulated_norm(x_ref, g_ref, sc_ref, sh_ref, store)

    o_ref[...] = jnp.dot(h_ref[...], w_ref[...], preferred_element_type=jnp.float32)


def _norm_matmul(x, gain, scale, shift, w, tn):
    n = w.shape[1]
    mod_spec = pl.BlockSpec((None, 1, D_MODEL), lambda i, j: (_mod_row(i), 0, 0))
    return pl.pallas_call(
        _norm_mm_kernel,
        grid=(N_TOK // ROW_TILE, n // tn),
        in_specs=[pl.BlockSpec((ROW_TILE, D_MODEL), lambda i, j: (i, 0)),
                  pl.BlockSpec((1, D_MODEL), lambda i, j: (0, 0)),
                  mod_spec, mod_spec,
                  pl.BlockSpec((D_MODEL, tn), lambda i, j: (0, j))],
        out_specs=pl.BlockSpec((ROW_TILE, tn), lambda i, j: (i, j)),
        out_shape=jax.ShapeDtypeStruct((N_TOK, n), jnp.float32),
        scratch_shapes=[pltpu.VMEM((ROW_TILE, D_MODEL), jnp.bfloat16)],
        compiler_params=pltpu.CompilerParams(
            dimension_semantics=("arbitrary", "arbitrary"), vmem_limit_bytes=VMEM_LIMIT_BYTES),
        name="norm_in_proj",
    )(x, gain, scale, shift, w)


def _norm_router_kernel(x_ref, g_ref, sc_ref, sh_ref, wr_hi_ref, wr_lo_ref, h_ref, logit_ref, lo_ref):
    def store(rows, h):
        h_hi = h.astype(jnp.bfloat16)
        h_ref[rows, :] = h_hi
        lo_ref[rows, :] = (h - h_hi.astype(jnp.float32)).astype(jnp.bfloat16)
    _modulated_norm(x_ref, g_ref, sc_ref, sh_ref, store)
    f32 = jnp.float32
    logit_ref[...] = (jnp.dot(h_ref[...], wr_hi_ref[...], preferred_element_type=f32)
                      + (jnp.dot(h_ref[...], wr_lo_ref[...], preferred_element_type=f32)
                         + jnp.dot(lo_ref[...], wr_hi_ref[...], preferred_element_type=f32)))


ROUTER_TILE = 512


def _norm_router(x, gain, scale, shift, wr_hi, wr_lo):
    tiles_per_mod = ROW_TILE // ROUTER_TILE
    mod_spec = pl.BlockSpec((None, 1, D_MODEL), lambda i: (_mod_row(i // tiles_per_mod), 0, 0))
    return pl.pallas_call(
        _norm_router_kernel,
        grid=(N_TOK // ROUTER_TILE,),
        in_specs=[pl.BlockSpec((ROUTER_TILE, D_MODEL), lambda i: (i, 0)),
                  pl.BlockSpec((1, D_MODEL), lambda i: (0, 0)),
                  mod_spec, mod_spec,
                  pl.BlockSpec((D_MODEL, ROUTER_PAD), lambda i: (0, 0)),
                  pl.BlockSpec((D_MODEL, ROUTER_PAD), lambda i: (0, 0))],
        out_specs=[pl.BlockSpec((ROUTER_TILE, D_MODEL), lambda i: (i, 0)),
                   pl.BlockSpec((ROUTER_TILE, ROUTER_PAD), lambda i: (i, 0))],
        out_shape=[jax.ShapeDtypeStruct((N_TOK, D_MODEL), jnp.bfloat16),
                   jax.ShapeDtypeStruct((N_TOK, ROUTER_PAD), jnp.float32)],
        scratch_shapes=[pltpu.VMEM((ROUTER_TILE, D_MODEL), jnp.bfloat16)],
        compiler_params=pltpu.CompilerParams(
            dimension_semantics=("arbitrary",), vmem_limit_bytes=VMEM_LIMIT_BYTES),
        name="norm_router",
    )(x, gain, scale, shift, wr_hi, wr_lo)


def _expert_kernel(be_ref, x_ref, w1_ref, w3_ref, w2_ref, o_ref):
    x = x_ref[...]
    a = jnp.dot(x, w1_ref[...], preferred_element_type=jnp.float32)
    b = jnp.dot(x, w3_ref[...], preferred_element_type=jnp.float32)
    hdn = (a * jax.nn.sigmoid(a)) * b
    o_ref[...] = jnp.dot(hdn.astype(jnp.bfloat16), w2_ref[...], preferred_element_type=jnp.float32)


def _expert_blocks(buf, block_e, w1, w3, w2):
    n_blocks = buf.shape[0] // MOE_BLOCK
    grid_spec = pltpu.PrefetchScalarGridSpec(
        num_scalar_prefetch=1,
        grid=(n_blocks,),
        in_specs=[pl.BlockSpec((MOE_BLOCK, D_MODEL), lambda i, be: (i, 0)),
                  pl.BlockSpec((None, D_MODEL, D_EXPERT), lambda i, be: (be[i], 0, 0)),
                  pl.BlockSpec((None, D_MODEL, D_EXPERT), lambda i, be: (be[i], 0, 0)),
                  pl.BlockSpec((None, D_EXPERT, D_MODEL), lambda i, be: (be[i], 0, 0))],
        out_specs=pl.BlockSpec((MOE_BLOCK, D_MODEL), lambda i, be: (i, 0)),
    )
    return pl.pallas_call(
        _expert_kernel,
        grid_spec=grid_spec,
        out_shape=jax.ShapeDtypeStruct((n_blocks * MOE_BLOCK, D_MODEL), jnp.float32),
        compiler_params=pltpu.CompilerParams(
            dimension_semantics=("arbitrary",), vmem_limit_bytes=VMEM_LIMIT_BYTES),
        name="moe_experts",
    )(block_e, buf, w1, w3, w2)


def _moe(h_bf16, logits, router_bias, w1, w3, w2):
    n = h_bf16.shape[0]
    nk = n * TOP_K
    n_blocks = (nk + N_EXPERTS * (MOE_BLOCK - 1) + MOE_BLOCK - 1) // MOE_BLOCK
    scores = jax.nn.sigmoid(logits[:, :N_EXPERTS])
    sel = scores + router_bias.astype(jnp.float32)
    group_score = lax.top_k(sel.reshape(n, N_EXPERT_GROUPS, EXPERTS_PER_GROUP), 2)[0].sum(-1)
    best_group = jnp.argmax(group_score, axis=-1)
    in_group = (jnp.arange(N_EXPERTS) // EXPERTS_PER_GROUP)[None, :] == best_group[:, None]
    _, idx = lax.top_k(jnp.where(in_group, sel, -jnp.inf), TOP_K)
    wts = jnp.take_along_axis(scores, idx, axis=-1)
    wts = wts / jnp.sum(wts, axis=-1, keepdims=True)

    flat_e = idx.reshape(-1).astype(jnp.int32)
    onehot = (flat_e[:, None] == jnp.arange(N_EXPERTS, dtype=jnp.int32)[None, :]).astype(jnp.int32)
    rank = jnp.take_along_axis(jnp.cumsum(onehot, axis=0) - onehot, flat_e[:, None], axis=1)[:, 0]
    counts = jnp.sum(onehot, axis=0)
    padded = (counts + MOE_BLOCK - 1) // MOE_BLOCK * MOE_BLOCK
    pad_end = jnp.cumsum(padded)
    pad_start = pad_end - padded
    dest = pad_start[flat_e] + rank
    tok = jnp.arange(nk, dtype=jnp.int32) // TOP_K
    src = jnp.zeros((n_blocks * MOE_BLOCK,), jnp.int32).at[dest].set(tok)
    buf = h_bf16[src]
    block_e = jnp.minimum(jnp.searchsorted(pad_end, jnp.arange(n_blocks) * MOE_BLOCK, side='right'),
                          N_EXPERTS - 1).astype(jnp.int32)
    out = _expert_blocks(buf, block_e, w1, w3, w2)
    picked = out[dest].reshape(n, TOP_K, D_MODEL)
    return jnp.sum(picked * wts[:, :, None], axis=1)


def _final_norm_kernel(x_ref, g_ref, o_ref):
    g = g_ref[...]

    def body(r, carry):
        rows = pl.ds(pl.multiple_of(r * NORM_ROWS, NORM_ROWS), NORM_ROWS)
        x = x_ref[rows, :]
        o_ref[rows, :] = (x * lax.rsqrt(jnp.mean(x * x, axis=-1, keepdims=True) + NORM_EPS)) * g
        return carry

    lax.fori_loop(0, x_ref.shape[0] // NORM_ROWS, body, 0)


def _final_norm(x, gain):
    return pl.pallas_call(
        _final_norm_kernel,
        grid=(N_TOK // ROW_TILE,),
        in_specs=[pl.BlockSpec((ROW_TILE, D_MODEL), lambda i: (i, 0)),
                  pl.BlockSpec((1, D_MODEL), lambda i: (0, 0))],
        out_specs=pl.BlockSpec((ROW_TILE, D_MODEL), lambda i: (i, 0)),
        out_shape=jax.ShapeDtypeStruct((N_TOK, D_MODEL), jnp.float32),
        compiler_params=pltpu.CompilerParams(
            dimension_semantics=("arbitrary",), vmem_limit_bytes=VMEM_LIMIT_BYTES),
        name="final_norm",
    )(x, gain)


def _rmsnorm(x, g):
    xf = x.astype(jnp.float32)
    y = xf * lax.rsqrt(jnp.mean(xf * xf, axis=-1, keepdims=True) + NORM_EPS)
    return (y * g.astype(jnp.float32)).astype(x.dtype)


def _flip(t):
    return jnp.flip(t, axis=1)


def _segsum(a):
    t = a.shape[-1]
    cs = jnp.cumsum(a, axis=-1)
    diff = cs[..., :, None] - cs[..., None, :]
    return jnp.where(jnp.tril(jnp.ones((t, t), dtype=bool)), diff, -jnp.inf)


def _centred_dwconv(x, w, b):
    length = x.shape[1]
    pad = SSD_CONV // 2
    xp = jnp.pad(x, ((0, 0), (pad, pad), (0, 0)))
    y = b
    for j in range(SSD_CONV):
        y = y + xp[:, j:j + length] * w[j]
    return y


def _ssd_scan(x, dt, a_neg, bm, cm, init):
    b, length, h, p = x.shape
    n = bm.shape[-1]
    nc = length // SSD_CHUNK
    xdt = (x.astype(jnp.float32) * dt[..., None]).reshape(b, nc, SSD_CHUNK, h, p)
    a = (dt * a_neg).reshape(b, nc, SSD_CHUNK, h).transpose(0, 3, 1, 2)
    bc = bm.astype(jnp.float32).reshape(b, nc, SSD_CHUNK, h, n)
    cc = cm.astype(jnp.float32).reshape(b, nc, SSD_CHUNK, h, n)
    a_cs = jnp.cumsum(a, axis=-1)
    scores = jnp.einsum('bclhn,bcshn->bhcls', cc, bc) * jnp.exp(_segsum(a))
    y_diag = jnp.einsum('bhcls,bcshp->bclhp', scores, xdt)
    decay_states = jnp.exp(a_cs[..., -1:] - a_cs)
    states = jnp.einsum('bclhn,bhcl,bclhp->bchpn', bc, decay_states, xdt)
    states = jnp.concatenate([init[:, None], states], axis=1)
    chunk_decay = jnp.exp(_segsum(jnp.pad(a_cs[..., -1], ((0, 0), (0, 0), (1, 0)))))
    carried = jnp.einsum('bhzc,bchpn->bzhpn', chunk_decay, states)
    y_off = jnp.einsum('bclhn,bchpn,bhcl->bclhp', cc, carried[:, :-1], jnp.exp(a_cs))
    return (y_diag + y_off).reshape(b, length, h, p), carried[:, -1]


def _hgrn_scan(q, k, v, log_f, init):
    b, length, h, _ = q.shape
    nc = length // HG_CHUNK

    def to_chunks(t):
        t = t.astype(jnp.float32)
        return t.reshape(b, nc, HG_CHUNK, h, t.shape[-1]).transpose(1, 0, 3, 2, 4)

    mask = jnp.tril(jnp.ones((HG_CHUNK, HG_CHUNK), dtype=bool))

    def step(s, inp):
        qc, kc, vc, gc = inp
        bcum = jnp.cumsum(gc, axis=-2)
        o_inter = jnp.einsum('bhtk,bhkv->bhtv', qc * jnp.exp(bcum), s)
        diff = bcum[:, :, :, None, :] - bcum[:, :, None, :, :]
        dec = jnp.exp(jnp.where(mask[:, :, None], diff, -jnp.inf))
        att = jnp.einsum('bhtk,bhsk,bhtsk->bhts', qc, kc, dec)
        o_intra = jnp.einsum('bhts,bhsv->bhtv', att, vc)
        last = bcum[:, :, -1:, :]
        s_new = jnp.exp(last[:, :, 0, :])[..., None] * s + jnp.einsum(
            'bhsk,bhsv->bhkv', kc * jnp.exp(last - bcum), vc)
        return s_new, o_inter + o_intra

    final, o = lax.scan(step, init.astype(jnp.float32),
                        (to_chunks(q), to_chunks(k), to_chunks(v), to_chunks(log_f)))
    return o.transpose(1, 0, 3, 2, 4).reshape(b, length, h, -1), final


def _context_attention(q, k, v):
    b, length, h, dh = q.shape
    nb = length // Q_BLOCK
    scale = dh ** -0.5
    qb = q.reshape(b, nb, Q_BLOCK, h, dh).transpose(1, 0, 2, 3, 4)

    def block(qi):
        s = jnp.einsum('bqhd,bkhd->bhqk', qi, k).astype(jnp.float32) * scale
        p = jax.nn.softmax(s, axis=-1).astype(v.dtype)
        return jnp.einsum('bhqk,bkhd->bqhd', p, v)

    o = lax.map(block, qb)
    return o.transpose(1, 0, 2, 3, 4).reshape(b, length, h, dh)


def _natten_latent(q, k, v, k_ctx, v_ctx, rpb):
    b, length, h, dh = q.shape
    rows = length // GRID_W
    kr = min(WIN_ROWS, rows)
    scale = dh ** -0.5
    qg = q.reshape(b, rows, GRID_W, h, dh)
    kg = k.reshape(b, rows, GRID_W, h, dh)
    vg = v.reshape(b, rows, GRID_W, h, dh)
    col = jnp.arange(GRID_W)
    cs = jnp.clip(col - WIN_COLS // 2, 0, GRID_W - WIN_COLS)
    col_mask = (col[None, :] >= cs[:, None]) & (col[None, :] < cs[:, None] + WIN_COLS)
    dc_idx = jnp.clip(col[None, :] - col[:, None] + WIN_COLS - 1, 0, 2 * WIN_COLS - 2)
    bias_cols = rpb[:, :, dc_idx]

    def row_block(r):
        rs = jnp.clip(r - kr // 2, 0, rows - kr)
        qr = lax.dynamic_index_in_dim(qg, r, axis=1, keepdims=False)
        kw = lax.dynamic_slice_in_dim(kg, rs, kr, axis=1)
        vw = lax.dynamic_slice_in_dim(vg, rs, kr, axis=1)
        dr = rs + jnp.arange(kr) - r + WIN_ROWS - 1
        bias = bias_cols[:, dr].transpose(0, 2, 1, 3)
        s_lat = jnp.einsum('bqhd,brkhd->bhqrk', qr, kw).astype(jnp.float32) * scale + bias.astype(jnp.float32)
        s_lat = jnp.where(col_mask[:, None, :], s_lat, -jnp.inf).reshape(b, h, GRID_W, kr * GRID_W)
        s_ctx = jnp.einsum('bqhd,bchd->bhqc', qr, k_ctx).astype(jnp.float32) * scale
        p = jax.nn.softmax(jnp.concatenate([s_lat, s_ctx], axis=-1), axis=-1).astype(v.dtype)
        o = jnp.einsum('bhqj,bjhd->bqhd', p[..., :kr * GRID_W], vw.reshape(b, kr * GRID_W, h, dh))
        return o + jnp.einsum('bhqc,bchd->bqhd', p[..., kr * GRID_W:], v_ctx)

    out = lax.map(row_block, jnp.arange(rows))
    return out.transpose(1, 0, 2, 3, 4).reshape(b, length, h, dh)


def _mixer_group(proj, lp, lb, ctx):
    b, length, _ = proj.shape
    z = proj[..., OFF_Z:OFF_Z + SSD_DIM]
    xbc = proj[..., OFF_XBC:OFF_XBC + SSD_CONV_DIM]
    dt_raw = proj[..., OFF_DT:OFF_DT + 2 * SSD_HEADS]
    hq = proj[..., OFF_HQ:OFF_HQ + HG_FDIM]
    hf = proj[..., OFF_HF:OFF_HF + 2 * HG_FDIM]
    hi = proj[..., OFF_HI:OFF_HI + HG_DIM]
    hg = proj[..., OFF_HG:OFF_HG + HG_DIM]
    aq = proj[..., OFF_AQ:OFF_AQ + ATT_DIM]
    ak = proj[..., OFF_AK:OFF_AK + ATT_DIM]
    av = proj[..., OFF_AV:OFF_AV + ATT_DIM]
    if ctx is None:
        ssd_init = jnp.zeros((b, 2, SSD_HEADS, SSD_HEAD_DIM, SSD_STATE), jnp.float32)
        hg_init = jnp.zeros((b, 2, HG_HEADS, HG_KDIM, HG_VDIM), jnp.float32)
    else:
        k_ctx, v_ctx, ssd_init, hg_init = ctx

    xbc = jax.nn.silu(_centred_dwconv(xbc, lp['ssd_conv_w'], lp['ssd_conv_b']))
    gn = SSD_GROUPS * SSD_STATE
    rep = SSD_HEADS // SSD_GROUPS
    xs = xbc[..., :SSD_DIM].reshape(b, length, SSD_HEADS, SSD_HEAD_DIM)
    bm = jnp.repeat(xbc[..., SSD_DIM:SSD_DIM + gn].reshape(b, length, SSD_GROUPS, SSD_STATE), rep, axis=2)
    cm = jnp.repeat(xbc[..., SSD_DIM + gn:].reshape(b, length, SSD_GROUPS, SSD_STATE), rep, axis=2)
    dt = jax.nn.softplus(dt_raw.reshape(b, length, 2, SSD_HEADS) + lp['ssd_dt_bias'])
    a_neg = -jnp.exp(lp['ssd_a_log'])
    y_f, s_f = _ssd_scan(xs, dt[:, :, 0], a_neg[0], bm, cm, ssd_init[:, 0])
    y_b, s_b = _ssd_scan(_flip(xs), _flip(dt[:, :, 1]), a_neg[1], _flip(bm), _flip(cm), ssd_init[:, 1])
    y = (y_f + _flip(y_b)) + lp['ssd_d'][:, None] * xs
    y = _rmsnorm(y.reshape(b, length, SSD_DIM) * jax.nn.silu(z), lp['ssd_norm'])

    q_h = jax.nn.silu(hq).reshape(b, length, HG_HEADS, HG_KDIM)
    v_h = hi.reshape(b, length, HG_HEADS, HG_VDIM)
    lbh = lb.reshape(2, HG_HEADS, HG_KDIM)
    f = lbh + (1.0 - lbh) * jax.nn.sigmoid(hf.reshape(b, length, 2, HG_HEADS, HG_KDIM))
    log_f = jnp.log(f)
    k_in = 1.0 - f
    o_f, g_f = _hgrn_scan(q_h, k_in[:, :, 0], v_h, log_f[:, :, 0], hg_init[:, 0])
    o_b, g_b = _hgrn_scan(_flip(q_h), _flip(k_in[:, :, 1]), _flip(v_h), _flip(log_f[:, :, 1]), hg_init[:, 1])
    o = _rmsnorm(o_f + _flip(o_b), lp['hg_norm']).reshape(b, length, HG_DIM)
    o = o * jax.nn.silu(hg)

    aq = aq.reshape(b, length, ATT_HEADS, ATT_HEAD_DIM)
    ak = ak.reshape(b, length, ATT_HEADS, ATT_HEAD_DIM)
    av = av.reshape(b, length, ATT_HEADS, ATT_HEAD_DIM)
    if ctx is None:
        o_att = _context_attention(aq, ak, av)
    else:
        o_att = _natten_latent(aq, ak, av, k_ctx, v_ctx, lp['att_rpb'])
    o_att = o_att.reshape(b, length, ATT_DIM)

    n = b * length
    mixed = (y.reshape(n, SSD_DIM), o.reshape(n, HG_DIM), o_att.reshape(n, ATT_DIM))
    if ctx is None:
        return mixed, (ak, av, jnp.stack([s_f, s_b], axis=1), jnp.stack([g_f, g_b], axis=1))
    return mixed, None


def _permute_w_in(w):
    c0 = SSD_DIM + SSD_CONV_DIM
    c1 = c0 + 2 * SSD_HEADS
    dt_cols = jnp.pad(w[:, c0:c1], ((0, 0), (0, DT_PAD - 2 * SSD_HEADS)))
    return jnp.concatenate([w[:, :c0], w[:, c1:], dt_cols], axis=1).astype(jnp.bfloat16)


def kernel(x_prompt, x_sample, cache_k, cache_v, state_ssd, state_hgrn, c, c_ctx, w_ada, b_ada, norm_mix, norm_moe, w_in, ssd_conv_w, ssd_conv_b, ssd_dt_bias, ssd_a_log, ssd_d, ssd_norm, hg_lb_logits, hg_norm, att_rpb, w_br_ssd, w_br_hg, w_br_att, w_out, w_router, router_bias, moe_w1, moe_w3, moe_w2, final_norm):
    bf16 = jnp.bfloat16
    lb_cum = jnp.cumsum(jax.nn.softmax(hg_lb_logits.astype(jnp.float32), axis=1), axis=1)
    lower_bounds = lb_cum - lb_cum[:, :1]

    x = jnp.concatenate([x_prompt.reshape(N_CTX, D_MODEL), x_sample.reshape(N_LAT, D_MODEL)], axis=0)

    n_mod = 1 + DEC_BATCH
    cond = jnp.concatenate([c_ctx[None, :], c], axis=0)
    cond = jnp.pad(jax.nn.silu(cond), ((0, 16 - n_mod), (0, 0)))

    wr = jnp.pad(w_router, ((0, 0), (0, ROUTER_PAD - N_EXPERTS)))
    wr_hi = wr.astype(bf16)
    wr_lo = (wr - wr_hi.astype(jnp.float32)).astype(bf16)

    new_k, new_v, new_ssd, new_hg = [], [], [], []
    for l in range(DEPTH):
        lp = {'ssd_conv_w': ssd_conv_w[l], 'ssd_conv_b': ssd_conv_b[l],
              'ssd_dt_bias': ssd_dt_bias[l], 'ssd_a_log': ssd_a_log[l], 'ssd_d': ssd_d[l],
              'ssd_norm': ssd_norm[l], 'hg_norm': hg_norm[l], 'att_rpb': att_rpb[l]}
        lb = lower_bounds[:, l]

        mod = _matmul(cond, w_ada[l].astype(bf16), 16, 1024)[:n_mod] + b_ada[l]
        mod = mod.reshape(n_mod, 6, 1, D_MODEL)
        shift_m, scale_m, gate_m, shift_f, scale_f, gate_f = (mod[:, i] for i in range(6))

        proj = _norm_matmul(x, norm_mix[l][None, :], scale_m, shift_m, _permute_w_in(w_in[l]), 1024)

        mixed_ctx, ctx_t = _mixer_group(proj[:N_CTX].reshape(BATCH, SEQ, PROJ_DIM), lp, lb, None)
        mixed_lat, _ = _mixer_group(proj[N_CTX:].reshape(DEC_BATCH, DEC_SEQ, PROJ_DIM), lp, lb,
                                    (cache_k[:, l], cache_v[:, l], state_ssd[:, l], state_hgrn[:, l]))
        new_k.append(ctx_t[0])
        new_v.append(ctx_t[1])
        new_ssd.append(ctx_t[2])
        new_hg.append(ctx_t[3])

        y_ssd, y_hg, y_att = (jnp.concatenate([a, b], axis=0) for a, b in zip(mixed_ctx, mixed_lat))
        g = jax.nn.sigmoid(proj[:, OFF_GATES:OFF_GATES + N_BRANCH * D_MODEL])
        merged = (g[:, :D_MODEL] * _matmul(y_ssd, w_br_ssd[l].astype(bf16), ROW_TILE, 1024)
                  + g[:, D_MODEL:2 * D_MODEL] * _matmul(y_hg, w_br_hg[l].astype(bf16), ROW_TILE, 1024)
                  + g[:, 2 * D_MODEL:] * _matmul(y_att, w_br_att[l].astype(bf16), ROW_TILE, 1024))
        mix = _matmul(merged, w_out[l].astype(bf16), ROW_TILE, 1024)

        def per_token(m):
            m = m[:, 0]
            return jnp.concatenate([jnp.broadcast_to(m[:1], (N_CTX, D_MODEL)),
                                    jnp.repeat(m[1:], DEC_SEQ, axis=0)], axis=0)

        x = x + per_token(gate_m) * mix

        h2, logits = _norm_router(x, norm_moe[l][None, :], scale_f, shift_f, wr_hi, wr_lo)
        moe = _moe(h2, logits, router_bias, moe_w1[l].astype(bf16), moe_w3[l].astype(bf16),
                   moe_w2[l].astype(bf16))
        x = x + per_token(gate_f) * moe

    y = _final_norm(x, final_norm[None, :])
    y_prompt = y[:N_CTX].reshape(BATCH, SEQ, D_MODEL)
    y_sample = y[N_CTX:].reshape(DEC_BATCH, DEC_SEQ, D_MODEL)
    return (y_prompt, y_sample, jnp.stack(new_k, axis=1), jnp.stack(new_v, axis=1),
            jnp.stack(new_ssd, axis=1), jnp.stack(new_hg, axis=1))
```

```python
import functools

import jax
import jax.numpy as jnp
from jax import lax
from jax.experimental import pallas as pl
from jax.experimental.pallas import tpu as pltpu

D_MODEL = 2048
BATCH = 32
SEQ = 256
DEPTH = 2
DEC_BATCH = 8
DEC_SEQ = 4096
GRID_W = 64
NORM_EPS = 1e-6
SSD_HEADS = 16
SSD_HEAD_DIM = 64
SSD_DIM = SSD_HEADS * SSD_HEAD_DIM
SSD_STATE = 64
SSD_GROUPS = 4
SSD_CONV = 5
SSD_CONV_DIM = SSD_DIM + 2 * SSD_GROUPS * SSD_STATE
HG_HEADS = 8
HG_KDIM = 128
HG_VDIM = 128
HG_FDIM = HG_HEADS * HG_KDIM
HG_DIM = HG_HEADS * HG_VDIM
ATT_HEADS = 8
ATT_HEAD_DIM = 128
ATT_DIM = ATT_HEADS * ATT_HEAD_DIM
WIN_ROWS = 8
WIN_COLS = 16
N_BRANCH = 3
N_EXPERTS = 16
N_EXPERT_GROUPS = 4
EXPERTS_PER_GROUP = N_EXPERTS // N_EXPERT_GROUPS
TOP_K = 2
D_EXPERT = 1024

N_CTX = BATCH * SEQ
N_LAT = DEC_BATCH * DEC_SEQ
N_TOK = N_CTX + N_LAT

VMEM_LIMIT_BYTES = 56 * 1024 * 1024
LANES = 128
SUBLANES = 8

OFF_Z = 0
OFF_XBC = OFF_Z + SSD_DIM
OFF_HQ = OFF_XBC + SSD_CONV_DIM
OFF_HF = OFF_HQ + HG_FDIM
OFF_HI = OFF_HF + 2 * HG_FDIM
OFF_HG = OFF_HI + HG_DIM
OFF_AQ = OFF_HG + HG_DIM
OFF_AK = OFF_AQ + ATT_DIM
OFF_AV = OFF_AK + ATT_DIM
OFF_GATES = OFF_AV + ATT_DIM
OFF_DT = OFF_GATES + N_BRANCH * D_MODEL
DT_PAD = 512
PROJ_DIM = OFF_DT + DT_PAD

ROW_TILE = 1024
COL_TILE = 1024
MERGE_COLS = 512
NORM_ROWS = 64
ROUTER_TILE = 512
MOE_BLOCK = 512
CONV_ROWS = 256
CONV_COLS = 512
SSD_C = 128
FINISH_ROWS = 256
HG_C = 128
HG_LEVELS = (16, 32, 64, 128)
CAST_ROWS = 512
MASKED = -1e30

_NT = (((1,), (1,)), ((), ()))
_TN = (((0,), (0,)), ((), ()))


def _params(*semantics):
    return pltpu.CompilerParams(dimension_semantics=semantics, vmem_limit_bytes=VMEM_LIMIT_BYTES)


def _mod_row(i):
    ctx_tiles = N_CTX // ROW_TILE
    tiles_per_req = DEC_SEQ // ROW_TILE
    return jnp.where(i < ctx_tiles, 0, 1 + (i - ctx_tiles) // tiles_per_req)


def _split3(x):
    bf16, f32 = jnp.bfloat16, jnp.float32
    x1 = x.astype(bf16)
    r = x - x1.astype(f32)
    x2 = r.astype(bf16)
    x3 = (r - x2.astype(f32)).astype(bf16)
    return x1, x2, x3


def _tri_cumsum(tri, x):
    x1, x2, x3 = _split3(x)
    f32 = jnp.float32
    return (jnp.dot(tri, x1, preferred_element_type=f32)
            + (jnp.dot(tri, x2, preferred_element_type=f32) + jnp.dot(tri, x3, preferred_element_type=f32)))


def _mm_kernel(x_ref, w_ref, o_ref):
    o_ref[...] = jnp.dot(x_ref[...].astype(jnp.bfloat16), w_ref[...],
                         preferred_element_type=jnp.float32).astype(o_ref.dtype)


def _matmul(x, w, tm, tn):
    m, k = x.shape
    n = w.shape[1]
    return pl.pallas_call(
        _mm_kernel,
        grid=(m // tm, n // tn),
        in_specs=[pl.BlockSpec((tm, k), lambda i, j: (i, 0)),
                  pl.BlockSpec((k, tn), lambda i, j: (0, j))],
        out_specs=pl.BlockSpec((tm, tn), lambda i, j: (i, j)),
        out_shape=jax.ShapeDtypeStruct((m, n), jnp.float32),
        compiler_params=_params("arbitrary", "arbitrary"),
        name="matmul",
    )(x, w)


def _modulated_norm(x_ref, g_ref, sc_ref, sh_ref, store):
    g = g_ref[...]
    sc = 1.0 + sc_ref[...]
    sh = sh_ref[...]

    def body(r, carry):
        rows = pl.ds(pl.multiple_of(r * NORM_ROWS, NORM_ROWS), NORM_ROWS)
        x = x_ref[rows, :]
        y = x * lax.rsqrt(jnp.mean(x * x, axis=-1, keepdims=True) + NORM_EPS)
        store(rows, (y * g) * sc + sh)
        return carry

    lax.fori_loop(0, x_ref.shape[0] // NORM_ROWS, body, 0)


def _norm_mm_kernel(x_ref, g_ref, sc_ref, sh_ref, w_ref, o_ref, h_ref):
    @pl.when(pl.program_id(1) == 0)
    def _():
        def store(rows, h):
            h_ref[rows, :] = h.astype(jnp.bfloat16)
        _modulated_norm(x_ref, g_ref, sc_ref, sh_ref, store)

    o_ref[...] = jnp.dot(h_ref[...], w_ref[...], preferred_element_type=jnp.float32)


def _norm_matmul(x, gain, scale, shift, w):
    n = w.shape[1]
    mod_spec = pl.BlockSpec((None, 1, D_MODEL), lambda i, j: (_mod_row(i), 0, 0))
    return pl.pallas_call(
        _norm_mm_kernel,
        grid=(N_TOK // ROW_TILE, n // COL_TILE),
        in_specs=[pl.BlockSpec((ROW_TILE, D_MODEL), lambda i, j: (i, 0)),
                  pl.BlockSpec((1, D_MODEL), lambda i, j: (0, 0)),
                  mod_spec, mod_spec,
                  pl.BlockSpec((D_MODEL, COL_TILE), lambda i, j: (0, j))],
        out_specs=pl.BlockSpec((ROW_TILE, COL_TILE), lambda i, j: (i, j)),
        out_shape=jax.ShapeDtypeStruct((N_TOK, n), jnp.float32),
        scratch_shapes=[pltpu.VMEM((ROW_TILE, D_MODEL), jnp.bfloat16)],
        compiler_params=_params("arbitrary", "arbitrary"),
        name="norm_in_proj",
    )(x, gain, scale, shift, w)


def _conv_kernel(prev_ref, x_ref, next_ref, w_ref, b_ref, o_ref, ext_ref, *, tiles_per_seq):
    i = pl.program_id(0)
    t = CONV_ROWS
    pad = SSD_CONV // 2
    first = (i % tiles_per_seq) == 0
    last = (i % tiles_per_seq) == tiles_per_seq - 1
    ext_ref[0:SUBLANES, :] = jnp.where(first, 0.0, prev_ref[...])
    ext_ref[SUBLANES:SUBLANES + t, :] = x_ref[...]
    ext_ref[SUBLANES + t:2 * SUBLANES + t, :] = jnp.where(last, 0.0, next_ref[...])
    y = jnp.broadcast_to(b_ref[...], (t, CONV_COLS))
    for j in range(SSD_CONV):
        y = y + ext_ref[SUBLANES - pad + j:SUBLANES - pad + j + t, :] * w_ref[j:j + 1, :]
    o_ref[...] = y * jax.nn.sigmoid(y)


def _ssd_conv(proj, row0, n_rows, seq_len, w, b):
    t = CONV_ROWS
    r0 = row0 // t
    c0 = OFF_XBC // CONV_COLS
    sub = t // SUBLANES
    n_sub = proj.shape[0] // SUBLANES
    return pl.pallas_call(
        functools.partial(_conv_kernel, tiles_per_seq=seq_len // t),
        grid=(n_rows // t, SSD_CONV_DIM // CONV_COLS),
        in_specs=[pl.BlockSpec((SUBLANES, CONV_COLS), lambda i, j: (jnp.maximum((r0 + i) * sub - 1, 0), c0 + j)),
                  pl.BlockSpec((t, CONV_COLS), lambda i, j: (r0 + i, c0 + j)),
                  pl.BlockSpec((SUBLANES, CONV_COLS),
                               lambda i, j: (jnp.minimum((r0 + i + 1) * sub, n_sub - 1), c0 + j)),
                  pl.BlockSpec((SSD_CONV, CONV_COLS), lambda i, j: (0, j)),
                  pl.BlockSpec((1, CONV_COLS), lambda i, j: (0, j))],
        out_specs=pl.BlockSpec((t, CONV_COLS), lambda i, j: (i, j)),
        out_shape=jax.ShapeDtypeStruct((n_rows, SSD_CONV_DIM), jnp.float32),
        scratch_shapes=[pltpu.VMEM((t + 2 * SUBLANES, CONV_COLS), jnp.float32)],
        compiler_params=_params("arbitrary", "arbitrary"),
        name="ssd_conv",
    )(proj, proj, proj, w, b)


def _softplus(x):
    return jnp.maximum(x, 0.0) + jnp.log1p(jnp.exp(-jnp.abs(x)))


def _ssd_kernel(xf_ref, xb_ref, dtf_ref, dtb_ref, dtbias_ref, aneg_ref, init_ref, yf_ref, yb_ref, st_ref,
                s_ref, *, nc):
    bf16, f32 = jnp.bfloat16, jnp.float32
    c = SSD_C
    hd, ns = SSD_HEAD_DIM, SSD_STATE
    j = pl.program_id(1)

    @pl.when(j == 0)
    def _():
        s_ref[...] = init_ref[...]

    t_ids = lax.broadcasted_iota(jnp.int32, (c, c), 0)
    s_ids = lax.broadcasted_iota(jnp.int32, (c, c), 1)
    eye = (lax.broadcasted_iota(jnp.int32, (SSD_HEADS, SSD_HEADS), 0)
           == lax.broadcasted_iota(jnp.int32, (SSD_HEADS, SSD_HEADS), 1)).astype(bf16)

    for d, (x_ref, dt_ref, y_ref) in enumerate(((xf_ref, dtf_ref, yf_ref), (xb_ref, dtb_ref, yb_ref))):
        rev = d == 1
        causal = (s_ids >= t_ids) if rev else (s_ids <= t_ids)
        tri = causal.astype(bf16)
        dt = _softplus(dt_ref[:, d * SSD_HEADS:(d + 1) * SSD_HEADS] + dtbias_ref[d:d + 1, :])
        acs = _tri_cumsum(tri, dt * aneg_ref[d:d + 1, :])
        acs_t = sum(lax.dot_general(eye, part, _NT, preferred_element_type=f32) for part in _split3(acs))
        end = acs[0:1, :] if rev else acs[c - 1:c, :]
        for g in range(SSD_GROUPS):
            bg = x_ref[:, SSD_DIM + g * ns:SSD_DIM + (g + 1) * ns].astype(bf16)
            cg = x_ref[:, SSD_DIM + (SSD_GROUPS + g) * ns:SSD_DIM + (SSD_GROUPS + g + 1) * ns].astype(bf16)
            cb = lax.dot_general(cg, bg, _NT, preferred_element_type=f32)
            for h in range(g * (SSD_HEADS // SSD_GROUPS), (g + 1) * (SSD_HEADS // SSD_GROUPS)):
                col = acs[:, h:h + 1]
                decay = jnp.exp(jnp.minimum(col - acs_t[h:h + 1, :], 0.0))
                m = jnp.where(causal, cb * decay, 0.0).astype(bf16)
                xdt = x_ref[:, h * hd:(h + 1) * hd] * dt[:, h:h + 1]
                s_h = s_ref[d, h]
                y = (jnp.dot(m, xdt.astype(bf16), preferred_element_type=f32)
                     + lax.dot_general(cg, s_h.astype(bf16), _NT, preferred_element_type=f32) * jnp.exp(col))
                y_ref[:, h * hd:(h + 1) * hd] = y
                end_h = end[:, h:h + 1]
                s_ref[d, h] = (s_h * jnp.exp(end_h)
                               + lax.dot_general((xdt * jnp.exp(end_h - col)).astype(bf16), bg, _TN,
                                                 preferred_element_type=f32))

    @pl.when(j == nc - 1)
    def _():
        st_ref[...] = s_ref[...]


def _ssd_scan(xbc, proj, row0, nb, length, dt_bias, a_neg, init):
    c = SSD_C
    nc = length // c
    r0 = row0 // c
    cdt = OFF_DT // LANES
    state_spec = pl.BlockSpec((None, 2, SSD_HEADS, SSD_HEAD_DIM, SSD_STATE), lambda b, j: (b, 0, 0, 0, 0))
    return pl.pallas_call(
        functools.partial(_ssd_kernel, nc=nc),
        grid=(nb, nc),
        in_specs=[pl.BlockSpec((c, SSD_CONV_DIM), lambda b, j: (b * nc + j, 0)),
                  pl.BlockSpec((c, SSD_CONV_DIM), lambda b, j: (b * nc + nc - 1 - j, 0)),
                  pl.BlockSpec((c, LANES), lambda b, j: (r0 + b * nc + j, cdt)),
                  pl.BlockSpec((c, LANES), lambda b, j: (r0 + b * nc + nc - 1 - j, cdt)),
                  pl.BlockSpec((2, SSD_HEADS), lambda b, j: (0, 0)),
                  pl.BlockSpec((2, SSD_HEADS), lambda b, j: (0, 0)),
                  state_spec],
        out_specs=[pl.BlockSpec((c, SSD_DIM), lambda b, j: (b * nc + j, 0)),
                   pl.BlockSpec((c, SSD_DIM), lambda b, j: (b * nc + nc - 1 - j, 0)),
                   state_spec],
        out_shape=[jax.ShapeDtypeStruct((nb * length, SSD_DIM), jnp.float32),
                   jax.ShapeDtypeStruct((nb * length, SSD_DIM), jnp.float32),
                   jax.ShapeDtypeStruct((nb, 2, SSD_HEADS, SSD_HEAD_DIM, SSD_STATE), jnp.float32)],
        scratch_shapes=[pltpu.VMEM((2, SSD_HEADS, SSD_HEAD_DIM, SSD_STATE), jnp.float32)],
        compiler_params=_params("arbitrary", "arbitrary"),
        name="ssd_scan",
    )(xbc, xbc, proj, proj, dt_bias, a_neg, init)


def _ssd_finish_kernel(yf_ref, yb_ref, x_ref, z_ref, d_ref, g_ref, o_ref):
    z = z_ref[...]
    y = (yf_ref[...] + yb_ref[...] + d_ref[...] * x_ref[...]) * (z * jax.nn.sigmoid(z))
    o_ref[...] = (y * lax.rsqrt(jnp.mean(y * y, axis=-1, keepdims=True) + NORM_EPS) * g_ref[...]).astype(o_ref.dtype)


def _ssd_finish(y_f, y_b, xbc, proj, row0, d_row, gain):
    n = y_f.shape[0]
    t = FINISH_ROWS
    r0 = row0 // t
    row_spec = pl.BlockSpec((t, SSD_DIM), lambda i: (i, 0))
    vec_spec = pl.BlockSpec((1, SSD_DIM), lambda i: (0, 0))
    return pl.pallas_call(
        _ssd_finish_kernel,
        grid=(n // t,),
        in_specs=[row_spec, row_spec, row_spec,
                  pl.BlockSpec((t, SSD_DIM), lambda i: (r0 + i, OFF_Z // SSD_DIM)),
                  vec_spec, vec_spec],
        out_specs=row_spec,
        out_shape=jax.ShapeDtypeStruct((n, SSD_DIM), jnp.bfloat16),
        compiler_params=_params("arbitrary"),
        name="ssd_finish",
    )(y_f, y_b, xbc, proj, d_row, gain)


def _hgrn_chunk(q, g, kk, v, s_t, rev, tri, lane_ids, causal, level_masks):
    bf16, f32 = jnp.bfloat16, jnp.float32
    c = q.shape[0]
    ng = c // SUBLANES
    b = _tri_cumsum(tri, g)
    row = lambda a, r: a[r:r + 1, :]
    grp = lambda a, i: a[i * SUBLANES:(i + 1) * SUBLANES, :]
    b_end = row(b, 0) if rev else row(b, c - 1)

    ones = jnp.ones((HG_KDIM, c), bf16)
    diag_rows = []
    for i in range(ng):
        qg, bg = grp(q, i), grp(b, i)
        tiles = []
        for j in range(SUBLANES):
            r = i * SUBLANES + j
            tiles.append(qg * jnp.exp(jnp.minimum(bg - row(b, r), 0.0)) * row(kk, r))
        sums = jnp.dot(jnp.concatenate(tiles, axis=0).astype(bf16), ones, preferred_element_type=f32)
        acc = jnp.zeros((SUBLANES, c), f32)
        for j in range(SUBLANES):
            acc = acc + jnp.where(lane_ids[:SUBLANES] == i * SUBLANES + j,
                                  sums[j * SUBLANES:(j + 1) * SUBLANES, :], 0.0)
        diag_rows.append(acc)
    att = jnp.where(causal, jnp.concatenate(diag_rows, axis=0), 0.0)

    for m, mask in zip(HG_LEVELS, level_masks):
        half = m // 2
        q_side, k_side = [], []
        for i in range(ng):
            start = (i * SUBLANES) // m * m
            later = (i * SUBLANES) % m >= half
            ref = row(b, start + half if rev else start + half - 1)
            if later != rev:
                q_side.append(grp(q, i) * jnp.exp(grp(b, i) - ref))
                k_side.append(jnp.zeros((SUBLANES, HG_KDIM), f32))
            else:
                q_side.append(jnp.zeros((SUBLANES, HG_KDIM), f32))
                k_side.append(grp(kk, i) * jnp.exp(ref - grp(b, i)))
        a_m = lax.dot_general(jnp.concatenate(q_side, axis=0).astype(bf16),
                              jnp.concatenate(k_side, axis=0).astype(bf16), _NT, preferred_element_type=f32)
        att = att + (a_m if m == c else jnp.where(mask, a_m, 0.0))

    q_in = (q * jnp.exp(b)).astype(bf16)
    o = (lax.dot_general(q_in, s_t.astype(bf16), _NT, preferred_element_type=f32)
         + jnp.dot(att.astype(bf16), v.astype(bf16), preferred_element_type=f32))
    k_out = (kk * jnp.exp(b_end - b)).astype(bf16)
    s_new = s_t * jnp.exp(b_end) + lax.dot_general(v.astype(bf16), k_out, _TN, preferred_element_type=f32)
    return o, s_new


def _hgrn_kernel(q_ref, ff_ref, fb_ref, v_ref, gate_ref, lb_ref, gn_ref, init_ref, o_ref, st_ref,
                 acc_ref, s_ref, *, nc):
    c = HG_C
    t_ids = lax.broadcasted_iota(jnp.int32, (c, c), 0)
    s_ids = lax.broadcasted_iota(jnp.int32, (c, c), 1)
    causal = (s_ids <= t_ids, s_ids >= t_ids)
    tri = tuple(m.astype(jnp.bfloat16) for m in causal)

    def level_mask(m, rev):
        same = (t_ids // m) == (s_ids // m)
        t_late = (t_ids % m) >= m // 2
        s_late = (s_ids % m) >= m // 2
        return same & (t_late != s_late) & (t_late != rev)
    masks = tuple([level_mask(m, rev) for m in HG_LEVELS] for rev in (False, True))

    s_ref[0] = init_ref[0].T
    s_ref[1] = init_ref[1].T
    gain = gn_ref[...]

    def run(chunk, d):
        rows = pl.ds(pl.multiple_of(chunk * c, c), c)
        x = q_ref[rows, :]
        q = x * jax.nn.sigmoid(x)
        lb = lb_ref[d:d + 1, :]
        f = lb + (1.0 - lb) * jax.nn.sigmoid((ff_ref, fb_ref)[d][rows, :])
        o, s_new = _hgrn_chunk(q, jnp.log(f), 1.0 - f, v_ref[rows, :], s_ref[d], d == 1,
                               tri[d], s_ids, causal[d], masks[d])
        s_ref[d] = s_new
        return rows, o

    def first_half(j, carry):
        for d, chunk in ((0, j), (1, nc - 1 - j)):
            rows, o = run(chunk, d)
            acc_ref[rows, :] = o
        return carry

    def second_half(j, carry):
        for d, chunk in ((0, j), (1, nc - 1 - j)):
            rows, o = run(chunk, d)
            o = o + acc_ref[rows, :]
            y = o * lax.rsqrt(jnp.mean(o * o, axis=-1, keepdims=True) + NORM_EPS) * gain
            gate = gate_ref[rows, :]
            o_ref[rows, :] = (y * (gate * jax.nn.sigmoid(gate))).astype(o_ref.dtype)
        return carry

    lax.fori_loop(0, nc // 2, first_half, 0)
    lax.fori_loop(nc // 2, nc, second_half, 0)
    st_ref[0] = s_ref[0].T
    st_ref[1] = s_ref[1].T


def _hgrn_mixer(proj, row0, nb, length, lb, gain, init):
    nc = length // HG_C
    assert nc % 2 == 0
    seq = lambda off: pl.BlockSpec((length, HG_KDIM), lambda b, h, col=off // HG_KDIM: (row0 + b, col + h))
    state_spec = pl.BlockSpec((None, 2, None, HG_KDIM, HG_VDIM), lambda b, h: (b, 0, h, 0, 0))
    return pl.pallas_call(
        functools.partial(_hgrn_kernel, nc=nc),
        grid=(nb, HG_HEADS),
        in_specs=[seq(OFF_HQ), seq(OFF_HF), seq(OFF_HF + HG_FDIM), seq(OFF_HI), seq(OFF_HG),
                  pl.BlockSpec((2, HG_KDIM), lambda b, h: (0, h)),
                  pl.BlockSpec((1, HG_VDIM), lambda b, h: (0, h)),
                  state_spec],
        out_specs=[pl.BlockSpec((length, HG_VDIM), lambda b, h: (b, h)), state_spec],
        out_shape=[jax.ShapeDtypeStruct((nb * length, HG_DIM), jnp.bfloat16),
                   jax.ShapeDtypeStruct((nb, 2, HG_HEADS, HG_KDIM, HG_VDIM), jnp.float32)],
        scratch_shapes=[pltpu.VMEM((length, HG_VDIM), jnp.float32),
                        pltpu.VMEM((2, HG_VDIM, HG_KDIM), jnp.float32)],
        compiler_params=_params("arbitrary", "arbitrary"),
        name="hgrn_mixer",
    )(proj, proj, proj, proj, proj, lb, gain, init)


def _softmax_av(scores, values):
    f32, bf16 = jnp.float32, jnp.bfloat16
    m = functools.reduce(jnp.maximum, [jnp.max(s, axis=-1, keepdims=True) for s in scores])
    ps = [jnp.exp(s - m) for s in scores]
    denom = functools.reduce(jnp.add, [jnp.sum(p, axis=-1, keepdims=True) for p in ps])
    acc = functools.reduce(jnp.add, [jnp.dot(p.astype(bf16), v, preferred_element_type=f32)
                                     for p, v in zip(ps, values)])
    return acc / denom


def _ctx_attn_kernel(q_ref, k_ref, v_ref, o_ref):
    bf16 = jnp.bfloat16
    scale = ATT_HEAD_DIM ** -0.5
    s = lax.dot_general(q_ref[...].astype(bf16), k_ref[...].astype(bf16), _NT,
                        preferred_element_type=jnp.float32) * scale
    o_ref[...] = _softmax_av([s], [v_ref[...].astype(bf16)]).astype(o_ref.dtype)


def _context_attention(proj, nb, length):
    spec = lambda off: pl.BlockSpec((length, ATT_HEAD_DIM), lambda b, h, col=off // ATT_HEAD_DIM: (b, col + h))
    return pl.pallas_call(
        _ctx_attn_kernel,
        grid=(nb, ATT_HEADS),
        in_specs=[spec(OFF_AQ), spec(OFF_AK), spec(OFF_AV)],
        out_specs=pl.BlockSpec((length, ATT_HEAD_DIM), lambda b, h: (b, h)),
        out_shape=jax.ShapeDtypeStruct((nb * length, ATT_DIM), jnp.bfloat16),
        compiler_params=_params("arbitrary", "arbitrary"),
        name="context_attention",
    )(proj, proj, proj)


def _window_bias(rpb):
    col = jnp.arange(GRID_W)
    cs = jnp.clip(col - WIN_COLS // 2, 0, GRID_W - WIN_COLS)
    col_mask = (col[None, :] >= cs[:, None]) & (col[None, :] < cs[:, None] + WIN_COLS)
    dc_idx = jnp.clip(col[None, :] - col[:, None] + WIN_COLS - 1, 0, 2 * WIN_COLS - 2)
    bias = jnp.where(col_mask, rpb[:, :, dc_idx].astype(jnp.float32), MASKED)
    wins = [bias[:, d0:d0 + WIN_ROWS].transpose(0, 2, 1, 3).reshape(rpb.shape[0], GRID_W, WIN_ROWS * GRID_W)
            for d0 in range(WIN_ROWS)]
    return jnp.stack(wins, axis=1)


def _natten_kernel(q_ref, k_ref, v_ref, kc_ref, vc_ref, bias_ref, o_ref, kb_ref, vb_ref, *, rows):
    bf16, f32 = jnp.bfloat16, jnp.float32
    scale = ATT_HEAD_DIM ** -0.5
    win = WIN_ROWS * GRID_W

    def cast(i, carry):
        sl = pl.ds(pl.multiple_of(i * CAST_ROWS, CAST_ROWS), CAST_ROWS)
        kb_ref[sl, :] = k_ref[sl, :].astype(bf16)
        vb_ref[sl, :] = v_ref[sl, :].astype(bf16)
        return carry
    lax.fori_loop(0, rows * GRID_W // CAST_ROWS, cast, 0)

    kc = kc_ref[...].astype(bf16)
    vc = vc_ref[...].astype(bf16)

    def row_block(r, carry):
        rs = jnp.clip(r - WIN_ROWS // 2, 0, rows - WIN_ROWS)
        d0 = rs - r + WIN_ROWS - 1
        q = q_ref[pl.ds(pl.multiple_of(r * GRID_W, GRID_W), GRID_W), :].astype(bf16)
        keys = pl.ds(pl.multiple_of(rs * GRID_W, GRID_W), win)
        s_lat = lax.dot_general(q, kb_ref[keys, :], _NT, preferred_element_type=f32) * scale + bias_ref[d0]
        s_ctx = lax.dot_general(q, kc, _NT, preferred_element_type=f32) * scale
        o = _softmax_av([s_lat, s_ctx], [vb_ref[keys, :], vc])
        o_ref[pl.ds(pl.multiple_of(r * GRID_W, GRID_W), GRID_W), :] = o.astype(o_ref.dtype)
        return carry
    lax.fori_loop(0, rows, row_block, 0)


def _neighbourhood_attention(proj, row0, nb, length, cache_k, cache_v, layer, bias_win):
    rows = length // GRID_W
    past = cache_k.shape[2]
    spec = lambda off: pl.BlockSpec((length, ATT_HEAD_DIM),
                                    lambda b, h, col=off // ATT_HEAD_DIM: (row0 + b, col + h))
    cache_spec = pl.BlockSpec((None, None, past, ATT_HEAD_DIM), lambda b, h: (b, layer, 0, h))
    return pl.pallas_call(
        functools.partial(_natten_kernel, rows=rows),
        grid=(nb, ATT_HEADS),
        in_specs=[spec(OFF_AQ), spec(OFF_AK), spec(OFF_AV), cache_spec, cache_spec,
                  pl.BlockSpec((None, WIN_ROWS, GRID_W, WIN_ROWS * GRID_W), lambda b, h: (h, 0, 0, 0))],
        out_specs=pl.BlockSpec((length, ATT_HEAD_DIM), lambda b, h: (b, h)),
        out_shape=jax.ShapeDtypeStruct((nb * length, ATT_DIM), jnp.bfloat16),
        scratch_shapes=[pltpu.VMEM((length, ATT_HEAD_DIM), jnp.bfloat16),
                        pltpu.VMEM((length, ATT_HEAD_DIM), jnp.bfloat16)],
        compiler_params=_params("arbitrary", "arbitrary"),
        name="neighbourhood_attention",
    )(proj, proj, proj, cache_k, cache_v, bias_win)


def _merge_kernel(ya_ref, yb_ref, yc_ref, wa_ref, wb_ref, wc_ref, ga_ref, gb_ref, gc_ref, o_ref):
    f32 = jnp.float32
    acc = jax.nn.sigmoid(ga_ref[...]) * jnp.dot(ya_ref[...], wa_ref[...], preferred_element_type=f32)
    acc = acc + jax.nn.sigmoid(gb_ref[...]) * jnp.dot(yb_ref[...], wb_ref[...], preferred_element_type=f32)
    acc = acc + jax.nn.sigmoid(gc_ref[...]) * jnp.dot(yc_ref[...], wc_ref[...], preferred_element_type=f32)
    o_ref[...] = acc.astype(o_ref.dtype)


def _branch_merge(ys, ws, proj):
    tm, tn = ROW_TILE, MERGE_COLS
    kdim = ys[0].shape[1]
    cg = OFF_GATES // tn
    per = D_MODEL // tn
    y_spec = pl.BlockSpec((tm, kdim), lambda i, j: (i, 0))
    w_spec = pl.BlockSpec((kdim, tn), lambda i, j: (0, j))
    g_spec = lambda b: pl.BlockSpec((tm, tn), lambda i, j, b=b: (i, cg + b * per + j))
    return pl.pallas_call(
        _merge_kernel,
        grid=(N_TOK // tm, D_MODEL // tn),
        in_specs=[y_spec, y_spec, y_spec, w_spec, w_spec, w_spec, g_spec(0), g_spec(1), g_spec(2)],
        out_specs=pl.BlockSpec((tm, tn), lambda i, j: (i, j)),
        out_shape=jax.ShapeDtypeStruct((N_TOK, D_MODEL), jnp.bfloat16),
        compiler_params=_params("arbitrary", "arbitrary"),
        name="branch_merge",
    )(*ys, *ws, proj, proj, proj)


def _out_residual_kernel(m_ref, w_ref, x_ref, gate_ref, o_ref):
    o_ref[...] = x_ref[...] + gate_ref[...] * jnp.dot(m_ref[...], w_ref[...], preferred_element_type=jnp.float32)


def _out_residual(merged, w, x, gate):
    tm, tn = ROW_TILE, COL_TILE
    return pl.pallas_call(
        _out_residual_kernel,
        grid=(N_TOK // tm, D_MODEL // tn),
        in_specs=[pl.BlockSpec((tm, D_MODEL), lambda i, j: (i, 0)),
                  pl.BlockSpec((D_MODEL, tn), lambda i, j: (0, j)),
                  pl.BlockSpec((tm, tn), lambda i, j: (i, j)),
                  pl.BlockSpec((None, 1, tn), lambda i, j: (_mod_row(i), 0, j))],
        out_specs=pl.BlockSpec((tm, tn), lambda i, j: (i, j)),
        out_shape=jax.ShapeDtypeStruct((N_TOK, D_MODEL), jnp.float32),
        compiler_params=_params("arbitrary", "arbitrary"),
        name="out_residual",
    )(merged, w, x, gate)


def _first_max(vals):
    best, idx = vals[0], jnp.zeros(vals[0].shape, jnp.int32)
    for k in range(1, len(vals)):
        better = vals[k] > best
        best = jnp.where(better, vals[k], best)
        idx = jnp.where(better, k, idx)
    return best, idx


def _pick(idx, vals):
    out = vals[0]
    for k in range(1, len(vals)):
        out = jnp.where(idx == k, vals[k], out)
    return out


def _norm_router_kernel(x_ref, g_ref, sc_ref, sh_ref, wr_hi_ref, wr_lo_ref, rb_ref,
                        h_ref, ids_ref, wts_ref, cnt_ref, lo_ref, base_ref):
    f32, bf16 = jnp.float32, jnp.bfloat16
    t = x_ref.shape[0]

    @pl.when(pl.program_id(0) == 0)
    def _():
        base_ref[...] = jnp.zeros_like(base_ref)

    def store(rows, h):
        h_hi = h.astype(bf16)
        h_ref[rows, :] = h_hi
        lo_ref[rows, :] = (h - h_hi.astype(f32)).astype(bf16)
    _modulated_norm(x_ref, g_ref, sc_ref, sh_ref, store)

    logits = (lax.dot_general(wr_hi_ref[...], h_ref[...], _NT, preferred_element_type=f32)
              + (lax.dot_general(wr_lo_ref[...], h_ref[...], _NT, preferred_element_type=f32)
                 + lax.dot_general(wr_hi_ref[...], lo_ref[...], _NT, preferred_element_type=f32)))
    scores = jax.nn.sigmoid(logits)
    sel = scores + rb_ref[...]
    row = lambda a, e: a[e:e + 1, :]

    group_scores = []
    for g in range(N_EXPERT_GROUPS):
        v = [row(sel, g * EXPERTS_PER_GROUP + k) for k in range(EXPERTS_PER_GROUP)]
        pair_sums = [v[a] + v[b] for a in range(EXPERTS_PER_GROUP) for b in range(a + 1, EXPERTS_PER_GROUP)]
        group_scores.append(functools.reduce(jnp.maximum, pair_sums))
    _, grp = _first_max(group_scores)

    in_sel = [_pick(grp, [row(sel, g * EXPERTS_PER_GROUP + k) for g in range(N_EXPERT_GROUPS)])
              for k in range(EXPERTS_PER_GROUP)]
    in_score = [_pick(grp, [row(scores, g * EXPERTS_PER_GROUP + k) for g in range(N_EXPERT_GROUPS)])
                for k in range(EXPERTS_PER_GROUP)]
    _, i1 = _first_max(in_sel)
    _, i2 = _first_max([jnp.where(i1 == k, -jnp.inf, in_sel[k]) for k in range(EXPERTS_PER_GROUP)])
    s1 = _pick(i1, in_score)
    s2 = _pick(i2, in_score)
    e1 = grp * EXPERTS_PER_GROUP + i1
    e2 = grp * EXPERTS_PER_GROUP + i2

    e_ids = lax.broadcasted_iota(jnp.int32, (N_EXPERTS, t), 0)
    hit1 = e_ids == e1
    hit2 = e_ids == e2
    cnt = jnp.where(hit1 | hit2, 1.0, 0.0).astype(bf16)
    before = (lax.broadcasted_iota(jnp.int32, (t, t), 0) < lax.broadcasted_iota(jnp.int32, (t, t), 1)).astype(bf16)
    prefix = jnp.dot(cnt, before, preferred_element_type=f32) + base_ref[:, 0:1]
    rank1 = jnp.sum(jnp.where(hit1, prefix, 0.0), axis=0, keepdims=True)
    rank2 = jnp.sum(jnp.where(hit2, prefix, 0.0), axis=0, keepdims=True)
    base_ref[...] = base_ref[...] + jnp.dot(cnt, jnp.ones((t, LANES), bf16), preferred_element_type=f32)
    cnt_ref[...] = base_ref[...]

    zeros = jnp.zeros((SUBLANES - 4, t), jnp.int32)
    ids_ref[...] = jnp.concatenate([e1, e2, rank1.astype(jnp.int32), rank2.astype(jnp.int32), zeros], axis=0)
    total = s1 + s2
    wts_ref[...] = jnp.concatenate([s1 / total, s2 / total, jnp.zeros((SUBLANES - 2, t), f32)], axis=0)


def _norm_router(x, gain, scale, shift, wr_hi_t, wr_lo_t, router_bias):
    t = ROUTER_TILE
    per = ROW_TILE // t
    mod_spec = pl.BlockSpec((None, 1, D_MODEL), lambda i: (_mod_row(i // per), 0, 0))
    return pl.pallas_call(
        _norm_router_kernel,
        grid=(N_TOK // t,),
        in_specs=[pl.BlockSpec((t, D_MODEL), lambda i: (i, 0)),
                  pl.BlockSpec((1, D_MODEL), lambda i: (0, 0)),
                  mod_spec, mod_spec,
                  pl.BlockSpec((N_EXPERTS, D_MODEL), lambda i: (0, 0)),
                  pl.BlockSpec((N_EXPERTS, D_MODEL), lambda i: (0, 0)),
                  pl.BlockSpec((N_EXPERTS, 1), lambda i: (0, 0))],
        out_specs=[pl.BlockSpec((t, D_MODEL), lambda i: (i, 0)),
                   pl.BlockSpec((SUBLANES, t), lambda i: (0, i)),
                   pl.BlockSpec((SUBLANES, t), lambda i: (0, i)),
                   pl.BlockSpec((N_EXPERTS, LANES), lambda i: (0, 0))],
        out_shape=[jax.ShapeDtypeStruct((N_TOK, D_MODEL), jnp.bfloat16),
                   jax.ShapeDtypeStruct((SUBLANES, N_TOK), jnp.int32),
                   jax.ShapeDtypeStruct((SUBLANES, N_TOK), jnp.float32),
                   jax.ShapeDtypeStruct((N_EXPERTS, LANES), jnp.float32)],
        scratch_shapes=[pltpu.VMEM((t, D_MODEL), jnp.bfloat16),
                        pltpu.VMEM((N_EXPERTS, LANES), jnp.float32)],
        compiler_params=_params("arbitrary"),
        name="norm_router",
    )(x, gain, scale, shift, wr_hi_t, wr_lo_t, router_bias)


def _expert_kernel(be_ref, na_ref, x_ref, w1_ref, w3_ref, w2_ref, o_ref):
    active = pl.program_id(0) < na_ref[0]

    @pl.when(active)
    def _():
        x = x_ref[...]
        a = jnp.dot(x, w1_ref[...], preferred_element_type=jnp.float32)
        b = jnp.dot(x, w3_ref[...], preferred_element_type=jnp.float32)
        hdn = (a * jax.nn.sigmoid(a)) * b
        o_ref[...] = jnp.dot(hdn.astype(jnp.bfloat16), w2_ref[...], preferred_element_type=jnp.float32)

    @pl.when(jnp.logical_not(active))
    def _():
        o_ref[...] = jnp.zeros_like(o_ref)


def _expert_blocks(buf, block_e, n_active, w1, w3, w2):
    n_blocks = buf.shape[0] // MOE_BLOCK
    grid_spec = pltpu.PrefetchScalarGridSpec(
        num_scalar_prefetch=2,
        grid=(n_blocks,),
        in_specs=[pl.BlockSpec((MOE_BLOCK, D_MODEL), lambda i, be, na: (i, 0)),
                  pl.BlockSpec((None, D_MODEL, D_EXPERT), lambda i, be, na: (be[i], 0, 0)),
                  pl.BlockSpec((None, D_MODEL, D_EXPERT), lambda i, be, na: (be[i], 0, 0)),
                  pl.BlockSpec((None, D_EXPERT, D_MODEL), lambda i, be, na: (be[i], 0, 0))],
        out_specs=pl.BlockSpec((MOE_BLOCK, D_MODEL), lambda i, be, na: (i, 0)),
    )
    return pl.pallas_call(
        _expert_kernel,
        grid_spec=grid_spec,
        out_shape=jax.ShapeDtypeStruct((n_blocks * MOE_BLOCK, D_MODEL), jnp.float32),
        compiler_params=_params("arbitrary"),
        name="moe_experts",
    )(block_e, n_active, buf, w1, w3, w2)


def _moe(h, ids, wts, counts, w1, w3, w2):
    nk = N_TOK * TOP_K
    n_blocks = (nk + N_EXPERTS * (MOE_BLOCK - 1)) // MOE_BLOCK
    counts = counts[:, 0].astype(jnp.int32)
    padded = (counts + MOE_BLOCK - 1) // MOE_BLOCK * MOE_BLOCK
    pad_end = jnp.cumsum(padded)
    pad_start = pad_end - padded
    dest1 = pad_start[ids[0]] + ids[2]
    dest2 = pad_start[ids[1]] + ids[3]
    tok = jnp.arange(N_TOK, dtype=jnp.int32)
    src = jnp.zeros((n_blocks * MOE_BLOCK,), jnp.int32).at[jnp.concatenate([dest1, dest2])].set(
        jnp.concatenate([tok, tok]))
    block_e = jnp.minimum(jnp.searchsorted(pad_end, jnp.arange(n_blocks, dtype=jnp.int32) * MOE_BLOCK, side='right'),
                          N_EXPERTS - 1).astype(jnp.int32)
    n_active = (pad_end[-1:] // MOE_BLOCK).astype(jnp.int32)
    out = _expert_blocks(h[src], block_e, n_active, w1, w3, w2)
    return wts[0][:, None] * out[dest1] + wts[1][:, None] * out[dest2]


def _final_norm_kernel(x_ref, g_ref, o_ref):
    g = g_ref[...]

    def body(r, carry):
        rows = pl.ds(pl.multiple_of(r * NORM_ROWS, NORM_ROWS), NORM_ROWS)
        x = x_ref[rows, :]
        o_ref[rows, :] = (x * lax.rsqrt(jnp.mean(x * x, axis=-1, keepdims=True) + NORM_EPS)) * g
        return carry

    lax.fori_loop(0, x_ref.shape[0] // NORM_ROWS, body, 0)


def _final_norm(x, gain):
    return pl.pallas_call(
        _final_norm_kernel,
        grid=(N_TOK // ROW_TILE,),
        in_specs=[pl.BlockSpec((ROW_TILE, D_MODEL), lambda i: (i, 0)),
                  pl.BlockSpec((1, D_MODEL), lambda i: (0, 0))],
        out_specs=pl.BlockSpec((ROW_TILE, D_MODEL), lambda i: (i, 0)),
        out_shape=jax.ShapeDtypeStruct((N_TOK, D_MODEL), jnp.float32),
        compiler_params=_params("arbitrary"),
        name="final_norm",
    )(x, gain)


def _permute_w_in(w):
    c0 = SSD_DIM + SSD_CONV_DIM
    c1 = c0 + 2 * SSD_HEADS
    dt_cols = jnp.pad(w[:, c0:c1], ((0, 0), (0, DT_PAD - 2 * SSD_HEADS)))
    return jnp.concatenate([w[:, :c0], w[:, c1:], dt_cols], axis=1).astype(jnp.bfloat16)


def _per_token(m):
    m = m[:, 0]
    return jnp.concatenate([jnp.broadcast_to(m[:1], (N_CTX, D_MODEL)), jnp.repeat(m[1:], DEC_SEQ, axis=0)], axis=0)


def kernel(x_prompt, x_sample, cache_k, cache_v, state_ssd, state_hgrn, c, c_ctx, w_ada, b_ada, norm_mix, norm_moe, w_in, ssd_conv_w, ssd_conv_b, ssd_dt_bias, ssd_a_log, ssd_d, ssd_norm, hg_lb_logits, hg_norm, att_rpb, w_br_ssd, w_br_hg, w_br_att, w_out, w_router, router_bias, moe_w1, moe_w3, moe_w2, final_norm):
    bf16, f32 = jnp.bfloat16, jnp.float32
    lb_cum = jnp.cumsum(jax.nn.softmax(hg_lb_logits.astype(f32), axis=1), axis=1)
    lower_bounds = lb_cum - lb_cum[:, :1]

    x = jnp.concatenate([x_prompt.reshape(N_CTX, D_MODEL), x_sample.reshape(N_LAT, D_MODEL)], axis=0)

    n_mod = 1 + DEC_BATCH
    cond = jnp.concatenate([c_ctx[None, :], c], axis=0)
    cond = jnp.pad(jax.nn.silu(cond), ((0, 2 * SUBLANES - n_mod), (0, 0)))

    wr_t = w_router.T
    wr_hi = wr_t.astype(bf16)
    wr_lo = (wr_t - wr_hi.astype(f32)).astype(bf16)
    cache_k = cache_k.reshape(DEC_BATCH, DEPTH, -1, ATT_DIM)
    cache_v = cache_v.reshape(DEC_BATCH, DEPTH, -1, ATT_DIM)
    lat_row0 = N_CTX // DEC_SEQ
    zero_ssd = jnp.zeros((BATCH, 2, SSD_HEADS, SSD_HEAD_DIM, SSD_STATE), f32)
    zero_hg = jnp.zeros((BATCH, 2, HG_HEADS, HG_KDIM, HG_VDIM), f32)

    new_k, new_v, new_ssd, new_hg = [], [], [], []
    for l in range(DEPTH):
        mod = _matmul(cond, w_ada[l].astype(bf16), 2 * SUBLANES, COL_TILE)[:n_mod] + b_ada[l]
        mod = mod.reshape(n_mod, 6, 1, D_MODEL)
        shift_m, scale_m, gate_m, shift_f, scale_f, gate_f = (mod[:, i] for i in range(6))

        proj = _norm_matmul(x, norm_mix[l][None, :], scale_m, shift_m, _permute_w_in(w_in[l]))
        new_k.append(proj[:N_CTX, OFF_AK:OFF_AK + ATT_DIM].reshape(BATCH, SEQ, ATT_HEADS, ATT_HEAD_DIM))
        new_v.append(proj[:N_CTX, OFF_AV:OFF_AV + ATT_DIM].reshape(BATCH, SEQ, ATT_HEADS, ATT_HEAD_DIM))

        conv_b = ssd_conv_b[l][None, :]
        a_neg = -jnp.exp(ssd_a_log[l].astype(f32))
        d_row = jnp.repeat(ssd_d[l], SSD_HEAD_DIM)[None, :]
        ssd_gain = ssd_norm[l][None, :]
        y_ssd = []
        for row0, nb, length, init in ((0, BATCH, SEQ, zero_ssd), (N_CTX, DEC_BATCH, DEC_SEQ, state_ssd[:, l])):
            xbc = _ssd_conv(proj, row0, nb * length, length, ssd_conv_w[l], conv_b)
            y_f, y_b, states = _ssd_scan(xbc, proj, row0, nb, length, ssd_dt_bias[l], a_neg, init)
            y_ssd.append(_ssd_finish(y_f, y_b, xbc, proj, row0, d_row, ssd_gain))
            if row0 == 0:
                new_ssd.append(states)

        lb = lower_bounds[:, l]
        hg_gain = hg_norm[l].reshape(1, HG_DIM)
        y_hg_ctx, states = _hgrn_mixer(proj, 0, BATCH, SEQ, lb, hg_gain, zero_hg)
        new_hg.append(states)
        y_hg_lat, _ = _hgrn_mixer(proj, lat_row0, DEC_BATCH, DEC_SEQ, lb, hg_gain, state_hgrn[:, l])

        y_att_ctx = _context_attention(proj, BATCH, SEQ)
        y_att_lat = _neighbourhood_attention(proj, lat_row0, DEC_BATCH, DEC_SEQ, cache_k, cache_v, l,
                                             _window_bias(att_rpb[l]))

        ys = (jnp.concatenate(y_ssd, axis=0), jnp.concatenate([y_hg_ctx, y_hg_lat], axis=0),
              jnp.concatenate([y_att_ctx, y_att_lat], axis=0))
        merged = _branch_merge(ys, (w_br_ssd[l].astype(bf16), w_br_hg[l].astype(bf16), w_br_att[l].astype(bf16)),
                               proj)
        x = _out_residual(merged, w_out[l].astype(bf16), x, gate_m)

        h2, ids, wts, counts = _norm_router(x, norm_moe[l][None, :], scale_f, shift_f, wr_hi, wr_lo,
                                            router_bias.astype(f32)[:, None])
        moe = _moe(h2, ids, wts, counts, moe_w1[l].astype(bf16), moe_w3[l].astype(bf16), moe_w2[l].astype(bf16))
        x = x + _per_token(gate_f) * moe

    y = _final_norm(x, final_norm[None, :])
    y_prompt = y[:N_CTX].reshape(BATCH, SEQ, D_MODEL)
    y_sample = y[N_CTX:].reshape(DEC_BATCH, DEC_SEQ, D_MODEL)
    return (y_prompt, y_sample, jnp.stack(new_k, axis=1), jnp.stack(new_v, axis=1),
            jnp.stack(new_ssd, axis=1), jnp.stack(new_hg, axis=1))
```

```python
import functools

import jax
import jax.numpy as jnp
from jax import lax
from jax.experimental import pallas as pl
from jax.experimental.pallas import tpu as pltpu

D_MODEL = 2048
BATCH = 32
SEQ = 256
DEPTH = 2
DEC_BATCH = 8
DEC_SEQ = 4096
GRID_W = 64
NORM_EPS = 1e-6
SSD_HEADS = 16
SSD_HEAD_DIM = 64
SSD_DIM = SSD_HEADS * SSD_HEAD_DIM
SSD_STATE = 64
SSD_GROUPS = 4
SSD_CONV = 5
SSD_CONV_DIM = SSD_DIM + 2 * SSD_GROUPS * SSD_STATE
HG_HEADS = 8
HG_KDIM = 128
HG_VDIM = 128
HG_FDIM = HG_HEADS * HG_KDIM
HG_DIM = HG_HEADS * HG_VDIM
ATT_HEADS = 8
ATT_HEAD_DIM = 128
ATT_DIM = ATT_HEADS * ATT_HEAD_DIM
WIN_ROWS = 8
WIN_COLS = 16
N_BRANCH = 3
N_EXPERTS = 16
N_EXPERT_GROUPS = 4
EXPERTS_PER_GROUP = N_EXPERTS // N_EXPERT_GROUPS
TOP_K = 2
D_EXPERT = 1024

N_CTX = BATCH * SEQ
N_LAT = DEC_BATCH * DEC_SEQ
N_TOK = N_CTX + N_LAT

VMEM_LIMIT_BYTES = 56 * 1024 * 1024
LANES = 128
SUBLANES = 8

OFF_Z = 0
OFF_XBC = OFF_Z + SSD_DIM
OFF_HQ = OFF_XBC + SSD_CONV_DIM
OFF_HF = OFF_HQ + HG_FDIM
OFF_HI = OFF_HF + 2 * HG_FDIM
OFF_HG = OFF_HI + HG_DIM
OFF_AQ = OFF_HG + HG_DIM
OFF_AK = OFF_AQ + ATT_DIM
OFF_AV = OFF_AK + ATT_DIM
OFF_GATES = OFF_AV + ATT_DIM
OFF_DT = OFF_GATES + N_BRANCH * D_MODEL
DT_PAD = 512
PROJ_DIM = OFF_DT + DT_PAD

ROW_TILE = 1024
COL_TILE = 1024
MERGE_COLS = 512
NORM_ROWS = 64
ROUTER_TILE = 512
MOE_BLOCK = 512
CONV_ROWS = 256
CONV_COLS = 512
SSD_C = 128
FINISH_ROWS = 256
HG_C = 128
HG_LEVELS = (16, 32, 64, 128)
CAST_ROWS = 512
CAST_TILE = 1024
ROW_UNROLL = 4
MASKED = -1e30

_NT = (((1,), (1,)), ((), ()))
_TN = (((0,), (0,)), ((), ()))


def _params(*semantics):
    return pltpu.CompilerParams(dimension_semantics=semantics, vmem_limit_bytes=VMEM_LIMIT_BYTES)


def _mod_row(i):
    ctx_tiles = N_CTX // ROW_TILE
    tiles_per_req = DEC_SEQ // ROW_TILE
    return jnp.where(i < ctx_tiles, 0, 1 + (i - ctx_tiles) // tiles_per_req)


def _split3(x):
    bf16, f32 = jnp.bfloat16, jnp.float32
    x1 = x.astype(bf16)
    r = x - x1.astype(f32)
    x2 = r.astype(bf16)
    x3 = (r - x2.astype(f32)).astype(bf16)
    return x1, x2, x3


def _tri_cumsum(tri, x):
    x1, x2, x3 = _split3(x)
    f32 = jnp.float32
    return (jnp.dot(tri, x1, preferred_element_type=f32)
            + (jnp.dot(tri, x2, preferred_element_type=f32) + jnp.dot(tri, x3, preferred_element_type=f32)))


def _mm_kernel(x_ref, w_ref, o_ref):
    o_ref[...] = jnp.dot(x_ref[...].astype(jnp.bfloat16), w_ref[...],
                         preferred_element_type=jnp.float32).astype(o_ref.dtype)


def _matmul(x, w, tm, tn):
    m, k = x.shape
    n = w.shape[1]
    return pl.pallas_call(
        _mm_kernel,
        grid=(m // tm, n // tn),
        in_specs=[pl.BlockSpec((tm, k), lambda i, j: (i, 0)),
                  pl.BlockSpec((k, tn), lambda i, j: (0, j))],
        out_specs=pl.BlockSpec((tm, tn), lambda i, j: (i, j)),
        out_shape=jax.ShapeDtypeStruct((m, n), jnp.float32),
        compiler_params=_params("arbitrary", "arbitrary"),
        name="matmul",
    )(x, w)


def _cast_kernel(x_ref, o_ref):
    o_ref[...] = x_ref[...].astype(o_ref.dtype)


def _cast_bf16(w):
    n, r, c = w.shape
    spec = pl.BlockSpec((None, CAST_TILE, c), lambda i, j: (i, j, 0))
    return pl.pallas_call(
        _cast_kernel,
        grid=(n, r // CAST_TILE),
        in_specs=[spec],
        out_specs=spec,
        out_shape=jax.ShapeDtypeStruct(w.shape, jnp.bfloat16),
        compiler_params=_params("arbitrary", "arbitrary"),
        name="cast_bf16",
    )(w)


def _modulated_norm(x_ref, g_ref, sc_ref, sh_ref, store):
    g = g_ref[...]
    sc = 1.0 + sc_ref[...]
    sh = sh_ref[...]

    def body(r, carry):
        rows = pl.ds(pl.multiple_of(r * NORM_ROWS, NORM_ROWS), NORM_ROWS)
        x = x_ref[rows, :]
        y = x * lax.rsqrt(jnp.mean(x * x, axis=-1, keepdims=True) + NORM_EPS)
        store(rows, (y * g) * sc + sh)
        return carry

    lax.fori_loop(0, x_ref.shape[0] // NORM_ROWS, body, 0)


def _norm_mm_kernel(x_ref, g_ref, sc_ref, sh_ref, w_ref, o_ref, h_ref):
    @pl.when(pl.program_id(1) == 0)
    def _():
        def store(rows, h):
            h_ref[rows, :] = h.astype(jnp.bfloat16)
        _modulated_norm(x_ref, g_ref, sc_ref, sh_ref, store)

    o_ref[...] = jnp.dot(h_ref[...], w_ref[...], preferred_element_type=jnp.float32)


def _norm_matmul(x, gain, scale, shift, w):
    n = w.shape[1]
    mod_spec = pl.BlockSpec((None, 1, D_MODEL), lambda i, j: (_mod_row(i), 0, 0))
    return pl.pallas_call(
        _norm_mm_kernel,
        grid=(N_TOK // ROW_TILE, n // COL_TILE),
        in_specs=[pl.BlockSpec((ROW_TILE, D_MODEL), lambda i, j: (i, 0)),
                  pl.BlockSpec((1, D_MODEL), lambda i, j: (0, 0)),
                  mod_spec, mod_spec,
                  pl.BlockSpec((D_MODEL, COL_TILE), lambda i, j: (0, j))],
        out_specs=pl.BlockSpec((ROW_TILE, COL_TILE), lambda i, j: (i, j)),
        out_shape=jax.ShapeDtypeStruct((N_TOK, n), jnp.float32),
        scratch_shapes=[pltpu.VMEM((ROW_TILE, D_MODEL), jnp.bfloat16)],
        compiler_params=_params("arbitrary", "arbitrary"),
        name="norm_in_proj",
    )(x, gain, scale, shift, w)


def _conv_kernel(prev_ref, x_ref, next_ref, w_ref, b_ref, o_ref, ext_ref, *, tiles_per_seq):
    i = pl.program_id(0)
    t = CONV_ROWS
    pad = SSD_CONV // 2
    first = (i % tiles_per_seq) == 0
    last = (i % tiles_per_seq) == tiles_per_seq - 1
    ext_ref[0:SUBLANES, :] = jnp.where(first, 0.0, prev_ref[...])
    ext_ref[SUBLANES:SUBLANES + t, :] = x_ref[...]
    ext_ref[SUBLANES + t:2 * SUBLANES + t, :] = jnp.where(last, 0.0, next_ref[...])
    y = jnp.broadcast_to(b_ref[...], (t, CONV_COLS))
    for j in range(SSD_CONV):
        y = y + ext_ref[SUBLANES - pad + j:SUBLANES - pad + j + t, :] * w_ref[j:j + 1, :]
    o_ref[...] = y * jax.nn.sigmoid(y)


def _ssd_conv(proj, row0, n_rows, seq_len, w, b):
    t = CONV_ROWS
    r0 = row0 // t
    c0 = OFF_XBC // CONV_COLS
    sub = t // SUBLANES
    n_sub = proj.shape[0] // SUBLANES
    return pl.pallas_call(
        functools.partial(_conv_kernel, tiles_per_seq=seq_len // t),
        grid=(n_rows // t, SSD_CONV_DIM // CONV_COLS),
        in_specs=[pl.BlockSpec((SUBLANES, CONV_COLS), lambda i, j: (jnp.maximum((r0 + i) * sub - 1, 0), c0 + j)),
                  pl.BlockSpec((t, CONV_COLS), lambda i, j: (r0 + i, c0 + j)),
                  pl.BlockSpec((SUBLANES, CONV_COLS),
                               lambda i, j: (jnp.minimum((r0 + i + 1) * sub, n_sub - 1), c0 + j)),
                  pl.BlockSpec((SSD_CONV, CONV_COLS), lambda i, j: (0, j)),
                  pl.BlockSpec((1, CONV_COLS), lambda i, j: (0, j))],
        out_specs=pl.BlockSpec((t, CONV_COLS), lambda i, j: (i, j)),
        out_shape=jax.ShapeDtypeStruct((n_rows, SSD_CONV_DIM), jnp.float32),
        scratch_shapes=[pltpu.VMEM((t + 2 * SUBLANES, CONV_COLS), jnp.float32)],
        compiler_params=_params("arbitrary", "arbitrary"),
        name="ssd_conv",
    )(proj, proj, proj, w, b)


def _softplus(x):
    return jnp.maximum(x, 0.0) + jnp.log1p(jnp.exp(-jnp.abs(x)))


def _expand(xs, sel, terms):
    parts = [jnp.concatenate(_split3(x)[:terms], axis=1) for x in xs]
    out = jnp.dot(jnp.concatenate(parts, axis=0), sel, preferred_element_type=jnp.float32)
    rows = xs[0].shape[0]
    return [out[i * rows:(i + 1) * rows] for i in range(len(xs))]


def _ssd_kernel(xf_ref, xb_ref, dtf_ref, dtb_ref, dtbias_ref, aneg_ref, init_ref, yf_ref, yb_ref, st_ref,
                s_ref, *, nc):
    bf16, f32 = jnp.bfloat16, jnp.float32
    c = SSD_C
    hd, ns = SSD_HEAD_DIM, SSD_STATE
    pair_w = 2 * hd
    j = pl.program_id(1)

    @pl.when(j == 0)
    def _():
        s_ref[...] = init_ref[...]

    t_ids = lax.broadcasted_iota(jnp.int32, (c, c), 0)
    s_ids = lax.broadcasted_iota(jnp.int32, (c, c), 1)
    eye = (lax.broadcasted_iota(jnp.int32, (SSD_HEADS, SSD_HEADS), 0)
           == lax.broadcasted_iota(jnp.int32, (SSD_HEADS, SSD_HEADS), 1)).astype(bf16)
    head_of = lambda terms, n, w: (lax.broadcasted_iota(jnp.int32, (terms * SSD_HEADS, n), 1) // w
                                   == lax.broadcasted_iota(jnp.int32, (terms * SSD_HEADS, n), 0) % SSD_HEADS
                                   ).astype(bf16)
    sel_x = head_of(2, SSD_DIM, hd)
    sel_c = head_of(3, SSD_HEADS * c, c)
    low_lanes = lax.broadcasted_iota(jnp.int32, (c, pair_w), 1) < hd
    low_rows = lax.broadcasted_iota(jnp.int32, (pair_w, ns), 0) < hd

    for d, (x_ref, dt_ref, y_ref) in enumerate(((xf_ref, dtf_ref, yf_ref), (xb_ref, dtb_ref, yb_ref))):
        rev = d == 1
        causal = (s_ids >= t_ids) if rev else (s_ids <= t_ids)
        tri = causal.astype(bf16)
        dt = _softplus(dt_ref[:, d * SSD_HEADS:(d + 1) * SSD_HEADS] + dtbias_ref[d:d + 1, :])
        acs = _tri_cumsum(tri, dt * aneg_ref[d:d + 1, :])
        acs_t = sum(lax.dot_general(eye, part, _NT, preferred_element_type=f32) for part in _split3(acs))
        end = acs[0:1, :] if rev else acs[c - 1:c, :]
        dt_x, out_x, in_x = _expand([dt, jnp.exp(end - acs), jnp.exp(acs)], sel_x, 2)
        acs_c, = _expand([acs], sel_c, 3)
        end_decay = jnp.exp(end)
        for g in range(SSD_GROUPS):
            bg = x_ref[:, SSD_DIM + g * ns:SSD_DIM + (g + 1) * ns].astype(bf16)
            cg = x_ref[:, SSD_DIM + (SSD_GROUPS + g) * ns:SSD_DIM + (SSD_GROUPS + g + 1) * ns].astype(bf16)
            cb = lax.dot_general(cg, bg, _NT, preferred_element_type=f32)
            for p in range(g * 2, g * 2 + 2):
                lanes = slice(p * pair_w, (p + 1) * pair_w)
                scores = []
                for h in (2 * p, 2 * p + 1):
                    decay = jnp.exp(jnp.minimum(acs_c[:, h * c:(h + 1) * c] - acs_t[h:h + 1, :], 0.0))
                    scores.append(jnp.where(causal, cb * decay, 0.0).astype(bf16))
                xdt = x_ref[:, lanes] * dt_x[:, lanes]
                rhs = jnp.concatenate([jnp.where(low_lanes, xdt, 0.0), jnp.where(low_lanes, 0.0, xdt)],
                                      axis=0).astype(bf16)
                s_p = s_ref[d, p]
                y = (jnp.dot(jnp.concatenate(scores, axis=1), rhs, preferred_element_type=f32)
                     + lax.dot_general(cg, s_p.astype(bf16), _NT, preferred_element_type=f32) * in_x[:, lanes])
                y_ref[:, lanes] = y
                keep = jnp.where(low_rows, end_decay[:, 2 * p:2 * p + 1], end_decay[:, 2 * p + 1:2 * p + 2])
                s_ref[d, p] = s_p * keep + lax.dot_general((xdt * out_x[:, lanes]).astype(bf16), bg, _TN,
                                                           preferred_element_type=f32)

    @pl.when(j == nc - 1)
    def _():
        st_ref[...] = s_ref[...]


def _ssd_scan(xbc, proj, row0, nb, length, dt_bias, a_neg, init):
    c = SSD_C
    nc = length // c
    r0 = row0 // c
    cdt = OFF_DT // LANES
    pair_state = (nb, 2, SSD_HEADS // 2, 2 * SSD_HEAD_DIM, SSD_STATE)
    state_spec = pl.BlockSpec((None,) + pair_state[1:], lambda b, j: (b, 0, 0, 0, 0))
    y_f, y_b, states = pl.pallas_call(
        functools.partial(_ssd_kernel, nc=nc),
        grid=(nb, nc),
        in_specs=[pl.BlockSpec((c, SSD_CONV_DIM), lambda b, j: (b * nc + j, 0)),
                  pl.BlockSpec((c, SSD_CONV_DIM), lambda b, j: (b * nc + nc - 1 - j, 0)),
                  pl.BlockSpec((c, LANES), lambda b, j: (r0 + b * nc + j, cdt)),
                  pl.BlockSpec((c, LANES), lambda b, j: (r0 + b * nc + nc - 1 - j, cdt)),
                  pl.BlockSpec((2, SSD_HEADS), lambda b, j: (0, 0)),
                  pl.BlockSpec((2, SSD_HEADS), lambda b, j: (0, 0)),
                  state_spec],
        out_specs=[pl.BlockSpec((c, SSD_DIM), lambda b, j: (b * nc + j, 0)),
                   pl.BlockSpec((c, SSD_DIM), lambda b, j: (b * nc + nc - 1 - j, 0)),
                   state_spec],
        out_shape=[jax.ShapeDtypeStruct((nb * length, SSD_DIM), jnp.float32),
                   jax.ShapeDtypeStruct((nb * length, SSD_DIM), jnp.float32),
                   jax.ShapeDtypeStruct(pair_state, jnp.float32)],
        scratch_shapes=[pltpu.VMEM(pair_state[1:], jnp.float32)],
        compiler_params=_params("arbitrary", "arbitrary"),
        name="ssd_scan",
    )(xbc, xbc, proj, proj, dt_bias, a_neg, init.reshape(pair_state))
    return y_f, y_b, states.reshape(nb, 2, SSD_HEADS, SSD_HEAD_DIM, SSD_STATE)


def _ssd_finish_kernel(yf_ref, yb_ref, x_ref, z_ref, d_ref, g_ref, o_ref):
    z = z_ref[...]
    y = (yf_ref[...] + yb_ref[...] + d_ref[...] * x_ref[...]) * (z * jax.nn.sigmoid(z))
    o_ref[...] = (y * lax.rsqrt(jnp.mean(y * y, axis=-1, keepdims=True) + NORM_EPS) * g_ref[...]).astype(o_ref.dtype)


def _ssd_finish(y_f, y_b, xbc, proj, row0, d_row, gain):
    n = y_f.shape[0]
    t = FINISH_ROWS
    r0 = row0 // t
    row_spec = pl.BlockSpec((t, SSD_DIM), lambda i: (i, 0))
    vec_spec = pl.BlockSpec((1, SSD_DIM), lambda i: (0, 0))
    return pl.pallas_call(
        _ssd_finish_kernel,
        grid=(n // t,),
        in_specs=[row_spec, row_spec, row_spec,
                  pl.BlockSpec((t, SSD_DIM), lambda i: (r0 + i, OFF_Z // SSD_DIM)),
                  vec_spec, vec_spec],
        out_specs=row_spec,
        out_shape=jax.ShapeDtypeStruct((n, SSD_DIM), jnp.bfloat16),
        compiler_params=_params("arbitrary"),
        name="ssd_finish",
    )(y_f, y_b, xbc, proj, d_row, gain)


def _hgrn_chunk(q, g, kk, v, s_t, rev, tri, lane_mod, diag, level_masks, b_ref, kk_ref):
    bf16, f32 = jnp.bfloat16, jnp.float32
    c = q.shape[0]
    ng = c // SUBLANES
    b = _tri_cumsum(tri, g)
    yield
    b_ref[...] = b
    kk_ref[...] = kk
    row = lambda a_ref, r: a_ref[r:r + 1, :]
    grp = lambda a, i: a[i * SUBLANES:(i + 1) * SUBLANES, :]
    b_end = row(b_ref, 0) if rev else row(b_ref, c - 1)

    ones = jnp.ones((HG_KDIM, c), bf16)
    diag_rows = []
    for i in range(ng):
        qg, bg = grp(q, i), grp(b, i)
        tiles = [qg * jnp.exp(bg - row(b_ref, i * SUBLANES + j)) * row(kk_ref, i * SUBLANES + j)
                 for j in range(SUBLANES)]
        sums = jnp.dot(jnp.concatenate(tiles, axis=0).astype(bf16), ones, preferred_element_type=f32)
        acc = sums[0:SUBLANES, :]
        for j in range(1, SUBLANES):
            acc = jnp.where(lane_mod[j], sums[j * SUBLANES:(j + 1) * SUBLANES, :], acc)
        diag_rows.append(acc)
        yield
    att = jnp.where(diag, jnp.concatenate(diag_rows, axis=0), 0.0)

    for m, mask in zip(HG_LEVELS, level_masks):
        half = m // 2
        q_side, k_side = [], []
        for i in range(ng):
            start = (i * SUBLANES) // m * m
            later = (i * SUBLANES) % m >= half
            ref = row(b_ref, start + half if rev else start + half - 1)
            if later != rev:
                q_side.append(grp(q, i) * jnp.exp(grp(b, i) - ref))
                k_side.append(jnp.zeros((SUBLANES, HG_KDIM), f32))
            else:
                q_side.append(jnp.zeros((SUBLANES, HG_KDIM), f32))
                k_side.append(grp(kk, i) * jnp.exp(ref - grp(b, i)))
        a_m = lax.dot_general(jnp.concatenate(q_side, axis=0).astype(bf16),
                              jnp.concatenate(k_side, axis=0).astype(bf16), _NT, preferred_element_type=f32)
        att = att + (a_m if m == c else jnp.where(mask, a_m, 0.0))
        yield

    q_in = (q * jnp.exp(b)).astype(bf16)
    o = (lax.dot_general(q_in, s_t.astype(bf16), _NT, preferred_element_type=f32)
         + jnp.dot(att.astype(bf16), v.astype(bf16), preferred_element_type=f32))
    k_out = (kk * jnp.exp(b_end - b)).astype(bf16)
    s_new = s_t * jnp.exp(b_end) + lax.dot_general(v.astype(bf16), k_out, _TN, preferred_element_type=f32)
    return o, s_new


def _in_lockstep(gens):
    results = [None] * len(gens)
    live = list(range(len(gens)))
    while live:
        for i in list(live):
            try:
                next(gens[i])
            except StopIteration as stop:
                results[i] = stop.value
                live.remove(i)
    return results


def _hgrn_kernel(q_ref, ff_ref, fb_ref, v_ref, gate_ref, lb_ref, gn_ref, init_ref, o_ref, st_ref,
                 acc_ref, s_ref, b_ref, kk_ref, *, nc):
    c = HG_C
    t_ids = lax.broadcasted_iota(jnp.int32, (c, c), 0)
    s_ids = lax.broadcasted_iota(jnp.int32, (c, c), 1)
    causal = (s_ids <= t_ids, s_ids >= t_ids)
    tri = tuple(m.astype(jnp.bfloat16) for m in causal)
    same_group = (t_ids // SUBLANES) == (s_ids // SUBLANES)
    diag = tuple(m & same_group for m in causal)
    lane_mod = [(s_ids[:SUBLANES] % SUBLANES) == j for j in range(SUBLANES)]

    def level_mask(m, rev):
        same = (t_ids // m) == (s_ids // m)
        t_late = (t_ids % m) >= m // 2
        s_late = (s_ids % m) >= m // 2
        return same & (t_late != s_late) & (t_late != rev)
    masks = tuple([level_mask(m, rev) for m in HG_LEVELS] for rev in (False, True))

    s_ref[0] = init_ref[0].T
    s_ref[1] = init_ref[1].T
    gain = gn_ref[...]

    def run_pair(j):
        rows, gens = [], []
        for d, chunk in ((0, j), (1, nc - 1 - j)):
            r = pl.ds(pl.multiple_of(chunk * c, c), c)
            x = q_ref[r, :]
            q = x * jax.nn.sigmoid(x)
            lb = lb_ref[d:d + 1, :]
            f = lb + (1.0 - lb) * jax.nn.sigmoid((ff_ref, fb_ref)[d][r, :])
            rows.append(r)
            gens.append(_hgrn_chunk(q, jnp.log(f), 1.0 - f, v_ref[r, :], s_ref[d], d == 1,
                                    tri[d], lane_mod, diag[d], masks[d], b_ref.at[d], kk_ref.at[d]))
        outs = _in_lockstep(gens)
        for d in (0, 1):
            s_ref[d] = outs[d][1]
        return rows, [o for o, _ in outs]

    def first_half(j, carry):
        rows, outs = run_pair(j)
        for r, o in zip(rows, outs):
            acc_ref[r, :] = o
        return carry

    def second_half(j, carry):
        rows, outs = run_pair(j)
        for r, o in zip(rows, outs):
            o = o + acc_ref[r, :]
            y = o * lax.rsqrt(jnp.mean(o * o, axis=-1, keepdims=True) + NORM_EPS) * gain
            gate = gate_ref[r, :]
            o_ref[r, :] = (y * (gate * jax.nn.sigmoid(gate))).astype(o_ref.dtype)
        return carry

    lax.fori_loop(0, nc // 2, first_half, 0)
    lax.fori_loop(nc // 2, nc, second_half, 0)
    st_ref[0] = s_ref[0].T
    st_ref[1] = s_ref[1].T


def _hgrn_mixer(proj, row0, nb, length, lb, gain, init):
    nc = length // HG_C
    assert nc % 2 == 0
    seq = lambda off: pl.BlockSpec((length, HG_KDIM), lambda b, h, col=off // HG_KDIM: (row0 + b, col + h))
    state_spec = pl.BlockSpec((None, 2, None, HG_KDIM, HG_VDIM), lambda b, h: (b, 0, h, 0, 0))
    return pl.pallas_call(
        functools.partial(_hgrn_kernel, nc=nc),
        grid=(nb, HG_HEADS),
        in_specs=[seq(OFF_HQ), seq(OFF_HF), seq(OFF_HF + HG_FDIM), seq(OFF_HI), seq(OFF_HG),
                  pl.BlockSpec((2, HG_KDIM), lambda b, h: (0, h)),
                  pl.BlockSpec((1, HG_VDIM), lambda b, h: (0, h)),
                  state_spec],
        out_specs=[pl.BlockSpec((length, HG_VDIM), lambda b, h: (b, h)), state_spec],
        out_shape=[jax.ShapeDtypeStruct((nb * length, HG_DIM), jnp.bfloat16),
                   jax.ShapeDtypeStruct((nb, 2, HG_HEADS, HG_KDIM, HG_VDIM), jnp.float32)],
        scratch_shapes=[pltpu.VMEM((length, HG_VDIM), jnp.float32),
                        pltpu.VMEM((2, HG_VDIM, HG_KDIM), jnp.float32),
                        pltpu.VMEM((2, HG_C, HG_KDIM), jnp.float32),
                        pltpu.VMEM((2, HG_C, HG_KDIM), jnp.float32)],
        compiler_params=_params("arbitrary", "arbitrary"),
        name="hgrn_mixer",
    )(proj, proj, proj, proj, proj, lb, gain, init)


def _softmax_av(scores, values):
    f32, bf16 = jnp.float32, jnp.bfloat16
    m = functools.reduce(jnp.maximum, [jnp.max(s, axis=-1, keepdims=True) for s in scores])
    ps = [jnp.exp(s - m) for s in scores]
    denom = functools.reduce(jnp.add, [jnp.sum(p, axis=-1, keepdims=True) for p in ps])
    acc = functools.reduce(jnp.add, [jnp.dot(p.astype(bf16), v, preferred_element_type=f32)
                                     for p, v in zip(ps, values)])
    return acc / denom


def _ctx_attn_kernel(q_ref, k_ref, v_ref, o_ref):
    bf16 = jnp.bfloat16
    scale = ATT_HEAD_DIM ** -0.5
    s = lax.dot_general(q_ref[...].astype(bf16), k_ref[...].astype(bf16), _NT,
                        preferred_element_type=jnp.float32) * scale
    o_ref[...] = _softmax_av([s], [v_ref[...].astype(bf16)]).astype(o_ref.dtype)


def _context_attention(proj, nb, length):
    spec = lambda off: pl.BlockSpec((length, ATT_HEAD_DIM), lambda b, h, col=off // ATT_HEAD_DIM: (b, col + h))
    return pl.pallas_call(
        _ctx_attn_kernel,
        grid=(nb, ATT_HEADS),
        in_specs=[spec(OFF_AQ), spec(OFF_AK), spec(OFF_AV)],
        out_specs=pl.BlockSpec((length, ATT_HEAD_DIM), lambda b, h: (b, h)),
        out_shape=jax.ShapeDtypeStruct((nb * length, ATT_DIM), jnp.bfloat16),
        compiler_params=_params("arbitrary", "arbitrary"),
        name="context_attention",
    )(proj, proj, proj)


def _window_bias(rpb):
    col = jnp.arange(GRID_W)
    cs = jnp.clip(col - WIN_COLS // 2, 0, GRID_W - WIN_COLS)
    col_mask = (col[None, :] >= cs[:, None]) & (col[None, :] < cs[:, None] + WIN_COLS)
    dc_idx = jnp.clip(col[None, :] - col[:, None] + WIN_COLS - 1, 0, 2 * WIN_COLS - 2)
    bias = jnp.where(col_mask, rpb[:, :, dc_idx].astype(jnp.float32), MASKED)
    wins = [bias[:, d0:d0 + WIN_ROWS].transpose(0, 2, 1, 3).reshape(rpb.shape[0], GRID_W, WIN_ROWS * GRID_W)
            for d0 in range(WIN_ROWS)]
    return jnp.stack(wins, axis=1)


def _natten_kernel(q_ref, k_ref, v_ref, kc_ref, vc_ref, bias_ref, o_ref, kb_ref, vb_ref, *, rows):
    bf16, f32 = jnp.bfloat16, jnp.float32
    scale = ATT_HEAD_DIM ** -0.5
    win = WIN_ROWS * GRID_W

    def cast(i, carry):
        sl = pl.ds(pl.multiple_of(i * CAST_ROWS, CAST_ROWS), CAST_ROWS)
        kb_ref[sl, :] = k_ref[sl, :].astype(bf16)
        vb_ref[sl, :] = v_ref[sl, :].astype(bf16)
        return carry
    lax.fori_loop(0, rows * GRID_W // CAST_ROWS, cast, 0)

    kc = kc_ref[...].astype(bf16)
    vc = vc_ref[...].astype(bf16)

    def row_block(r, carry):
        rs = jnp.clip(r - WIN_ROWS // 2, 0, rows - WIN_ROWS)
        d0 = rs - r + WIN_ROWS - 1
        q = q_ref[pl.ds(pl.multiple_of(r * GRID_W, GRID_W), GRID_W), :].astype(bf16)
        keys = pl.ds(pl.multiple_of(rs * GRID_W, GRID_W), win)
        s_lat = lax.dot_general(q, kb_ref[keys, :], _NT, preferred_element_type=f32) * scale + bias_ref[d0]
        s_ctx = lax.dot_general(q, kc, _NT, preferred_element_type=f32) * scale
        o = _softmax_av([s_lat, s_ctx], [vb_ref[keys, :], vc])
        o_ref[pl.ds(pl.multiple_of(r * GRID_W, GRID_W), GRID_W), :] = o.astype(o_ref.dtype)
        return carry
    lax.fori_loop(0, rows, row_block, 0, unroll=ROW_UNROLL)


def _neighbourhood_attention(proj, row0, nb, length, cache_k, cache_v, layer, bias_win):
    rows = length // GRID_W
    past = cache_k.shape[2]
    spec = lambda off: pl.BlockSpec((length, ATT_HEAD_DIM),
                                    lambda b, h, col=off // ATT_HEAD_DIM: (row0 + b, col + h))
    cache_spec = pl.BlockSpec((None, None, past, ATT_HEAD_DIM), lambda b, h: (b, layer, 0, h))
    return pl.pallas_call(
        functools.partial(_natten_kernel, rows=rows),
        grid=(nb, ATT_HEADS),
        in_specs=[spec(OFF_AQ), spec(OFF_AK), spec(OFF_AV), cache_spec, cache_spec,
                  pl.BlockSpec((None, WIN_ROWS, GRID_W, WIN_ROWS * GRID_W), lambda b, h: (h, 0, 0, 0))],
        out_specs=pl.BlockSpec((length, ATT_HEAD_DIM), lambda b, h: (b, h)),
        out_shape=jax.ShapeDtypeStruct((nb * length, ATT_DIM), jnp.bfloat16),
        scratch_shapes=[pltpu.VMEM((length, ATT_HEAD_DIM), jnp.bfloat16),
                        pltpu.VMEM((length, ATT_HEAD_DIM), jnp.bfloat16)],
        compiler_params=_params("arbitrary", "arbitrary"),
        name="neighbourhood_attention",
    )(proj, proj, proj, cache_k, cache_v, bias_win)


def _merge_kernel(ya_ref, yb_ref, yc_ref, wa_ref, wb_ref, wc_ref, ga_ref, gb_ref, gc_ref, o_ref):
    f32 = jnp.float32
    acc = jax.nn.sigmoid(ga_ref[...]) * jnp.dot(ya_ref[...], wa_ref[...], preferred_element_type=f32)
    acc = acc + jax.nn.sigmoid(gb_ref[...]) * jnp.dot(yb_ref[...], wb_ref[...], preferred_element_type=f32)
    acc = acc + jax.nn.sigmoid(gc_ref[...]) * jnp.dot(yc_ref[...], wc_ref[...], preferred_element_type=f32)
    o_ref[...] = acc.astype(o_ref.dtype)


def _branch_merge(ys, ws, proj):
    tm, tn = ROW_TILE, MERGE_COLS
    kdim = ys[0].shape[1]
    cg = OFF_GATES // tn
    per = D_MODEL // tn
    y_spec = pl.BlockSpec((tm, kdim), lambda i, j: (i, 0))
    w_spec = pl.BlockSpec((kdim, tn), lambda i, j: (0, j))
    g_spec = lambda b: pl.BlockSpec((tm, tn), lambda i, j, b=b: (i, cg + b * per + j))
    return pl.pallas_call(
        _merge_kernel,
        grid=(N_TOK // tm, D_MODEL // tn),
        in_specs=[y_spec, y_spec, y_spec, w_spec, w_spec, w_spec, g_spec(0), g_spec(1), g_spec(2)],
        out_specs=pl.BlockSpec((tm, tn), lambda i, j: (i, j)),
        out_shape=jax.ShapeDtypeStruct((N_TOK, D_MODEL), jnp.bfloat16),
        compiler_params=_params("arbitrary", "arbitrary"),
        name="branch_merge",
    )(*ys, *ws, proj, proj, proj)


def _out_residual_kernel(m_ref, w_ref, x_ref, gate_ref, o_ref):
    o_ref[...] = x_ref[...] + gate_ref[...] * jnp.dot(m_ref[...], w_ref[...], preferred_element_type=jnp.float32)


def _out_residual(merged, w, x, gate):
    tm, tn = ROW_TILE, COL_TILE
    return pl.pallas_call(
        _out_residual_kernel,
        grid=(N_TOK // tm, D_MODEL // tn),
        in_specs=[pl.BlockSpec((tm, D_MODEL), lambda i, j: (i, 0)),
                  pl.BlockSpec((D_MODEL, tn), lambda i, j: (0, j)),
                  pl.BlockSpec((tm, tn), lambda i, j: (i, j)),
                  pl.BlockSpec((None, 1, tn), lambda i, j: (_mod_row(i), 0, j))],
        out_specs=pl.BlockSpec((tm, tn), lambda i, j: (i, j)),
        out_shape=jax.ShapeDtypeStruct((N_TOK, D_MODEL), jnp.float32),
        compiler_params=_params("arbitrary", "arbitrary"),
        name="out_residual",
    )(merged, w, x, gate)


def _first_max(vals):
    best, idx = vals[0], jnp.zeros(vals[0].shape, jnp.int32)
    for k in range(1, len(vals)):
        better = vals[k] > best
        best = jnp.where(better, vals[k], best)
        idx = jnp.where(better, k, idx)
    return best, idx


def _pick(idx, vals):
    out = vals[0]
    for k in range(1, len(vals)):
        out = jnp.where(idx == k, vals[k], out)
    return out


def _norm_router_kernel(x_ref, g_ref, sc_ref, sh_ref, wr_hi_ref, wr_lo_ref, rb_ref,
                        h_ref, ids_ref, wts_ref, cnt_ref, lo_ref, base_ref):
    f32, bf16 = jnp.float32, jnp.bfloat16
    t = x_ref.shape[0]

    @pl.when(pl.program_id(0) == 0)
    def _():
        base_ref[...] = jnp.zeros_like(base_ref)

    def store(rows, h):
        h_hi = h.astype(bf16)
        h_ref[rows, :] = h_hi
        lo_ref[rows, :] = (h - h_hi.astype(f32)).astype(bf16)
    _modulated_norm(x_ref, g_ref, sc_ref, sh_ref, store)

    logits = (lax.dot_general(wr_hi_ref[...], h_ref[...], _NT, preferred_element_type=f32)
              + (lax.dot_general(wr_lo_ref[...], h_ref[...], _NT, preferred_element_type=f32)
                 + lax.dot_general(wr_hi_ref[...], lo_ref[...], _NT, preferred_element_type=f32)))
    scores = jax.nn.sigmoid(logits)
    sel = scores + rb_ref[...]
    row = lambda a, e: a[e:e + 1, :]

    group_scores = []
    for g in range(N_EXPERT_GROUPS):
        v = [row(sel, g * EXPERTS_PER_GROUP + k) for k in range(EXPERTS_PER_GROUP)]
        pair_sums = [v[a] + v[b] for a in range(EXPERTS_PER_GROUP) for b in range(a + 1, EXPERTS_PER_GROUP)]
        group_scores.append(functools.reduce(jnp.maximum, pair_sums))
    _, grp = _first_max(group_scores)

    in_sel = [_pick(grp, [row(sel, g * EXPERTS_PER_GROUP + k) for g in range(N_EXPERT_GROUPS)])
              for k in range(EXPERTS_PER_GROUP)]
    in_score = [_pick(grp, [row(scores, g * EXPERTS_PER_GROUP + k) for g in range(N_EXPERT_GROUPS)])
                for k in range(EXPERTS_PER_GROUP)]
    _, i1 = _first_max(in_sel)
    _, i2 = _first_max([jnp.where(i1 == k, -jnp.inf, in_sel[k]) for k in range(EXPERTS_PER_GROUP)])
    s1 = _pick(i1, in_score)
    s2 = _pick(i2, in_score)
    e1 = grp * EXPERTS_PER_GROUP + i1
    e2 = grp * EXPERTS_PER_GROUP + i2

    e_ids = lax.broadcasted_iota(jnp.int32, (N_EXPERTS, t), 0)
    hit1 = e_ids == e1
    hit2 = e_ids == e2
    cnt = jnp.where(hit1 | hit2, 1.0, 0.0).astype(bf16)
    before = (lax.broadcasted_iota(jnp.int32, (t, t), 0) < lax.broadcasted_iota(jnp.int32, (t, t), 1)).astype(bf16)
    prefix = jnp.dot(cnt, before, preferred_element_type=f32) + base_ref[:, 0:1]
    rank1 = jnp.sum(jnp.where(hit1, prefix, 0.0), axis=0, keepdims=True)
    rank2 = jnp.sum(jnp.where(hit2, prefix, 0.0), axis=0, keepdims=True)
    base_ref[...] = base_ref[...] + jnp.dot(cnt, jnp.ones((t, LANES), bf16), preferred_element_type=f32)
    cnt_ref[...] = base_ref[...]

    zeros = jnp.zeros((SUBLANES - 4, t), jnp.int32)
    ids_ref[...] = jnp.concatenate([e1, e2, rank1.astype(jnp.int32), rank2.astype(jnp.int32), zeros], axis=0)
    total = s1 + s2
    wts_ref[...] = jnp.concatenate([s1 / total, s2 / total, jnp.zeros((SUBLANES - 2, t), f32)], axis=0)


def _norm_router(x, gain, scale, shift, wr_hi_t, wr_lo_t, router_bias):
    t = ROUTER_TILE
    per = ROW_TILE // t
    mod_spec = pl.BlockSpec((None, 1, D_MODEL), lambda i: (_mod_row(i // per), 0, 0))
    return pl.pallas_call(
        _norm_router_kernel,
        grid=(N_TOK // t,),
        in_specs=[pl.BlockSpec((t, D_MODEL), lambda i: (i, 0)),
                  pl.BlockSpec((1, D_MODEL), lambda i: (0, 0)),
                  mod_spec, mod_spec,
                  pl.BlockSpec((N_EXPERTS, D_MODEL), lambda i: (0, 0)),
                  pl.BlockSpec((N_EXPERTS, D_MODEL), lambda i: (0, 0)),
                  pl.BlockSpec((N_EXPERTS, 1), lambda i: (0, 0))],
        out_specs=[pl.BlockSpec((t, D_MODEL), lambda i: (i, 0)),
                   pl.BlockSpec((SUBLANES, t), lambda i: (0, i)),
                   pl.BlockSpec((SUBLANES, t), lambda i: (0, i)),
                   pl.BlockSpec((N_EXPERTS, LANES), lambda i: (0, 0))],
        out_shape=[jax.ShapeDtypeStruct((N_TOK, D_MODEL), jnp.bfloat16),
                   jax.ShapeDtypeStruct((SUBLANES, N_TOK), jnp.int32),
                   jax.ShapeDtypeStruct((SUBLANES, N_TOK), jnp.float32),
                   jax.ShapeDtypeStruct((N_EXPERTS, LANES), jnp.float32)],
        scratch_shapes=[pltpu.VMEM((t, D_MODEL), jnp.bfloat16),
                        pltpu.VMEM((N_EXPERTS, LANES), jnp.float32)],
        compiler_params=_params("arbitrary"),
        name="norm_router",
    )(x, gain, scale, shift, wr_hi_t, wr_lo_t, router_bias)


def _expert_kernel(be_ref, na_ref, x_ref, w1_ref, w3_ref, w2_ref, o_ref):
    active = pl.program_id(0) < na_ref[0]

    @pl.when(active)
    def _():
        x = x_ref[...]
        a = jnp.dot(x, w1_ref[...], preferred_element_type=jnp.float32)
        b = jnp.dot(x, w3_ref[...], preferred_element_type=jnp.float32)
        hdn = (a * jax.nn.sigmoid(a)) * b
        o_ref[...] = jnp.dot(hdn.astype(jnp.bfloat16), w2_ref[...],
                             preferred_element_type=jnp.float32).astype(o_ref.dtype)

    @pl.when(jnp.logical_not(active))
    def _():
        o_ref[...] = jnp.zeros_like(o_ref)


def _expert_blocks(buf, block_e, n_active, w1, w3, w2):
    n_blocks = buf.shape[0] // MOE_BLOCK
    grid_spec = pltpu.PrefetchScalarGridSpec(
        num_scalar_prefetch=2,
        grid=(n_blocks,),
        in_specs=[pl.BlockSpec((MOE_BLOCK, D_MODEL), lambda i, be, na: (i, 0)),
                  pl.BlockSpec((None, D_MODEL, D_EXPERT), lambda i, be, na: (be[i], 0, 0)),
                  pl.BlockSpec((None, D_MODEL, D_EXPERT), lambda i, be, na: (be[i], 0, 0)),
                  pl.BlockSpec((None, D_EXPERT, D_MODEL), lambda i, be, na: (be[i], 0, 0))],
        out_specs=pl.BlockSpec((MOE_BLOCK, D_MODEL), lambda i, be, na: (i, 0)),
    )
    return pl.pallas_call(
        _expert_kernel,
        grid_spec=grid_spec,
        out_shape=jax.ShapeDtypeStruct((n_blocks * MOE_BLOCK, D_MODEL), jnp.bfloat16),
        compiler_params=_params("arbitrary"),
        name="moe_experts",
    )(block_e, n_active, buf, w1, w3, w2)


def _moe(h, ids, wts, counts, layer, w1, w3, w2):
    nk = N_TOK * TOP_K
    n_blocks = (nk + N_EXPERTS * (MOE_BLOCK - 1)) // MOE_BLOCK
    counts = counts[:, 0].astype(jnp.int32)
    padded = (counts + MOE_BLOCK - 1) // MOE_BLOCK * MOE_BLOCK
    pad_end = jnp.cumsum(padded)
    pad_start = pad_end - padded
    dest1 = pad_start[ids[0]] + ids[2]
    dest2 = pad_start[ids[1]] + ids[3]
    tok = jnp.arange(N_TOK, dtype=jnp.int32)
    src = jnp.zeros((n_blocks * MOE_BLOCK,), jnp.int32).at[jnp.concatenate([dest1, dest2])].set(
        jnp.concatenate([tok, tok]))
    block_e = jnp.minimum(jnp.searchsorted(pad_end, jnp.arange(n_blocks, dtype=jnp.int32) * MOE_BLOCK, side='right'),
                          N_EXPERTS - 1).astype(jnp.int32)
    n_active = (pad_end[-1:] // MOE_BLOCK).astype(jnp.int32)
    out = _expert_blocks(h[src], block_e + layer * N_EXPERTS, n_active, w1, w3, w2)
    return wts[0][:, None] * out[dest1] + wts[1][:, None] * out[dest2]


def _final_norm_kernel(x_ref, g_ref, o_ref):
    g = g_ref[...]

    def body(r, carry):
        rows = pl.ds(pl.multiple_of(r * NORM_ROWS, NORM_ROWS), NORM_ROWS)
        x = x_ref[rows, :]
        o_ref[rows, :] = (x * lax.rsqrt(jnp.mean(x * x, axis=-1, keepdims=True) + NORM_EPS)) * g
        return carry

    lax.fori_loop(0, x_ref.shape[0] // NORM_ROWS, body, 0)


def _final_norm(x, gain):
    return pl.pallas_call(
        _final_norm_kernel,
        grid=(N_TOK // ROW_TILE,),
        in_specs=[pl.BlockSpec((ROW_TILE, D_MODEL), lambda i: (i, 0)),
                  pl.BlockSpec((1, D_MODEL), lambda i: (0, 0))],
        out_specs=pl.BlockSpec((ROW_TILE, D_MODEL), lambda i: (i, 0)),
        out_shape=jax.ShapeDtypeStruct((N_TOK, D_MODEL), jnp.float32),
        compiler_params=_params("arbitrary"),
        name="final_norm",
    )(x, gain)


def _permute_w_in(w):
    c0 = SSD_DIM + SSD_CONV_DIM
    c1 = c0 + 2 * SSD_HEADS
    dt_cols = jnp.pad(w[:, c0:c1], ((0, 0), (0, DT_PAD - 2 * SSD_HEADS)))
    return jnp.concatenate([w[:, :c0], w[:, c1:], dt_cols], axis=1).astype(jnp.bfloat16)


def _per_token(m):
    m = m[:, 0]
    return jnp.concatenate([jnp.broadcast_to(m[:1], (N_CTX, D_MODEL)), jnp.repeat(m[1:], DEC_SEQ, axis=0)], axis=0)


def kernel(x_prompt, x_sample, cache_k, cache_v, state_ssd, state_hgrn, c, c_ctx, w_ada, b_ada, norm_mix, norm_moe, w_in, ssd_conv_w, ssd_conv_b, ssd_dt_bias, ssd_a_log, ssd_d, ssd_norm, hg_lb_logits, hg_norm, att_rpb, w_br_ssd, w_br_hg, w_br_att, w_out, w_router, router_bias, moe_w1, moe_w3, moe_w2, final_norm):
    bf16, f32 = jnp.bfloat16, jnp.float32
    lb_cum = jnp.cumsum(jax.nn.softmax(hg_lb_logits.astype(f32), axis=1), axis=1)
    lower_bounds = lb_cum - lb_cum[:, :1]

    x = jnp.concatenate([x_prompt.reshape(N_CTX, D_MODEL), x_sample.reshape(N_LAT, D_MODEL)], axis=0)

    n_mod = 1 + DEC_BATCH
    cond = jnp.concatenate([c_ctx[None, :], c], axis=0)
    cond = jnp.pad(jax.nn.silu(cond), ((0, 2 * SUBLANES - n_mod), (0, 0)))

    wr_t = w_router.T
    wr_hi = wr_t.astype(bf16)
    wr_lo = (wr_t - wr_hi.astype(f32)).astype(bf16)
    cache_k = cache_k.reshape(DEC_BATCH, DEPTH, -1, ATT_DIM)
    cache_v = cache_v.reshape(DEC_BATCH, DEPTH, -1, ATT_DIM)
    lat_row0 = N_CTX // DEC_SEQ
    zero_ssd = jnp.zeros((BATCH, 2, SSD_HEADS, SSD_HEAD_DIM, SSD_STATE), f32)
    zero_hg = jnp.zeros((BATCH, 2, HG_HEADS, HG_KDIM, HG_VDIM), f32)

    expert_w = tuple(_cast_bf16(w.reshape((DEPTH * N_EXPERTS,) + w.shape[2:])) for w in (moe_w1, moe_w3, moe_w2))

    new_k, new_v, new_ssd, new_hg = [], [], [], []
    for l in range(DEPTH):
        mod = _matmul(cond, w_ada[l].astype(bf16), 2 * SUBLANES, COL_TILE)[:n_mod] + b_ada[l]
        mod = mod.reshape(n_mod, 6, 1, D_MODEL)
        shift_m, scale_m, gate_m, shift_f, scale_f, gate_f = (mod[:, i] for i in range(6))

        proj = _norm_matmul(x, norm_mix[l][None, :], scale_m, shift_m, _permute_w_in(w_in[l]))
        new_k.append(proj[:N_CTX, OFF_AK:OFF_AK + ATT_DIM].reshape(BATCH, SEQ, ATT_HEADS, ATT_HEAD_DIM))
        new_v.append(proj[:N_CTX, OFF_AV:OFF_AV + ATT_DIM].reshape(BATCH, SEQ, ATT_HEADS, ATT_HEAD_DIM))

        conv_b = ssd_conv_b[l][None, :]
        a_neg = -jnp.exp(ssd_a_log[l].astype(f32))
        d_row = jnp.repeat(ssd_d[l], SSD_HEAD_DIM)[None, :]
        ssd_gain = ssd_norm[l][None, :]
        y_ssd = []
        for row0, nb, length, init in ((0, BATCH, SEQ, zero_ssd), (N_CTX, DEC_BATCH, DEC_SEQ, state_ssd[:, l])):
            xbc = _ssd_conv(proj, row0, nb * length, length, ssd_conv_w[l], conv_b)
            y_f, y_b, states = _ssd_scan(xbc, proj, row0, nb, length, ssd_dt_bias[l], a_neg, init)
            y_ssd.append(_ssd_finish(y_f, y_b, xbc, proj, row0, d_row, ssd_gain))
            if row0 == 0:
                new_ssd.append(states)

        lb = lower_bounds[:, l]
        hg_gain = hg_norm[l].reshape(1, HG_DIM)
        y_hg_ctx, states = _hgrn_mixer(proj, 0, BATCH, SEQ, lb, hg_gain, zero_hg)
        new_hg.append(states)
        y_hg_lat, _ = _hgrn_mixer(proj, lat_row0, DEC_BATCH, DEC_SEQ, lb, hg_gain, state_hgrn[:, l])

        y_att_ctx = _context_attention(proj, BATCH, SEQ)
        y_att_lat = _neighbourhood_attention(proj, lat_row0, DEC_BATCH, DEC_SEQ, cache_k, cache_v, l,
                                             _window_bias(att_rpb[l]))

        ys = (jnp.concatenate(y_ssd, axis=0), jnp.concatenate([y_hg_ctx, y_hg_lat], axis=0),
              jnp.concatenate([y_att_ctx, y_att_lat], axis=0))
        merged = _branch_merge(ys, (w_br_ssd[l].astype(bf16), w_br_hg[l].astype(bf16), w_br_att[l].astype(bf16)),
                               proj)
        x = _out_residual(merged, w_out[l].astype(bf16), x, gate_m)

        h2, ids, wts, counts = _norm_router(x, norm_moe[l][None, :], scale_f, shift_f, wr_hi, wr_lo,
                                            router_bias.astype(f32)[:, None])
        moe = _moe(h2, ids, wts, counts, l, *expert_w)
        x = x + _per_token(gate_f) * moe

    y = _final_norm(x, final_norm[None, :])
    y_prompt = y[:N_CTX].reshape(BATCH, SEQ, D_MODEL)
    y_sample = y[N_CTX:].reshape(DEC_BATCH, DEC_SEQ, D_MODEL)
    return (y_prompt, y_sample, jnp.stack(new_k, axis=1), jnp.stack(new_v, axis=1),
            jnp.stack(new_ssd, axis=1), jnp.stack(new_hg, axis=1))
```

```python
import functools

import jax
import jax.numpy as jnp
from jax import lax
from jax.experimental import pallas as pl
from jax.experimental.pallas import tpu as pltpu

D_MODEL = 2048
BATCH = 32
SEQ = 256
DEPTH = 2
DEC_BATCH = 8
DEC_SEQ = 4096
GRID_W = 64
NORM_EPS = 1e-6
SSD_HEADS = 16
SSD_HEAD_DIM = 64
SSD_DIM = SSD_HEADS * SSD_HEAD_DIM
SSD_STATE = 64
SSD_GROUPS = 4
SSD_CONV = 5
SSD_CONV_DIM = SSD_DIM + 2 * SSD_GROUPS * SSD_STATE
HG_HEADS = 8
HG_KDIM = 128
HG_VDIM = 128
HG_FDIM = HG_HEADS * HG_KDIM
HG_DIM = HG_HEADS * HG_VDIM
ATT_HEADS = 8
ATT_HEAD_DIM = 128
ATT_DIM = ATT_HEADS * ATT_HEAD_DIM
WIN_ROWS = 8
WIN_COLS = 16
N_BRANCH = 3
N_EXPERTS = 16
N_EXPERT_GROUPS = 4
EXPERTS_PER_GROUP = N_EXPERTS // N_EXPERT_GROUPS
TOP_K = 2
D_EXPERT = 1024

N_CTX = BATCH * SEQ
N_LAT = DEC_BATCH * DEC_SEQ
N_TOK = N_CTX + N_LAT

VMEM_LIMIT_BYTES = 56 * 1024 * 1024
LANES = 128
SUBLANES = 8

OFF_Z = 0
OFF_XBC = OFF_Z + SSD_DIM
OFF_HQ = OFF_XBC + SSD_CONV_DIM
OFF_HF = OFF_HQ + HG_FDIM
OFF_HI = OFF_HF + 2 * HG_FDIM
OFF_HG = OFF_HI + HG_DIM
OFF_AQ = OFF_HG + HG_DIM
OFF_AK = OFF_AQ + ATT_DIM
OFF_AV = OFF_AK + ATT_DIM
OFF_GATES = OFF_AV + ATT_DIM
OFF_DT = OFF_GATES + N_BRANCH * D_MODEL
DT_PAD = 512
PROJ_DIM = OFF_DT + DT_PAD

ROW_TILE = 1024
COL_TILE = 1024
MERGE_COLS = 512
NORM_ROWS = 64
ROUTER_TILE = 512
MOE_BLOCK = 512
CONV_ROWS = 1024
CONV_SUB = 256
CONV_COLS = 512
SSD_C = 128
FINISH_ROWS = 256
HG_C = 128
HG_LEVELS = (16, 32, 64, 128)
CAST_ROWS = 512
DIAG_BATCH = 4
ROW_UNROLL = 4
MASKED = -1e30

_NT = (((1,), (1,)), ((), ()))
_TN = (((0,), (0,)), ((), ()))


def _params(*semantics):
    return pltpu.CompilerParams(dimension_semantics=semantics, vmem_limit_bytes=VMEM_LIMIT_BYTES)


def _mod_row(i):
    ctx_tiles = N_CTX // ROW_TILE
    tiles_per_req = DEC_SEQ // ROW_TILE
    return jnp.where(i < ctx_tiles, 0, 1 + (i - ctx_tiles) // tiles_per_req)


def _split3(x):
    bf16, f32 = jnp.bfloat16, jnp.float32
    x1 = x.astype(bf16)
    r = x - x1.astype(f32)
    x2 = r.astype(bf16)
    x3 = (r - x2.astype(f32)).astype(bf16)
    return x1, x2, x3


def _tri_cumsum(tri, x):
    x1, x2, x3 = _split3(x)
    f32 = jnp.float32
    return (jnp.dot(tri, x1, preferred_element_type=f32)
            + (jnp.dot(tri, x2, preferred_element_type=f32) + jnp.dot(tri, x3, preferred_element_type=f32)))


def _mm_kernel(x_ref, w_ref, o_ref):
    o_ref[...] = jnp.dot(x_ref[...].astype(jnp.bfloat16), w_ref[...],
                         preferred_element_type=jnp.float32).astype(o_ref.dtype)


def _matmul(x, w, tm, tn):
    m, k = x.shape
    n = w.shape[1]
    return pl.pallas_call(
        _mm_kernel,
        grid=(m // tm, n // tn),
        in_specs=[pl.BlockSpec((tm, k), lambda i, j: (i, 0)),
                  pl.BlockSpec((k, tn), lambda i, j: (0, j))],
        out_specs=pl.BlockSpec((tm, tn), lambda i, j: (i, j)),
        out_shape=jax.ShapeDtypeStruct((m, n), jnp.float32),
        compiler_params=_params("arbitrary", "arbitrary"),
        name="matmul",
    )(x, w)


def _modulated_norm(x_ref, g_ref, sc_ref, sh_ref, store):
    g = g_ref[...]
    sc = 1.0 + sc_ref[...]
    sh = sh_ref[...]

    def body(r, carry):
        rows = pl.ds(pl.multiple_of(r * NORM_ROWS, NORM_ROWS), NORM_ROWS)
        x = x_ref[rows, :]
        y = x * lax.rsqrt(jnp.mean(x * x, axis=-1, keepdims=True) + NORM_EPS)
        store(rows, (y * g) * sc + sh)
        return carry

    lax.fori_loop(0, x_ref.shape[0] // NORM_ROWS, body, 0)


def _norm_mm_kernel(x_ref, g_ref, sc_ref, sh_ref, w_ref, o_ref, h_ref):
    @pl.when(pl.program_id(1) == 0)
    def _():
        def store(rows, h):
            h_ref[rows, :] = h.astype(jnp.bfloat16)
        _modulated_norm(x_ref, g_ref, sc_ref, sh_ref, store)

    o_ref[...] = jnp.dot(h_ref[...], w_ref[...], preferred_element_type=jnp.float32)


def _norm_matmul(x, gain, scale, shift, w):
    n = w.shape[1]
    mod_spec = pl.BlockSpec((None, 1, D_MODEL), lambda i, j: (_mod_row(i), 0, 0))
    return pl.pallas_call(
        _norm_mm_kernel,
        grid=(N_TOK // ROW_TILE, n // COL_TILE),
        in_specs=[pl.BlockSpec((ROW_TILE, D_MODEL), lambda i, j: (i, 0)),
                  pl.BlockSpec((1, D_MODEL), lambda i, j: (0, 0)),
                  mod_spec, mod_spec,
                  pl.BlockSpec((D_MODEL, COL_TILE), lambda i, j: (0, j))],
        out_specs=pl.BlockSpec((ROW_TILE, COL_TILE), lambda i, j: (i, j)),
        out_shape=jax.ShapeDtypeStruct((N_TOK, n), jnp.float32),
        scratch_shapes=[pltpu.VMEM((ROW_TILE, D_MODEL), jnp.bfloat16)],
        compiler_params=_params("arbitrary", "arbitrary"),
        name="norm_in_proj",
    )(x, gain, scale, shift, w)


def _conv_kernel(prev_ref, x_ref, next_ref, w_ref, b_ref, o_ref, ext_ref, *, tiles_per_seq):
    i = pl.program_id(0)
    t = x_ref.shape[0]
    pad = SSD_CONV // 2
    first = (i % tiles_per_seq) == 0
    last = (i % tiles_per_seq) == tiles_per_seq - 1
    ext_ref[0:SUBLANES, :] = jnp.where(first, 0.0, prev_ref[...])
    ext_ref[SUBLANES:SUBLANES + t, :] = x_ref[...]
    ext_ref[SUBLANES + t:2 * SUBLANES + t, :] = jnp.where(last, 0.0, next_ref[...])
    for r0 in range(0, t, CONV_SUB):
        y = jnp.broadcast_to(b_ref[...], (CONV_SUB, CONV_COLS))
        for j in range(SSD_CONV):
            start = SUBLANES - pad + j + r0
            y = y + ext_ref[start:start + CONV_SUB, :] * w_ref[j:j + 1, :]
        o_ref[r0:r0 + CONV_SUB, :] = y * jax.nn.sigmoid(y)


def _ssd_conv(proj, row0, n_rows, seq_len, w, b):
    t = min(seq_len, CONV_ROWS)
    r0 = row0 // t
    c0 = OFF_XBC // CONV_COLS
    sub = t // SUBLANES
    n_sub = proj.shape[0] // SUBLANES
    return pl.pallas_call(
        functools.partial(_conv_kernel, tiles_per_seq=seq_len // t),
        grid=(n_rows // t, SSD_CONV_DIM // CONV_COLS),
        in_specs=[pl.BlockSpec((SUBLANES, CONV_COLS), lambda i, j: (jnp.maximum((r0 + i) * sub - 1, 0), c0 + j)),
                  pl.BlockSpec((t, CONV_COLS), lambda i, j: (r0 + i, c0 + j)),
                  pl.BlockSpec((SUBLANES, CONV_COLS),
                               lambda i, j: (jnp.minimum((r0 + i + 1) * sub, n_sub - 1), c0 + j)),
                  pl.BlockSpec((SSD_CONV, CONV_COLS), lambda i, j: (0, j)),
                  pl.BlockSpec((1, CONV_COLS), lambda i, j: (0, j))],
        out_specs=pl.BlockSpec((t, CONV_COLS), lambda i, j: (i, j)),
        out_shape=jax.ShapeDtypeStruct((n_rows, SSD_CONV_DIM), jnp.float32),
        scratch_shapes=[pltpu.VMEM((t + 2 * SUBLANES, CONV_COLS), jnp.float32)],
        compiler_params=_params("arbitrary", "arbitrary"),
        name="ssd_conv",
    )(proj, proj, proj, w, b)


def _softplus(x):
    return jnp.maximum(x, 0.0) + jnp.log1p(jnp.exp(-jnp.abs(x)))


def _expand(xs, sel, terms):
    parts = [jnp.concatenate(_split3(x)[:terms], axis=1) for x in xs]
    out = jnp.dot(jnp.concatenate(parts, axis=0), sel, preferred_element_type=jnp.float32)
    rows = xs[0].shape[0]
    return [out[i * rows:(i + 1) * rows] for i in range(len(xs))]


def _ssd_kernel(xf_ref, xb_ref, dtf_ref, dtb_ref, dtbias_ref, aneg_ref, init_ref, yf_ref, yb_ref, st_ref,
                s_ref, *, nc):
    bf16, f32 = jnp.bfloat16, jnp.float32
    c = SSD_C
    hd, ns = SSD_HEAD_DIM, SSD_STATE
    pair_w = 2 * hd
    j = pl.program_id(1)

    @pl.when(j == 0)
    def _():
        s_ref[...] = init_ref[...]

    t_ids = lax.broadcasted_iota(jnp.int32, (c, c), 0)
    s_ids = lax.broadcasted_iota(jnp.int32, (c, c), 1)
    eye = (lax.broadcasted_iota(jnp.int32, (SSD_HEADS, SSD_HEADS), 0)
           == lax.broadcasted_iota(jnp.int32, (SSD_HEADS, SSD_HEADS), 1)).astype(bf16)
    head_of = lambda terms, n, w: (lax.broadcasted_iota(jnp.int32, (terms * SSD_HEADS, n), 1) // w
                                   == lax.broadcasted_iota(jnp.int32, (terms * SSD_HEADS, n), 0) % SSD_HEADS
                                   ).astype(bf16)
    sel_x = head_of(2, SSD_DIM, hd)
    sel_c = head_of(3, SSD_HEADS * c, c)
    low_lanes = lax.broadcasted_iota(jnp.int32, (c, pair_w), 1) < hd
    low_rows = lax.broadcasted_iota(jnp.int32, (pair_w, ns), 0) < hd

    for d, (x_ref, dt_ref, y_ref) in enumerate(((xf_ref, dtf_ref, yf_ref), (xb_ref, dtb_ref, yb_ref))):
        rev = d == 1
        causal = (s_ids >= t_ids) if rev else (s_ids <= t_ids)
        tri = causal.astype(bf16)
        dt = _softplus(dt_ref[:, d * SSD_HEADS:(d + 1) * SSD_HEADS] + dtbias_ref[d:d + 1, :])
        acs = _tri_cumsum(tri, dt * aneg_ref[d:d + 1, :])
        acs_t = sum(lax.dot_general(eye, part, _NT, preferred_element_type=f32) for part in _split3(acs))
        end = acs[0:1, :] if rev else acs[c - 1:c, :]
        dt_x, out_x, in_x = _expand([dt, jnp.exp(end - acs), jnp.exp(acs)], sel_x, 2)
        acs_c, = _expand([acs], sel_c, 3)
        end_decay = jnp.exp(end)
        for g in range(SSD_GROUPS):
            bg = x_ref[:, SSD_DIM + g * ns:SSD_DIM + (g + 1) * ns].astype(bf16)
            cg = x_ref[:, SSD_DIM + (SSD_GROUPS + g) * ns:SSD_DIM + (SSD_GROUPS + g + 1) * ns].astype(bf16)
            cb = lax.dot_general(cg, bg, _NT, preferred_element_type=f32)
            for p in range(g * 2, g * 2 + 2):
                lanes = slice(p * pair_w, (p + 1) * pair_w)
                scores = []
                for h in (2 * p, 2 * p + 1):
                    decay = jnp.exp(jnp.minimum(acs_c[:, h * c:(h + 1) * c] - acs_t[h:h + 1, :], 0.0))
                    scores.append(jnp.where(causal, cb * decay, 0.0).astype(bf16))
                xdt = x_ref[:, lanes] * dt_x[:, lanes]
                rhs = jnp.concatenate([jnp.where(low_lanes, xdt, 0.0), jnp.where(low_lanes, 0.0, xdt)],
                                      axis=0).astype(bf16)
                s_p = s_ref[d, p]
                y = (jnp.dot(jnp.concatenate(scores, axis=1), rhs, preferred_element_type=f32)
                     + lax.dot_general(cg, s_p.astype(bf16), _NT, preferred_element_type=f32) * in_x[:, lanes])
                y_ref[:, lanes] = y
                keep = jnp.where(low_rows, end_decay[:, 2 * p:2 * p + 1], end_decay[:, 2 * p + 1:2 * p + 2])
                s_ref[d, p] = s_p * keep + lax.dot_general((xdt * out_x[:, lanes]).astype(bf16), bg, _TN,
                                                           preferred_element_type=f32)

    @pl.when(j == nc - 1)
    def _():
        st_ref[...] = s_ref[...]


def _ssd_scan(xbc, proj, row0, nb, length, dt_bias, a_neg, init):
    c = SSD_C
    nc = length // c
    r0 = row0 // c
    cdt = OFF_DT // LANES
    pair_state = (nb, 2, SSD_HEADS // 2, 2 * SSD_HEAD_DIM, SSD_STATE)
    state_spec = pl.BlockSpec((None,) + pair_state[1:], lambda b, j: (b, 0, 0, 0, 0))
    y_f, y_b, states = pl.pallas_call(
        functools.partial(_ssd_kernel, nc=nc),
        grid=(nb, nc),
        in_specs=[pl.BlockSpec((c, SSD_CONV_DIM), lambda b, j: (b * nc + j, 0)),
                  pl.BlockSpec((c, SSD_CONV_DIM), lambda b, j: (b * nc + nc - 1 - j, 0)),
                  pl.BlockSpec((c, LANES), lambda b, j: (r0 + b * nc + j, cdt)),
                  pl.BlockSpec((c, LANES), lambda b, j: (r0 + b * nc + nc - 1 - j, cdt)),
                  pl.BlockSpec((2, SSD_HEADS), lambda b, j: (0, 0)),
                  pl.BlockSpec((2, SSD_HEADS), lambda b, j: (0, 0)),
                  state_spec],
        out_specs=[pl.BlockSpec((c, SSD_DIM), lambda b, j: (b * nc + j, 0)),
                   pl.BlockSpec((c, SSD_DIM), lambda b, j: (b * nc + nc - 1 - j, 0)),
                   state_spec],
        out_shape=[jax.ShapeDtypeStruct((nb * length, SSD_DIM), jnp.float32),
                   jax.ShapeDtypeStruct((nb * length, SSD_DIM), jnp.float32),
                   jax.ShapeDtypeStruct(pair_state, jnp.float32)],
        scratch_shapes=[pltpu.VMEM(pair_state[1:], jnp.float32)],
        compiler_params=_params("arbitrary", "arbitrary"),
        name="ssd_scan",
    )(xbc, xbc, proj, proj, dt_bias, a_neg, init.reshape(pair_state))
    return y_f, y_b, states.reshape(nb, 2, SSD_HEADS, SSD_HEAD_DIM, SSD_STATE)


def _ssd_finish_kernel(yf_ref, yb_ref, x_ref, z_ref, d_ref, g_ref, o_ref):
    z = z_ref[...]
    y = (yf_ref[...] + yb_ref[...] + d_ref[...] * x_ref[...]) * (z * jax.nn.sigmoid(z))
    o_ref[...] = (y * lax.rsqrt(jnp.mean(y * y, axis=-1, keepdims=True) + NORM_EPS) * g_ref[...]).astype(o_ref.dtype)


def _ssd_finish(y_f, y_b, xbc, proj, row0, d_row, gain):
    n = y_f.shape[0]
    t = FINISH_ROWS
    r0 = row0 // t
    row_spec = pl.BlockSpec((t, SSD_DIM), lambda i: (i, 0))
    vec_spec = pl.BlockSpec((1, SSD_DIM), lambda i: (0, 0))
    return pl.pallas_call(
        _ssd_finish_kernel,
        grid=(n // t,),
        in_specs=[row_spec, row_spec, row_spec,
                  pl.BlockSpec((t, SSD_DIM), lambda i: (r0 + i, OFF_Z // SSD_DIM)),
                  vec_spec, vec_spec],
        out_specs=row_spec,
        out_shape=jax.ShapeDtypeStruct((n, SSD_DIM), jnp.bfloat16),
        compiler_params=_params("arbitrary"),
        name="ssd_finish",
    )(y_f, y_b, xbc, proj, d_row, gain)


def _hgrn_chunk(q, g, kk, v, s_t, rev, tri, lane_mod, diag, level_masks, b_ref, kk_ref):
    bf16, f32 = jnp.bfloat16, jnp.float32
    c = q.shape[0]
    ng = c // SUBLANES
    b = _tri_cumsum(tri, g)
    yield
    b_ref[...] = b
    kk_ref[...] = kk
    row = lambda a_ref, r: a_ref[r:r + 1, :]
    grp = lambda a, i: a[i * SUBLANES:(i + 1) * SUBLANES, :]
    b_end = row(b_ref, 0) if rev else row(b_ref, c - 1)

    ones = jnp.ones((HG_KDIM, c), bf16)
    diag_rows = []
    for i0 in range(0, ng, DIAG_BATCH):
        tiles = []
        for i in range(i0, i0 + DIAG_BATCH):
            qg, bg = grp(q, i), grp(b, i)
            tiles += [qg * jnp.exp(bg - row(b_ref, i * SUBLANES + j)) * row(kk_ref, i * SUBLANES + j)
                      for j in range(SUBLANES)]
        sums = jnp.dot(jnp.concatenate(tiles, axis=0).astype(bf16), ones, preferred_element_type=f32)
        for n in range(DIAG_BATCH):
            base = n * SUBLANES * SUBLANES
            acc = sums[base:base + SUBLANES, :]
            for j in range(1, SUBLANES):
                acc = jnp.where(lane_mod[j], sums[base + j * SUBLANES:base + (j + 1) * SUBLANES, :], acc)
            diag_rows.append(acc)
        yield
    att = jnp.where(diag, jnp.concatenate(diag_rows, axis=0), 0.0)

    for m, mask in zip(HG_LEVELS, level_masks):
        half = m // 2
        q_side, k_side = [], []
        for i in range(ng):
            start = (i * SUBLANES) // m * m
            later = (i * SUBLANES) % m >= half
            ref = row(b_ref, start + half if rev else start + half - 1)
            if later != rev:
                q_side.append(grp(q, i) * jnp.exp(grp(b, i) - ref))
                k_side.append(jnp.zeros((SUBLANES, HG_KDIM), f32))
            else:
                q_side.append(jnp.zeros((SUBLANES, HG_KDIM), f32))
                k_side.append(grp(kk, i) * jnp.exp(ref - grp(b, i)))
        a_m = lax.dot_general(jnp.concatenate(q_side, axis=0).astype(bf16),
                              jnp.concatenate(k_side, axis=0).astype(bf16), _NT, preferred_element_type=f32)
        att = att + (a_m if m == c else jnp.where(mask, a_m, 0.0))
        yield

    q_in = (q * jnp.exp(b)).astype(bf16)
    o = (lax.dot_general(q_in, s_t.astype(bf16), _NT, preferred_element_type=f32)
         + jnp.dot(att.astype(bf16), v.astype(bf16), preferred_element_type=f32))
    k_out = (kk * jnp.exp(b_end - b)).astype(bf16)
    s_new = s_t * jnp.exp(b_end) + lax.dot_general(v.astype(bf16), k_out, _TN, preferred_element_type=f32)
    return o, s_new


def _in_lockstep(gens):
    results = [None] * len(gens)
    live = list(range(len(gens)))
    while live:
        for i in list(live):
            try:
                next(gens[i])
            except StopIteration as stop:
                results[i] = stop.value
                live.remove(i)
    return results


def _hgrn_kernel(q_ref, ff_ref, fb_ref, v_ref, gate_ref, lb_ref, gn_ref, init_ref, o_ref, st_ref,
                 acc_ref, s_ref, b_ref, kk_ref, *, nc):
    c = HG_C
    t_ids = lax.broadcasted_iota(jnp.int32, (c, c), 0)
    s_ids = lax.broadcasted_iota(jnp.int32, (c, c), 1)
    causal = (s_ids <= t_ids, s_ids >= t_ids)
    tri = tuple(m.astype(jnp.bfloat16) for m in causal)
    same_group = (t_ids // SUBLANES) == (s_ids // SUBLANES)
    diag = tuple(m & same_group for m in causal)
    lane_mod = [(s_ids[:SUBLANES] % SUBLANES) == j for j in range(SUBLANES)]

    def level_mask(m, rev):
        same = (t_ids // m) == (s_ids // m)
        t_late = (t_ids % m) >= m // 2
        s_late = (s_ids % m) >= m // 2
        return same & (t_late != s_late) & (t_late != rev)
    masks = tuple([level_mask(m, rev) for m in HG_LEVELS] for rev in (False, True))

    s_ref[0] = init_ref[0].T
    s_ref[1] = init_ref[1].T
    gain = gn_ref[...]

    def run_pair(j):
        rows, gens = [], []
        for d, chunk in ((0, j), (1, nc - 1 - j)):
            r = pl.ds(pl.multiple_of(chunk * c, c), c)
            x = q_ref[r, :]
            q = x * jax.nn.sigmoid(x)
            lb = lb_ref[d:d + 1, :]
            f = lb + (1.0 - lb) * jax.nn.sigmoid((ff_ref, fb_ref)[d][r, :])
            rows.append(r)
            gens.append(_hgrn_chunk(q, jnp.log(f), 1.0 - f, v_ref[r, :], s_ref[d], d == 1,
                                    tri[d], lane_mod, diag[d], masks[d], b_ref.at[d], kk_ref.at[d]))
        outs = _in_lockstep(gens)
        for d in (0, 1):
            s_ref[d] = outs[d][1]
        return rows, [o for o, _ in outs]

    def first_half(j, carry):
        rows, outs = run_pair(j)
        for r, o in zip(rows, outs):
            acc_ref[r, :] = o
        return carry

    def second_half(j, carry):
        rows, outs = run_pair(j)
        for r, o in zip(rows, outs):
            o = o + acc_ref[r, :]
            y = o * lax.rsqrt(jnp.mean(o * o, axis=-1, keepdims=True) + NORM_EPS) * gain
            gate = gate_ref[r, :]
            o_ref[r, :] = (y * (gate * jax.nn.sigmoid(gate))).astype(o_ref.dtype)
        return carry

    lax.fori_loop(0, nc // 2, first_half, 0)
    lax.fori_loop(nc // 2, nc, second_half, 0)
    st_ref[0] = s_ref[0].T
    st_ref[1] = s_ref[1].T


def _hgrn_mixer(proj, row0, nb, length, lb, gain, init):
    nc = length // HG_C
    assert nc % 2 == 0
    seq = lambda off: pl.BlockSpec((length, HG_KDIM), lambda b, h, col=off // HG_KDIM: (row0 + b, col + h))
    state_spec = pl.BlockSpec((None, 2, None, HG_KDIM, HG_VDIM), lambda b, h: (b, 0, h, 0, 0))
    return pl.pallas_call(
        functools.partial(_hgrn_kernel, nc=nc),
        grid=(nb, HG_HEADS),
        in_specs=[seq(OFF_HQ), seq(OFF_HF), seq(OFF_HF + HG_FDIM), seq(OFF_HI), seq(OFF_HG),
                  pl.BlockSpec((2, HG_KDIM), lambda b, h: (0, h)),
                  pl.BlockSpec((1, HG_VDIM), lambda b, h: (0, h)),
                  state_spec],
        out_specs=[pl.BlockSpec((length, HG_VDIM), lambda b, h: (b, h)), state_spec],
        out_shape=[jax.ShapeDtypeStruct((nb * length, HG_DIM), jnp.bfloat16),
                   jax.ShapeDtypeStruct((nb, 2, HG_HEADS, HG_KDIM, HG_VDIM), jnp.float32)],
        scratch_shapes=[pltpu.VMEM((length, HG_VDIM), jnp.float32),
                        pltpu.VMEM((2, HG_VDIM, HG_KDIM), jnp.float32),
                        pltpu.VMEM((2, HG_C, HG_KDIM), jnp.float32),
                        pltpu.VMEM((2, HG_C, HG_KDIM), jnp.float32)],
        compiler_params=_params("arbitrary", "arbitrary"),
        name="hgrn_mixer",
    )(proj, proj, proj, proj, proj, lb, gain, init)


def _softmax_av(scores, values):
    f32, bf16 = jnp.float32, jnp.bfloat16
    m = functools.reduce(jnp.maximum, [jnp.max(s, axis=-1, keepdims=True) for s in scores])
    ps = [jnp.exp(s - m) for s in scores]
    denom = functools.reduce(jnp.add, [jnp.sum(p, axis=-1, keepdims=True) for p in ps])
    acc = functools.reduce(jnp.add, [jnp.dot(p.astype(bf16), v, preferred_element_type=f32)
                                     for p, v in zip(ps, values)])
    return acc / denom


def _ctx_attn_kernel(q_ref, k_ref, v_ref, o_ref):
    bf16 = jnp.bfloat16
    scale = ATT_HEAD_DIM ** -0.5
    s = lax.dot_general(q_ref[...].astype(bf16), k_ref[...].astype(bf16), _NT,
                        preferred_element_type=jnp.float32) * scale
    o_ref[...] = _softmax_av([s], [v_ref[...].astype(bf16)]).astype(o_ref.dtype)


def _context_attention(proj, nb, length):
    spec = lambda off: pl.BlockSpec((length, ATT_HEAD_DIM), lambda b, h, col=off // ATT_HEAD_DIM: (b, col + h))
    return pl.pallas_call(
        _ctx_attn_kernel,
        grid=(nb, ATT_HEADS),
        in_specs=[spec(OFF_AQ), spec(OFF_AK), spec(OFF_AV)],
        out_specs=pl.BlockSpec((length, ATT_HEAD_DIM), lambda b, h: (b, h)),
        out_shape=jax.ShapeDtypeStruct((nb * length, ATT_DIM), jnp.bfloat16),
        compiler_params=_params("arbitrary", "arbitrary"),
        name="context_attention",
    )(proj, proj, proj)


def _window_bias(rpb):
    col = jnp.arange(GRID_W)
    cs = jnp.clip(col - WIN_COLS // 2, 0, GRID_W - WIN_COLS)
    col_mask = (col[None, :] >= cs[:, None]) & (col[None, :] < cs[:, None] + WIN_COLS)
    dc_idx = jnp.clip(col[None, :] - col[:, None] + WIN_COLS - 1, 0, 2 * WIN_COLS - 2)
    bias = jnp.where(col_mask, rpb[:, :, dc_idx].astype(jnp.float32), MASKED)
    wins = [bias[:, d0:d0 + WIN_ROWS].transpose(0, 2, 1, 3).reshape(rpb.shape[0], GRID_W, WIN_ROWS * GRID_W)
            for d0 in range(WIN_ROWS)]
    return jnp.stack(wins, axis=1)


def _natten_kernel(q_ref, k_ref, v_ref, kc_ref, vc_ref, bias_ref, o_ref, kb_ref, vb_ref, *, rows):
    bf16, f32 = jnp.bfloat16, jnp.float32
    scale = ATT_HEAD_DIM ** -0.5
    win = WIN_ROWS * GRID_W

    def cast(i, carry):
        sl = pl.ds(pl.multiple_of(i * CAST_ROWS, CAST_ROWS), CAST_ROWS)
        kb_ref[sl, :] = k_ref[sl, :].astype(bf16)
        vb_ref[sl, :] = v_ref[sl, :].astype(bf16)
        return carry
    lax.fori_loop(0, rows * GRID_W // CAST_ROWS, cast, 0)

    kc = kc_ref[...].astype(bf16)
    vc = vc_ref[...].astype(bf16)

    def row_block(r, carry):
        rs = jnp.clip(r - WIN_ROWS // 2, 0, rows - WIN_ROWS)
        d0 = rs - r + WIN_ROWS - 1
        q = q_ref[pl.ds(pl.multiple_of(r * GRID_W, GRID_W), GRID_W), :].astype(bf16)
        keys = pl.ds(pl.multiple_of(rs * GRID_W, GRID_W), win)
        s_lat = lax.dot_general(q, kb_ref[keys, :], _NT, preferred_element_type=f32) * scale + bias_ref[d0]
        s_ctx = lax.dot_general(q, kc, _NT, preferred_element_type=f32) * scale
        o = _softmax_av([s_lat, s_ctx], [vb_ref[keys, :], vc])
        o_ref[pl.ds(pl.multiple_of(r * GRID_W, GRID_W), GRID_W), :] = o.astype(o_ref.dtype)
        return carry
    lax.fori_loop(0, rows, row_block, 0, unroll=ROW_UNROLL)


def _neighbourhood_attention(proj, row0, nb, length, cache_k, cache_v, layer, bias_win):
    rows = length // GRID_W
    past = cache_k.shape[2]
    spec = lambda off: pl.BlockSpec((length, ATT_HEAD_DIM),
                                    lambda b, h, col=off // ATT_HEAD_DIM: (row0 + b, col + h))
    cache_spec = pl.BlockSpec((None, None, past, ATT_HEAD_DIM), lambda b, h: (b, layer, 0, h))
    return pl.pallas_call(
        functools.partial(_natten_kernel, rows=rows),
        grid=(nb, ATT_HEADS),
        in_specs=[spec(OFF_AQ), spec(OFF_AK), spec(OFF_AV), cache_spec, cache_spec,
                  pl.BlockSpec((None, WIN_ROWS, GRID_W, WIN_ROWS * GRID_W), lambda b, h: (h, 0, 0, 0))],
        out_specs=pl.BlockSpec((length, ATT_HEAD_DIM), lambda b, h: (b, h)),
        out_shape=jax.ShapeDtypeStruct((nb * length, ATT_DIM), jnp.bfloat16),
        scratch_shapes=[pltpu.VMEM((length, ATT_HEAD_DIM), jnp.bfloat16),
                        pltpu.VMEM((length, ATT_HEAD_DIM), jnp.bfloat16)],
        compiler_params=_params("arbitrary", "arbitrary"),
        name="neighbourhood_attention",
    )(proj, proj, proj, cache_k, cache_v, bias_win)


def _merge_kernel(ca_ref, cb_ref, cc_ref, la_ref, lb_ref, lc_ref, wa_ref, wb_ref, wc_ref,
                  ga_ref, gb_ref, gc_ref, o_ref):
    f32 = jnp.float32
    is_ctx = pl.program_id(0) < N_CTX // ROW_TILE

    def merge(ya_ref, yb_ref, yc_ref):
        acc = jax.nn.sigmoid(ga_ref[...]) * jnp.dot(ya_ref[...], wa_ref[...], preferred_element_type=f32)
        acc = acc + jax.nn.sigmoid(gb_ref[...]) * jnp.dot(yb_ref[...], wb_ref[...], preferred_element_type=f32)
        acc = acc + jax.nn.sigmoid(gc_ref[...]) * jnp.dot(yc_ref[...], wc_ref[...], preferred_element_type=f32)
        o_ref[...] = acc.astype(o_ref.dtype)

    @pl.when(is_ctx)
    def _():
        merge(ca_ref, cb_ref, cc_ref)

    @pl.when(jnp.logical_not(is_ctx))
    def _():
        merge(la_ref, lb_ref, lc_ref)


def _branch_merge(ys_ctx, ys_lat, ws, proj):
    tm, tn = ROW_TILE, MERGE_COLS
    kdim = ws[0].shape[0]
    cg = OFF_GATES // tn
    per = D_MODEL // tn
    ctx_tiles = N_CTX // tm
    ctx_spec = pl.BlockSpec((tm, kdim), lambda i, j: (jnp.minimum(i, ctx_tiles - 1), 0))
    lat_spec = pl.BlockSpec((tm, kdim), lambda i, j: (jnp.maximum(i - ctx_tiles, 0), 0))
    w_spec = pl.BlockSpec((kdim, tn), lambda i, j: (0, j))
    g_spec = lambda b: pl.BlockSpec((tm, tn), lambda i, j, b=b: (i, cg + b * per + j))
    return pl.pallas_call(
        _merge_kernel,
        grid=(N_TOK // tm, D_MODEL // tn),
        in_specs=[ctx_spec] * N_BRANCH + [lat_spec] * N_BRANCH + [w_spec] * N_BRANCH
                 + [g_spec(b) for b in range(N_BRANCH)],
        out_specs=pl.BlockSpec((tm, tn), lambda i, j: (i, j)),
        out_shape=jax.ShapeDtypeStruct((N_TOK, D_MODEL), jnp.bfloat16),
        compiler_params=_params("arbitrary", "arbitrary"),
        name="branch_merge",
    )(*ys_ctx, *ys_lat, *ws, proj, proj, proj)


def _out_residual_kernel(m_ref, w_ref, x_ref, gate_ref, o_ref):
    o_ref[...] = x_ref[...] + gate_ref[...] * jnp.dot(m_ref[...], w_ref[...], preferred_element_type=jnp.float32)


def _out_residual(merged, w, x, gate):
    tm, tn = ROW_TILE, COL_TILE
    return pl.pallas_call(
        _out_residual_kernel,
        grid=(N_TOK // tm, D_MODEL // tn),
        in_specs=[pl.BlockSpec((tm, D_MODEL), lambda i, j: (i, 0)),
                  pl.BlockSpec((D_MODEL, tn), lambda i, j: (0, j)),
                  pl.BlockSpec((tm, tn), lambda i, j: (i, j)),
                  pl.BlockSpec((None, 1, tn), lambda i, j: (_mod_row(i), 0, j))],
        out_specs=pl.BlockSpec((tm, tn), lambda i, j: (i, j)),
        out_shape=jax.ShapeDtypeStruct((N_TOK, D_MODEL), jnp.float32),
        compiler_params=_params("arbitrary", "arbitrary"),
        name="out_residual",
    )(merged, w, x, gate)


def _first_max(vals):
    best, idx = vals[0], jnp.zeros(vals[0].shape, jnp.int32)
    for k in range(1, len(vals)):
        better = vals[k] > best
        best = jnp.where(better, vals[k], best)
        idx = jnp.where(better, k, idx)
    return best, idx


def _pick(idx, vals):
    out = vals[0]
    for k in range(1, len(vals)):
        out = jnp.where(idx == k, vals[k], out)
    return out


def _norm_router_kernel(x_ref, g_ref, sc_ref, sh_ref, wr_hi_ref, wr_lo_ref, rb_ref,
                        h_ref, ids_ref, wts_ref, cnt_ref, lo_ref, base_ref):
    f32, bf16 = jnp.float32, jnp.bfloat16
    t = x_ref.shape[0]

    @pl.when(pl.program_id(0) == 0)
    def _():
        base_ref[...] = jnp.zeros_like(base_ref)

    def store(rows, h):
        h_hi = h.astype(bf16)
        h_ref[rows, :] = h_hi
        lo_ref[rows, :] = (h - h_hi.astype(f32)).astype(bf16)
    _modulated_norm(x_ref, g_ref, sc_ref, sh_ref, store)

    logits = (lax.dot_general(wr_hi_ref[...], h_ref[...], _NT, preferred_element_type=f32)
              + (lax.dot_general(wr_lo_ref[...], h_ref[...], _NT, preferred_element_type=f32)
                 + lax.dot_general(wr_hi_ref[...], lo_ref[...], _NT, preferred_element_type=f32)))
    scores = jax.nn.sigmoid(logits)
    sel = scores + rb_ref[...]
    row = lambda a, e: a[e:e + 1, :]

    group_scores = []
    for g in range(N_EXPERT_GROUPS):
        v = [row(sel, g * EXPERTS_PER_GROUP + k) for k in range(EXPERTS_PER_GROUP)]
        pair_sums = [v[a] + v[b] for a in range(EXPERTS_PER_GROUP) for b in range(a + 1, EXPERTS_PER_GROUP)]
        group_scores.append(functools.reduce(jnp.maximum, pair_sums))
    _, grp = _first_max(group_scores)

    in_sel = [_pick(grp, [row(sel, g * EXPERTS_PER_GROUP + k) for g in range(N_EXPERT_GROUPS)])
              for k in range(EXPERTS_PER_GROUP)]
    in_score = [_pick(grp, [row(scores, g * EXPERTS_PER_GROUP + k) for g in range(N_EXPERT_GROUPS)])
                for k in range(EXPERTS_PER_GROUP)]
    _, i1 = _first_max(in_sel)
    _, i2 = _first_max([jnp.where(i1 == k, -jnp.inf, in_sel[k]) for k in range(EXPERTS_PER_GROUP)])
    s1 = _pick(i1, in_score)
    s2 = _pick(i2, in_score)
    e1 = grp * EXPERTS_PER_GROUP + i1
    e2 = grp * EXPERTS_PER_GROUP + i2

    e_ids = lax.broadcasted_iota(jnp.int32, (N_EXPERTS, t), 0)
    hit1 = e_ids == e1
    hit2 = e_ids == e2
    cnt = jnp.where(hit1 | hit2, 1.0, 0.0).astype(bf16)
    before = (lax.broadcasted_iota(jnp.int32, (t, t), 0) < lax.broadcasted_iota(jnp.int32, (t, t), 1)).astype(bf16)
    prefix = jnp.dot(cnt, before, preferred_element_type=f32) + base_ref[:, 0:1]
    rank1 = jnp.sum(jnp.where(hit1, prefix, 0.0), axis=0, keepdims=True)
    rank2 = jnp.sum(jnp.where(hit2, prefix, 0.0), axis=0, keepdims=True)
    base_ref[...] = base_ref[...] + jnp.dot(cnt, jnp.ones((t, LANES), bf16), preferred_element_type=f32)
    cnt_ref[...] = base_ref[...]

    zeros = jnp.zeros((SUBLANES - 4, t), jnp.int32)
    ids_ref[...] = jnp.concatenate([e1, e2, rank1.astype(jnp.int32), rank2.astype(jnp.int32), zeros], axis=0)
    total = s1 + s2
    wts_ref[...] = jnp.concatenate([s1 / total, s2 / total, jnp.zeros((SUBLANES - 2, t), f32)], axis=0)


def _norm_router(x, gain, scale, shift, wr_hi_t, wr_lo_t, router_bias):
    t = ROUTER_TILE
    per = ROW_TILE // t
    mod_spec = pl.BlockSpec((None, 1, D_MODEL), lambda i: (_mod_row(i // per), 0, 0))
    return pl.pallas_call(
        _norm_router_kernel,
        grid=(N_TOK // t,),
        in_specs=[pl.BlockSpec((t, D_MODEL), lambda i: (i, 0)),
                  pl.BlockSpec((1, D_MODEL), lambda i: (0, 0)),
                  mod_spec, mod_spec,
                  pl.BlockSpec((N_EXPERTS, D_MODEL), lambda i: (0, 0)),
                  pl.BlockSpec((N_EXPERTS, D_MODEL), lambda i: (0, 0)),
                  pl.BlockSpec((N_EXPERTS, 1), lambda i: (0, 0))],
        out_specs=[pl.BlockSpec((t, D_MODEL), lambda i: (i, 0)),
                   pl.BlockSpec((SUBLANES, t), lambda i: (0, i)),
                   pl.BlockSpec((SUBLANES, t), lambda i: (0, i)),
                   pl.BlockSpec((N_EXPERTS, LANES), lambda i: (0, 0))],
        out_shape=[jax.ShapeDtypeStruct((N_TOK, D_MODEL), jnp.bfloat16),
                   jax.ShapeDtypeStruct((SUBLANES, N_TOK), jnp.int32),
                   jax.ShapeDtypeStruct((SUBLANES, N_TOK), jnp.float32),
                   jax.ShapeDtypeStruct((N_EXPERTS, LANES), jnp.float32)],
        scratch_shapes=[pltpu.VMEM((t, D_MODEL), jnp.bfloat16),
                        pltpu.VMEM((N_EXPERTS, LANES), jnp.float32)],
        compiler_params=_params("arbitrary"),
        name="norm_router",
    )(x, gain, scale, shift, wr_hi_t, wr_lo_t, router_bias)


def _expert_kernel(be_ref, na_ref, x_ref, w1_ref, w3_ref, w2_ref, o_ref):
    active = pl.program_id(0) < na_ref[0]

    @pl.when(active)
    def _():
        bf16 = jnp.bfloat16
        x = x_ref[...]
        a = jnp.dot(x, w1_ref[...].astype(bf16), preferred_element_type=jnp.float32)
        b = jnp.dot(x, w3_ref[...].astype(bf16), preferred_element_type=jnp.float32)
        hdn = (a * jax.nn.sigmoid(a)) * b
        o_ref[...] = jnp.dot(hdn.astype(bf16), w2_ref[...].astype(bf16),
                             preferred_element_type=jnp.float32).astype(o_ref.dtype)

    @pl.when(jnp.logical_not(active))
    def _():
        o_ref[...] = jnp.zeros_like(o_ref)


def _expert_blocks(buf, block_e, n_active, w1, w3, w2):
    n_blocks = buf.shape[0] // MOE_BLOCK
    w_spec = lambda r, c: pl.BlockSpec((None, r, c), lambda i, be, na: (be[i], 0, 0), pipeline_mode=pl.Buffered(1))
    grid_spec = pltpu.PrefetchScalarGridSpec(
        num_scalar_prefetch=2,
        grid=(n_blocks,),
        in_specs=[pl.BlockSpec((MOE_BLOCK, D_MODEL), lambda i, be, na: (i, 0)),
                  w_spec(D_MODEL, D_EXPERT), w_spec(D_MODEL, D_EXPERT), w_spec(D_EXPERT, D_MODEL)],
        out_specs=pl.BlockSpec((MOE_BLOCK, D_MODEL), lambda i, be, na: (i, 0)),
    )
    return pl.pallas_call(
        _expert_kernel,
        grid_spec=grid_spec,
        out_shape=jax.ShapeDtypeStruct((n_blocks * MOE_BLOCK, D_MODEL), jnp.bfloat16),
        compiler_params=_params("arbitrary"),
        name="moe_experts",
    )(block_e, n_active, buf, w1, w3, w2)


def _moe(h, ids, wts, counts, layer, w1, w3, w2):
    nk = N_TOK * TOP_K
    n_blocks = (nk + N_EXPERTS * (MOE_BLOCK - 1)) // MOE_BLOCK
    counts = counts[:, 0].astype(jnp.int32)
    padded = (counts + MOE_BLOCK - 1) // MOE_BLOCK * MOE_BLOCK
    pad_end = jnp.cumsum(padded)
    pad_start = pad_end - padded
    dest1 = pad_start[ids[0]] + ids[2]
    dest2 = pad_start[ids[1]] + ids[3]
    tok = jnp.arange(N_TOK, dtype=jnp.int32)
    src = jnp.zeros((n_blocks * MOE_BLOCK,), jnp.int32).at[jnp.concatenate([dest1, dest2])].set(
        jnp.concatenate([tok, tok]))
    block_e = jnp.minimum(jnp.searchsorted(pad_end, jnp.arange(n_blocks, dtype=jnp.int32) * MOE_BLOCK, side='right'),
                          N_EXPERTS - 1).astype(jnp.int32)
    n_active = (pad_end[-1:] // MOE_BLOCK).astype(jnp.int32)
    out = _expert_blocks(h[src], block_e + layer * N_EXPERTS, n_active, w1, w3, w2)
    return wts[0][:, None] * out[dest1] + wts[1][:, None] * out[dest2]


def _final_norm_kernel(x_ref, g_ref, o_ref):
    g = g_ref[...]

    def body(r, carry):
        rows = pl.ds(pl.multiple_of(r * NORM_ROWS, NORM_ROWS), NORM_ROWS)
        x = x_ref[rows, :]
        o_ref[rows, :] = (x * lax.rsqrt(jnp.mean(x * x, axis=-1, keepdims=True) + NORM_EPS)) * g
        return carry

    lax.fori_loop(0, x_ref.shape[0] // NORM_ROWS, body, 0)


def _final_norm(x, gain):
    return pl.pallas_call(
        _final_norm_kernel,
        grid=(N_TOK // ROW_TILE,),
        in_specs=[pl.BlockSpec((ROW_TILE, D_MODEL), lambda i: (i, 0)),
                  pl.BlockSpec((1, D_MODEL), lambda i: (0, 0))],
        out_specs=pl.BlockSpec((ROW_TILE, D_MODEL), lambda i: (i, 0)),
        out_shape=jax.ShapeDtypeStruct((N_TOK, D_MODEL), jnp.float32),
        compiler_params=_params("arbitrary"),
        name="final_norm",
    )(x, gain)


def _permute_w_in(w):
    c0 = SSD_DIM + SSD_CONV_DIM
    c1 = c0 + 2 * SSD_HEADS
    dt_cols = jnp.pad(w[:, c0:c1], ((0, 0), (0, DT_PAD - 2 * SSD_HEADS)))
    return jnp.concatenate([w[:, :c0], w[:, c1:], dt_cols], axis=1).astype(jnp.bfloat16)


def _per_token(m):
    m = m[:, 0]
    return jnp.concatenate([jnp.broadcast_to(m[:1], (N_CTX, D_MODEL)), jnp.repeat(m[1:], DEC_SEQ, axis=0)], axis=0)


def kernel(x_prompt, x_sample, cache_k, cache_v, state_ssd, state_hgrn, c, c_ctx, w_ada, b_ada, norm_mix, norm_moe, w_in, ssd_conv_w, ssd_conv_b, ssd_dt_bias, ssd_a_log, ssd_d, ssd_norm, hg_lb_logits, hg_norm, att_rpb, w_br_ssd, w_br_hg, w_br_att, w_out, w_router, router_bias, moe_w1, moe_w3, moe_w2, final_norm):
    bf16, f32 = jnp.bfloat16, jnp.float32
    lb_cum = jnp.cumsum(jax.nn.softmax(hg_lb_logits.astype(f32), axis=1), axis=1)
    lower_bounds = lb_cum - lb_cum[:, :1]

    x = jnp.concatenate([x_prompt.reshape(N_CTX, D_MODEL), x_sample.reshape(N_LAT, D_MODEL)], axis=0)

    n_mod = 1 + DEC_BATCH
    cond = jnp.concatenate([c_ctx[None, :], c], axis=0)
    cond = jnp.pad(jax.nn.silu(cond), ((0, 2 * SUBLANES - n_mod), (0, 0)))

    wr_t = w_router.T
    wr_hi = wr_t.astype(bf16)
    wr_lo = (wr_t - wr_hi.astype(f32)).astype(bf16)
    cache_k = cache_k.reshape(DEC_BATCH, DEPTH, -1, ATT_DIM)
    cache_v = cache_v.reshape(DEC_BATCH, DEPTH, -1, ATT_DIM)
    lat_row0 = N_CTX // DEC_SEQ
    zero_ssd = jnp.zeros((BATCH, 2, SSD_HEADS, SSD_HEAD_DIM, SSD_STATE), f32)
    zero_hg = jnp.zeros((BATCH, 2, HG_HEADS, HG_KDIM, HG_VDIM), f32)

    expert_w = tuple(w.reshape((DEPTH * N_EXPERTS,) + w.shape[2:]) for w in (moe_w1, moe_w3, moe_w2))

    new_k, new_v, new_ssd, new_hg = [], [], [], []
    for l in range(DEPTH):
        mod = _matmul(cond, w_ada[l].astype(bf16), 2 * SUBLANES, COL_TILE)[:n_mod] + b_ada[l]
        mod = mod.reshape(n_mod, 6, 1, D_MODEL)
        shift_m, scale_m, gate_m, shift_f, scale_f, gate_f = (mod[:, i] for i in range(6))

        proj = _norm_matmul(x, norm_mix[l][None, :], scale_m, shift_m, _permute_w_in(w_in[l]))
        new_k.append(proj[:N_CTX, OFF_AK:OFF_AK + ATT_DIM].reshape(BATCH, SEQ, ATT_HEADS, ATT_HEAD_DIM))
        new_v.append(proj[:N_CTX, OFF_AV:OFF_AV + ATT_DIM].reshape(BATCH, SEQ, ATT_HEADS, ATT_HEAD_DIM))

        conv_b = ssd_conv_b[l][None, :]
        a_neg = -jnp.exp(ssd_a_log[l].astype(f32))
        d_row = jnp.repeat(ssd_d[l], SSD_HEAD_DIM)[None, :]
        ssd_gain = ssd_norm[l][None, :]
        y_ssd = []
        for row0, nb, length, init in ((0, BATCH, SEQ, zero_ssd), (N_CTX, DEC_BATCH, DEC_SEQ, state_ssd[:, l])):
            xbc = _ssd_conv(proj, row0, nb * length, length, ssd_conv_w[l], conv_b)
            y_f, y_b, states = _ssd_scan(xbc, proj, row0, nb, length, ssd_dt_bias[l], a_neg, init)
            y_ssd.append(_ssd_finish(y_f, y_b, xbc, proj, row0, d_row, ssd_gain))
            if row0 == 0:
                new_ssd.append(states)

        lb = lower_bounds[:, l]
        hg_gain = hg_norm[l].reshape(1, HG_DIM)
        y_hg_ctx, states = _hgrn_mixer(proj, 0, BATCH, SEQ, lb, hg_gain, zero_hg)
        new_hg.append(states)
        y_hg_lat, _ = _hgrn_mixer(proj, lat_row0, DEC_BATCH, DEC_SEQ, lb, hg_gain, state_hgrn[:, l])

        y_att_ctx = _context_attention(proj, BATCH, SEQ)
        y_att_lat = _neighbourhood_attention(proj, lat_row0, DEC_BATCH, DEC_SEQ, cache_k, cache_v, l,
                                             _window_bias(att_rpb[l]))

        merged = _branch_merge((y_ssd[0], y_hg_ctx, y_att_ctx), (y_ssd[1], y_hg_lat, y_att_lat),
                               (w_br_ssd[l].astype(bf16), w_br_hg[l].astype(bf16), w_br_att[l].astype(bf16)), proj)
        x = _out_residual(merged, w_out[l].astype(bf16), x, gate_m)

        h2, ids, wts, counts = _norm_router(x, norm_moe[l][None, :], scale_f, shift_f, wr_hi, wr_lo,
                                            router_bias.astype(f32)[:, None])
        moe = _moe(h2, ids, wts, counts, l, *expert_w)
        x = x + _per_token(gate_f) * moe

    y = _final_norm(x, final_norm[None, :])
    y_prompt = y[:N_CTX].reshape(BATCH, SEQ, D_MODEL)
    y_sample = y[N_CTX:].reshape(DEC_BATCH, DEC_SEQ, D_MODEL)
    return (y_prompt, y_sample, jnp.stack(new_k, axis=1), jnp.stack(new_v, axis=1),
            jnp.stack(new_ssd, axis=1), jnp.stack(new_hg, axis=1))
```

```python
import functools

import jax
import jax.numpy as jnp
from jax import lax
from jax.experimental import pallas as pl
from jax.experimental.pallas import tpu as pltpu

D_MODEL = 2048
BATCH = 32
SEQ = 256
DEPTH = 2
DEC_BATCH = 8
DEC_SEQ = 4096
GRID_W = 64
NORM_EPS = 1e-6
SSD_HEADS = 16
SSD_HEAD_DIM = 64
SSD_DIM = SSD_HEADS * SSD_HEAD_DIM
SSD_STATE = 64
SSD_GROUPS = 4
SSD_CONV = 5
SSD_CONV_DIM = SSD_DIM + 2 * SSD_GROUPS * SSD_STATE
HG_HEADS = 8
HG_KDIM = 128
HG_VDIM = 128
HG_FDIM = HG_HEADS * HG_KDIM
HG_DIM = HG_HEADS * HG_VDIM
ATT_HEADS = 8
ATT_HEAD_DIM = 128
ATT_DIM = ATT_HEADS * ATT_HEAD_DIM
WIN_ROWS = 8
WIN_COLS = 16
N_BRANCH = 3
N_EXPERTS = 16
N_EXPERT_GROUPS = 4
EXPERTS_PER_GROUP = N_EXPERTS // N_EXPERT_GROUPS
TOP_K = 2
D_EXPERT = 1024

N_CTX = BATCH * SEQ
N_LAT = DEC_BATCH * DEC_SEQ
N_TOK = N_CTX + N_LAT

VMEM_LIMIT_BYTES = 56 * 1024 * 1024
LANES = 128
SUBLANES = 8

OFF_Z = 0
OFF_XBC = OFF_Z + SSD_DIM
OFF_HQ = OFF_XBC + SSD_CONV_DIM
OFF_HF = OFF_HQ + HG_FDIM
OFF_HI = OFF_HF + 2 * HG_FDIM
OFF_HG = OFF_HI + HG_DIM
OFF_AQ = OFF_HG + HG_DIM
OFF_AK = OFF_AQ + ATT_DIM
OFF_AV = OFF_AK + ATT_DIM
OFF_GATES = OFF_AV + ATT_DIM
OFF_DT = OFF_GATES + N_BRANCH * D_MODEL
DT_PAD = 512
PROJ_DIM = OFF_DT + DT_PAD

ROW_TILE = 1024
COL_TILE = 1024
MERGE_COLS = 512
NORM_ROWS = 64
ROUTER_TILE = 512
MOE_BLOCK = 512
CONV_ROWS = 1024
CONV_SUB = 256
CONV_COLS = 512
SSD_C = 128
FINISH_ROWS = 256
HG_C = 128
HG_LEVELS = (16, 32, 64, 128)
CAST_ROWS = 512
COMBINE_ROWS = 256
COMBINE_UNROLL = 8
DIAG_BATCH = 4
ROW_UNROLL = 4
HG_CHUNKS_PER_STEP = 2
MASKED = -1e30

_NT = (((1,), (1,)), ((), ()))
_TN = (((0,), (0,)), ((), ()))


def _params(*semantics):
    return pltpu.CompilerParams(dimension_semantics=semantics, vmem_limit_bytes=VMEM_LIMIT_BYTES)


def _mod_row(i):
    ctx_tiles = N_CTX // ROW_TILE
    tiles_per_req = DEC_SEQ // ROW_TILE
    return jnp.where(i < ctx_tiles, 0, 1 + (i - ctx_tiles) // tiles_per_req)


def _split3(x):
    bf16, f32 = jnp.bfloat16, jnp.float32
    x1 = x.astype(bf16)
    r = x - x1.astype(f32)
    x2 = r.astype(bf16)
    x3 = (r - x2.astype(f32)).astype(bf16)
    return x1, x2, x3


def _tri_cumsum(tri, x):
    x1, x2, x3 = _split3(x)
    f32 = jnp.float32
    return (jnp.dot(tri, x1, preferred_element_type=f32)
            + (jnp.dot(tri, x2, preferred_element_type=f32) + jnp.dot(tri, x3, preferred_element_type=f32)))


def _mm_kernel(x_ref, w_ref, o_ref):
    o_ref[...] = jnp.dot(x_ref[...].astype(jnp.bfloat16), w_ref[...],
                         preferred_element_type=jnp.float32).astype(o_ref.dtype)


def _matmul(x, w, tm, tn):
    m, k = x.shape
    n = w.shape[1]
    return pl.pallas_call(
        _mm_kernel,
        grid=(m // tm, n // tn),
        in_specs=[pl.BlockSpec((tm, k), lambda i, j: (i, 0)),
                  pl.BlockSpec((k, tn), lambda i, j: (0, j))],
        out_specs=pl.BlockSpec((tm, tn), lambda i, j: (i, j)),
        out_shape=jax.ShapeDtypeStruct((m, n), jnp.float32),
        compiler_params=_params("arbitrary", "arbitrary"),
        name="matmul",
    )(x, w)


def _modulated_norm(x_ref, g_ref, sc_ref, sh_ref, store):
    g = g_ref[...]
    sc = 1.0 + sc_ref[...]
    sh = sh_ref[...]

    def body(r, carry):
        rows = pl.ds(pl.multiple_of(r * NORM_ROWS, NORM_ROWS), NORM_ROWS)
        x = x_ref[rows, :]
        y = x * lax.rsqrt(jnp.mean(x * x, axis=-1, keepdims=True) + NORM_EPS)
        store(rows, (y * g) * sc + sh)
        return carry

    lax.fori_loop(0, x_ref.shape[0] // NORM_ROWS, body, 0)


def _norm_mm_kernel(x_ref, g_ref, sc_ref, sh_ref, w_ref, o_ref, h_ref):
    @pl.when(pl.program_id(1) == 0)
    def _():
        def store(rows, h):
            h_ref[rows, :] = h.astype(jnp.bfloat16)
        _modulated_norm(x_ref, g_ref, sc_ref, sh_ref, store)

    o_ref[...] = jnp.dot(h_ref[...], w_ref[...], preferred_element_type=jnp.float32)


def _norm_matmul(x, gain, scale, shift, w):
    n = w.shape[1]
    mod_spec = pl.BlockSpec((None, 1, D_MODEL), lambda i, j: (_mod_row(i), 0, 0))
    return pl.pallas_call(
        _norm_mm_kernel,
        grid=(N_TOK // ROW_TILE, n // COL_TILE),
        in_specs=[pl.BlockSpec((ROW_TILE, D_MODEL), lambda i, j: (i, 0)),
                  pl.BlockSpec((1, D_MODEL), lambda i, j: (0, 0)),
                  mod_spec, mod_spec,
                  pl.BlockSpec((D_MODEL, COL_TILE), lambda i, j: (0, j))],
        out_specs=pl.BlockSpec((ROW_TILE, COL_TILE), lambda i, j: (i, j)),
        out_shape=jax.ShapeDtypeStruct((N_TOK, n), jnp.float32),
        scratch_shapes=[pltpu.VMEM((ROW_TILE, D_MODEL), jnp.bfloat16)],
        compiler_params=_params("arbitrary", "arbitrary"),
        name="norm_in_proj",
    )(x, gain, scale, shift, w)


def _conv_kernel(prev_ref, x_ref, next_ref, w_ref, b_ref, o_ref, ext_ref, *, tiles_per_seq):
    i = pl.program_id(0)
    t = x_ref.shape[0]
    pad = SSD_CONV // 2
    first = (i % tiles_per_seq) == 0
    last = (i % tiles_per_seq) == tiles_per_seq - 1
    ext_ref[0:SUBLANES, :] = jnp.where(first, 0.0, prev_ref[...])
    ext_ref[SUBLANES:SUBLANES + t, :] = x_ref[...]
    ext_ref[SUBLANES + t:2 * SUBLANES + t, :] = jnp.where(last, 0.0, next_ref[...])
    for r0 in range(0, t, CONV_SUB):
        y = jnp.broadcast_to(b_ref[...], (CONV_SUB, CONV_COLS))
        for j in range(SSD_CONV):
            start = SUBLANES - pad + j + r0
            y = y + ext_ref[start:start + CONV_SUB, :] * w_ref[j:j + 1, :]
        o_ref[r0:r0 + CONV_SUB, :] = y * jax.nn.sigmoid(y)


def _ssd_conv(proj, row0, n_rows, seq_len, w, b):
    t = min(seq_len, CONV_ROWS)
    r0 = row0 // t
    c0 = OFF_XBC // CONV_COLS
    sub = t // SUBLANES
    n_sub = proj.shape[0] // SUBLANES
    return pl.pallas_call(
        functools.partial(_conv_kernel, tiles_per_seq=seq_len // t),
        grid=(n_rows // t, SSD_CONV_DIM // CONV_COLS),
        in_specs=[pl.BlockSpec((SUBLANES, CONV_COLS), lambda i, j: (jnp.maximum((r0 + i) * sub - 1, 0), c0 + j)),
                  pl.BlockSpec((t, CONV_COLS), lambda i, j: (r0 + i, c0 + j)),
                  pl.BlockSpec((SUBLANES, CONV_COLS),
                               lambda i, j: (jnp.minimum((r0 + i + 1) * sub, n_sub - 1), c0 + j)),
                  pl.BlockSpec((SSD_CONV, CONV_COLS), lambda i, j: (0, j)),
                  pl.BlockSpec((1, CONV_COLS), lambda i, j: (0, j))],
        out_specs=pl.BlockSpec((t, CONV_COLS), lambda i, j: (i, j)),
        out_shape=jax.ShapeDtypeStruct((n_rows, SSD_CONV_DIM), jnp.float32),
        scratch_shapes=[pltpu.VMEM((t + 2 * SUBLANES, CONV_COLS), jnp.float32)],
        compiler_params=_params("arbitrary", "arbitrary"),
        name="ssd_conv",
    )(proj, proj, proj, w, b)


def _softplus(x):
    return jnp.maximum(x, 0.0) + jnp.log1p(jnp.exp(-jnp.abs(x)))


def _expand(xs, sel, terms):
    parts = [jnp.concatenate(_split3(x)[:terms], axis=1) for x in xs]
    out = jnp.dot(jnp.concatenate(parts, axis=0), sel, preferred_element_type=jnp.float32)
    rows = xs[0].shape[0]
    return [out[i * rows:(i + 1) * rows] for i in range(len(xs))]


def _ssd_kernel(xf_ref, xb_ref, dtf_ref, dtb_ref, dtbias_ref, aneg_ref, init_ref, yf_ref, yb_ref, st_ref,
                s_ref, *, nc):
    bf16, f32 = jnp.bfloat16, jnp.float32
    c = SSD_C
    hd, ns = SSD_HEAD_DIM, SSD_STATE
    pair_w = 2 * hd
    j = pl.program_id(1)

    @pl.when(j == 0)
    def _():
        s_ref[...] = init_ref[...]

    t_ids = lax.broadcasted_iota(jnp.int32, (c, c), 0)
    s_ids = lax.broadcasted_iota(jnp.int32, (c, c), 1)
    eye = (lax.broadcasted_iota(jnp.int32, (SSD_HEADS, SSD_HEADS), 0)
           == lax.broadcasted_iota(jnp.int32, (SSD_HEADS, SSD_HEADS), 1)).astype(bf16)
    head_of = lambda terms, n, w: (lax.broadcasted_iota(jnp.int32, (terms * SSD_HEADS, n), 1) // w
                                   == lax.broadcasted_iota(jnp.int32, (terms * SSD_HEADS, n), 0) % SSD_HEADS
                                   ).astype(bf16)
    sel_x = head_of(2, SSD_DIM, hd)
    sel_c = head_of(3, SSD_HEADS * c, c)
    low_lanes = lax.broadcasted_iota(jnp.int32, (c, pair_w), 1) < hd
    low_rows = lax.broadcasted_iota(jnp.int32, (pair_w, ns), 0) < hd

    for d, (x_ref, dt_ref, y_ref) in enumerate(((xf_ref, dtf_ref, yf_ref), (xb_ref, dtb_ref, yb_ref))):
        rev = d == 1
        causal = (s_ids >= t_ids) if rev else (s_ids <= t_ids)
        tri = causal.astype(bf16)
        dt = _softplus(dt_ref[:, d * SSD_HEADS:(d + 1) * SSD_HEADS] + dtbias_ref[d:d + 1, :])
        acs = _tri_cumsum(tri, dt * aneg_ref[d:d + 1, :])
        acs_t = sum(lax.dot_general(eye, part, _NT, preferred_element_type=f32) for part in _split3(acs))
        end = acs[0:1, :] if rev else acs[c - 1:c, :]
        dt_x, out_x, in_x = _expand([dt, jnp.exp(end - acs), jnp.exp(acs)], sel_x, 2)
        acs_c, = _expand([acs], sel_c, 3)
        end_decay = jnp.exp(end)
        for g in range(SSD_GROUPS):
            bg = x_ref[:, SSD_DIM + g * ns:SSD_DIM + (g + 1) * ns].astype(bf16)
            cg = x_ref[:, SSD_DIM + (SSD_GROUPS + g) * ns:SSD_DIM + (SSD_GROUPS + g + 1) * ns].astype(bf16)
            cb = lax.dot_general(cg, bg, _NT, preferred_element_type=f32)
            for p in range(g * 2, g * 2 + 2):
                lanes = slice(p * pair_w, (p + 1) * pair_w)
                scores = []
                for h in (2 * p, 2 * p + 1):
                    decay = jnp.exp(jnp.minimum(acs_c[:, h * c:(h + 1) * c] - acs_t[h:h + 1, :], 0.0))
                    scores.append(jnp.where(causal, cb * decay, 0.0).astype(bf16))
                xdt = x_ref[:, lanes] * dt_x[:, lanes]
                rhs = jnp.concatenate([jnp.where(low_lanes, xdt, 0.0), jnp.where(low_lanes, 0.0, xdt)],
                                      axis=0).astype(bf16)
                s_p = s_ref[d, p]
                y = (jnp.dot(jnp.concatenate(scores, axis=1), rhs, preferred_element_type=f32)
                     + lax.dot_general(cg, s_p.astype(bf16), _NT, preferred_element_type=f32) * in_x[:, lanes])
                y_ref[:, lanes] = y
                keep = jnp.where(low_rows, end_decay[:, 2 * p:2 * p + 1], end_decay[:, 2 * p + 1:2 * p + 2])
                s_ref[d, p] = s_p * keep + lax.dot_general((xdt * out_x[:, lanes]).astype(bf16), bg, _TN,
                                                           preferred_element_type=f32)

    @pl.when(j == nc - 1)
    def _():
        st_ref[...] = s_ref[...]


def _ssd_scan(xbc, proj, row0, nb, length, dt_bias, a_neg, init):
    c = SSD_C
    nc = length // c
    r0 = row0 // c
    cdt = OFF_DT // LANES
    pair_state = (nb, 2, SSD_HEADS // 2, 2 * SSD_HEAD_DIM, SSD_STATE)
    state_spec = pl.BlockSpec((None,) + pair_state[1:], lambda b, j: (b, 0, 0, 0, 0))
    y_f, y_b, states = pl.pallas_call(
        functools.partial(_ssd_kernel, nc=nc),
        grid=(nb, nc),
        in_specs=[pl.BlockSpec((c, SSD_CONV_DIM), lambda b, j: (b * nc + j, 0)),
                  pl.BlockSpec((c, SSD_CONV_DIM), lambda b, j: (b * nc + nc - 1 - j, 0)),
                  pl.BlockSpec((c, LANES), lambda b, j: (r0 + b * nc + j, cdt)),
                  pl.BlockSpec((c, LANES), lambda b, j: (r0 + b * nc + nc - 1 - j, cdt)),
                  pl.BlockSpec((2, SSD_HEADS), lambda b, j: (0, 0)),
                  pl.BlockSpec((2, SSD_HEADS), lambda b, j: (0, 0)),
                  state_spec],
        out_specs=[pl.BlockSpec((c, SSD_DIM), lambda b, j: (b * nc + j, 0)),
                   pl.BlockSpec((c, SSD_DIM), lambda b, j: (b * nc + nc - 1 - j, 0)),
                   state_spec],
        out_shape=[jax.ShapeDtypeStruct((nb * length, SSD_DIM), jnp.float32),
                   jax.ShapeDtypeStruct((nb * length, SSD_DIM), jnp.float32),
                   jax.ShapeDtypeStruct(pair_state, jnp.float32)],
        scratch_shapes=[pltpu.VMEM(pair_state[1:], jnp.float32)],
        compiler_params=_params("arbitrary", "arbitrary"),
        name="ssd_scan",
    )(xbc, xbc, proj, proj, dt_bias, a_neg, init.reshape(pair_state))
    return y_f, y_b, states.reshape(nb, 2, SSD_HEADS, SSD_HEAD_DIM, SSD_STATE)


def _ssd_finish_kernel(yf_ref, yb_ref, x_ref, z_ref, d_ref, g_ref, o_ref):
    z = z_ref[...]
    y = (yf_ref[...] + yb_ref[...] + d_ref[...] * x_ref[...]) * (z * jax.nn.sigmoid(z))
    o_ref[...] = (y * lax.rsqrt(jnp.mean(y * y, axis=-1, keepdims=True) + NORM_EPS) * g_ref[...]).astype(o_ref.dtype)


def _ssd_finish(y_f, y_b, xbc, proj, row0, d_row, gain):
    n = y_f.shape[0]
    t = FINISH_ROWS
    r0 = row0 // t
    row_spec = pl.BlockSpec((t, SSD_DIM), lambda i: (i, 0))
    vec_spec = pl.BlockSpec((1, SSD_DIM), lambda i: (0, 0))
    return pl.pallas_call(
        _ssd_finish_kernel,
        grid=(n // t,),
        in_specs=[row_spec, row_spec, row_spec,
                  pl.BlockSpec((t, SSD_DIM), lambda i: (r0 + i, OFF_Z // SSD_DIM)),
                  vec_spec, vec_spec],
        out_specs=row_spec,
        out_shape=jax.ShapeDtypeStruct((n, SSD_DIM), jnp.bfloat16),
        compiler_params=_params("arbitrary"),
        name="ssd_finish",
    )(y_f, y_b, xbc, proj, d_row, gain)


def _hgrn_chunk(q, g, kk, v, state, rev, tri, lane_mod, diag, level_masks, b_ref, kk_ref):
    bf16, f32 = jnp.bfloat16, jnp.float32
    c = q.shape[0]
    ng = c // SUBLANES
    b = _tri_cumsum(tri, g)
    yield
    b_ref[...] = b
    kk_ref[...] = kk
    row = lambda a_ref, r: a_ref[r:r + 1, :]
    grp = lambda a, i: a[i * SUBLANES:(i + 1) * SUBLANES, :]
    b_end = row(b_ref, 0) if rev else row(b_ref, c - 1)

    ones = jnp.ones((HG_KDIM, c), bf16)
    diag_rows = []
    for i0 in range(0, ng, DIAG_BATCH):
        tiles = []
        for i in range(i0, i0 + DIAG_BATCH):
            qg, bg = grp(q, i), grp(b, i)
            tiles += [qg * jnp.exp(bg - row(b_ref, i * SUBLANES + j)) * row(kk_ref, i * SUBLANES + j)
                      for j in range(SUBLANES)]
        sums = jnp.dot(jnp.concatenate(tiles, axis=0).astype(bf16), ones, preferred_element_type=f32)
        for n in range(DIAG_BATCH):
            base = n * SUBLANES * SUBLANES
            acc = sums[base:base + SUBLANES, :]
            for j in range(1, SUBLANES):
                acc = jnp.where(lane_mod[j], sums[base + j * SUBLANES:base + (j + 1) * SUBLANES, :], acc)
            diag_rows.append(acc)
        yield
    att = jnp.where(diag, jnp.concatenate(diag_rows, axis=0), 0.0)

    for m, mask in zip(HG_LEVELS, level_masks):
        half = m // 2
        q_side, k_side = [], []
        for i in range(ng):
            start = (i * SUBLANES) // m * m
            later = (i * SUBLANES) % m >= half
            ref = row(b_ref, start + half if rev else start + half - 1)
            if later != rev:
                q_side.append(grp(q, i) * jnp.exp(grp(b, i) - ref))
                k_side.append(jnp.zeros((SUBLANES, HG_KDIM), f32))
            else:
                q_side.append(jnp.zeros((SUBLANES, HG_KDIM), f32))
                k_side.append(grp(kk, i) * jnp.exp(ref - grp(b, i)))
        a_m = lax.dot_general(jnp.concatenate(q_side, axis=0).astype(bf16),
                              jnp.concatenate(k_side, axis=0).astype(bf16), _NT, preferred_element_type=f32)
        att = att + (a_m if m == c else jnp.where(mask, a_m, 0.0))
        yield

    q_in = (q * jnp.exp(b)).astype(bf16)
    k_out = (kk * jnp.exp(b_end - b)).astype(bf16)
    yield
    s_t = state[0]
    o = (lax.dot_general(q_in, s_t.astype(bf16), _NT, preferred_element_type=f32)
         + jnp.dot(att.astype(bf16), v.astype(bf16), preferred_element_type=f32))
    state[0] = s_t * jnp.exp(b_end) + lax.dot_general(v.astype(bf16), k_out, _TN, preferred_element_type=f32)
    return o


def _in_lockstep(gens):
    results = [None] * len(gens)
    live = list(range(len(gens)))
    while live:
        for i in list(live):
            try:
                next(gens[i])
            except StopIteration as stop:
                results[i] = stop.value
                live.remove(i)
    return results


def _hgrn_kernel(q_ref, ff_ref, fb_ref, v_ref, gate_ref, lb_ref, gn_ref, init_ref, o_ref, st_ref,
                 acc_ref, s_ref, b_ref, kk_ref, *, nc):
    c = HG_C
    t_ids = lax.broadcasted_iota(jnp.int32, (c, c), 0)
    s_ids = lax.broadcasted_iota(jnp.int32, (c, c), 1)
    causal = (s_ids <= t_ids, s_ids >= t_ids)
    tri = tuple(m.astype(jnp.bfloat16) for m in causal)
    same_group = (t_ids // SUBLANES) == (s_ids // SUBLANES)
    diag = tuple(m & same_group for m in causal)
    lane_mod = [(s_ids[:SUBLANES] % SUBLANES) == j for j in range(SUBLANES)]

    def level_mask(m, rev):
        same = (t_ids // m) == (s_ids // m)
        t_late = (t_ids % m) >= m // 2
        s_late = (s_ids % m) >= m // 2
        return same & (t_late != s_late) & (t_late != rev)
    masks = tuple([level_mask(m, rev) for m in HG_LEVELS] for rev in (False, True))

    s_ref[0] = init_ref[0].T
    s_ref[1] = init_ref[1].T
    gain = gn_ref[...]

    per_step = HG_CHUNKS_PER_STEP if (nc // 2) % HG_CHUNKS_PER_STEP == 0 else 1
    half_steps = nc // 2 // per_step

    def run_step(j):
        states = [[s_ref[0]], [s_ref[1]]]
        rows, gens = [], []
        for u in range(per_step):
            for d in (0, 1):
                chunk = j * per_step + u if d == 0 else nc - 1 - (j * per_step + u)
                r = pl.ds(pl.multiple_of(chunk * c, c), c)
                x = q_ref[r, :]
                q = x * jax.nn.sigmoid(x)
                lb = lb_ref[d:d + 1, :]
                f = lb + (1.0 - lb) * jax.nn.sigmoid((ff_ref, fb_ref)[d][r, :])
                rows.append(r)
                gens.append(_hgrn_chunk(q, jnp.log(f), 1.0 - f, v_ref[r, :], states[d], d == 1, tri[d], lane_mod,
                                        diag[d], masks[d], b_ref.at[2 * u + d], kk_ref.at[2 * u + d]))
        outs = _in_lockstep(gens)
        for d in (0, 1):
            s_ref[d] = states[d][0]
        return rows, outs

    def first_half(j, carry):
        rows, outs = run_step(j)
        for r, o in zip(rows, outs):
            acc_ref[r, :] = o
        return carry

    def second_half(j, carry):
        rows, outs = run_step(j)
        for r, o in zip(rows, outs):
            o = o + acc_ref[r, :]
            y = o * lax.rsqrt(jnp.mean(o * o, axis=-1, keepdims=True) + NORM_EPS) * gain
            gate = gate_ref[r, :]
            o_ref[r, :] = (y * (gate * jax.nn.sigmoid(gate))).astype(o_ref.dtype)
        return carry

    lax.fori_loop(0, half_steps, first_half, 0)
    lax.fori_loop(half_steps, 2 * half_steps, second_half, 0)
    st_ref[0] = s_ref[0].T
    st_ref[1] = s_ref[1].T


def _hgrn_mixer(proj, row0, nb, length, lb, gain, init):
    nc = length // HG_C
    assert nc % 2 == 0
    seq = lambda off: pl.BlockSpec((length, HG_KDIM), lambda b, h, col=off // HG_KDIM: (row0 + b, col + h))
    state_spec = pl.BlockSpec((None, 2, None, HG_KDIM, HG_VDIM), lambda b, h: (b, 0, h, 0, 0))
    return pl.pallas_call(
        functools.partial(_hgrn_kernel, nc=nc),
        grid=(nb, HG_HEADS),
        in_specs=[seq(OFF_HQ), seq(OFF_HF), seq(OFF_HF + HG_FDIM), seq(OFF_HI), seq(OFF_HG),
                  pl.BlockSpec((2, HG_KDIM), lambda b, h: (0, h)),
                  pl.BlockSpec((1, HG_VDIM), lambda b, h: (0, h)),
                  state_spec],
        out_specs=[pl.BlockSpec((length, HG_VDIM), lambda b, h: (b, h)), state_spec],
        out_shape=[jax.ShapeDtypeStruct((nb * length, HG_DIM), jnp.bfloat16),
                   jax.ShapeDtypeStruct((nb, 2, HG_HEADS, HG_KDIM, HG_VDIM), jnp.float32)],
        scratch_shapes=[pltpu.VMEM((length, HG_VDIM), jnp.float32),
                        pltpu.VMEM((2, HG_VDIM, HG_KDIM), jnp.float32),
                        pltpu.VMEM((2 * HG_CHUNKS_PER_STEP, HG_C, HG_KDIM), jnp.float32),
                        pltpu.VMEM((2 * HG_CHUNKS_PER_STEP, HG_C, HG_KDIM), jnp.float32)],
        compiler_params=_params("arbitrary", "arbitrary"),
        name="hgrn_mixer",
    )(proj, proj, proj, proj, proj, lb, gain, init)


def _softmax_av(scores, values):
    f32, bf16 = jnp.float32, jnp.bfloat16
    m = functools.reduce(jnp.maximum, [jnp.max(s, axis=-1, keepdims=True) for s in scores])
    ps = [jnp.exp(s - m) for s in scores]
    denom = functools.reduce(jnp.add, [jnp.sum(p, axis=-1, keepdims=True) for p in ps])
    acc = functools.reduce(jnp.add, [jnp.dot(p.astype(bf16), v, preferred_element_type=f32)
                                     for p, v in zip(ps, values)])
    return acc / denom


def _ctx_attn_kernel(q_ref, k_ref, v_ref, o_ref):
    bf16 = jnp.bfloat16
    scale = ATT_HEAD_DIM ** -0.5
    s = lax.dot_general(q_ref[...].astype(bf16), k_ref[...].astype(bf16), _NT,
                        preferred_element_type=jnp.float32) * scale
    o_ref[...] = _softmax_av([s], [v_ref[...].astype(bf16)]).astype(o_ref.dtype)


def _context_attention(proj, nb, length):
    spec = lambda off: pl.BlockSpec((length, ATT_HEAD_DIM), lambda b, h, col=off // ATT_HEAD_DIM: (b, col + h))
    return pl.pallas_call(
        _ctx_attn_kernel,
        grid=(nb, ATT_HEADS),
        in_specs=[spec(OFF_AQ), spec(OFF_AK), spec(OFF_AV)],
        out_specs=pl.BlockSpec((length, ATT_HEAD_DIM), lambda b, h: (b, h)),
        out_shape=jax.ShapeDtypeStruct((nb * length, ATT_DIM), jnp.bfloat16),
        compiler_params=_params("arbitrary", "arbitrary"),
        name="context_attention",
    )(proj, proj, proj)


def _window_bias(rpb):
    col = jnp.arange(GRID_W)
    cs = jnp.clip(col - WIN_COLS // 2, 0, GRID_W - WIN_COLS)
    col_mask = (col[None, :] >= cs[:, None]) & (col[None, :] < cs[:, None] + WIN_COLS)
    dc_idx = jnp.clip(col[None, :] - col[:, None] + WIN_COLS - 1, 0, 2 * WIN_COLS - 2)
    bias = jnp.where(col_mask, rpb[:, :, dc_idx].astype(jnp.float32), MASKED)
    wins = [bias[:, d0:d0 + WIN_ROWS].transpose(0, 2, 1, 3).reshape(rpb.shape[0], GRID_W, WIN_ROWS * GRID_W)
            for d0 in range(WIN_ROWS)]
    return jnp.stack(wins, axis=1)


def _natten_kernel(q_ref, k_ref, v_ref, kc_ref, vc_ref, bias_ref, o_ref, kb_ref, vb_ref, *, rows):
    bf16, f32 = jnp.bfloat16, jnp.float32
    scale = ATT_HEAD_DIM ** -0.5
    win = WIN_ROWS * GRID_W

    def cast(i, carry):
        sl = pl.ds(pl.multiple_of(i * CAST_ROWS, CAST_ROWS), CAST_ROWS)
        kb_ref[sl, :] = k_ref[sl, :].astype(bf16)
        vb_ref[sl, :] = v_ref[sl, :].astype(bf16)
        return carry
    lax.fori_loop(0, rows * GRID_W // CAST_ROWS, cast, 0)

    kc = kc_ref[...].astype(bf16)
    vc = vc_ref[...].astype(bf16)

    def row_block(r):
        rs = jnp.clip(r - WIN_ROWS // 2, 0, rows - WIN_ROWS)
        d0 = rs - r + WIN_ROWS - 1
        q = q_ref[pl.ds(pl.multiple_of(r * GRID_W, GRID_W), GRID_W), :].astype(bf16)
        keys = pl.ds(pl.multiple_of(rs * GRID_W, GRID_W), win)
        s_lat = lax.dot_general(q, kb_ref[keys, :], _NT, preferred_element_type=f32) * scale + bias_ref[d0]
        s_ctx = lax.dot_general(q, kc, _NT, preferred_element_type=f32) * scale
        yield
        m = jnp.maximum(jnp.max(s_lat, axis=-1, keepdims=True), jnp.max(s_ctx, axis=-1, keepdims=True))
        yield
        p_lat = jnp.exp(s_lat - m)
        p_ctx = jnp.exp(s_ctx - m)
        denom = jnp.sum(p_lat, axis=-1, keepdims=True) + jnp.sum(p_ctx, axis=-1, keepdims=True)
        acc = (jnp.dot(p_lat.astype(bf16), vb_ref[keys, :], preferred_element_type=f32)
               + jnp.dot(p_ctx.astype(bf16), vc, preferred_element_type=f32))
        yield
        o_ref[pl.ds(pl.multiple_of(r * GRID_W, GRID_W), GRID_W), :] = (acc / denom).astype(o_ref.dtype)

    def row_group(g, carry):
        _in_lockstep([row_block(g * ROW_UNROLL + u) for u in range(ROW_UNROLL)])
        return carry
    lax.fori_loop(0, rows // ROW_UNROLL, row_group, 0)


def _neighbourhood_attention(proj, row0, nb, length, cache_k, cache_v, layer, bias_win):
    rows = length // GRID_W
    past = cache_k.shape[2]
    spec = lambda off: pl.BlockSpec((length, ATT_HEAD_DIM),
                                    lambda b, h, col=off // ATT_HEAD_DIM: (row0 + b, col + h))
    cache_spec = pl.BlockSpec((None, None, past, ATT_HEAD_DIM), lambda b, h: (b, layer, 0, h))
    return pl.pallas_call(
        functools.partial(_natten_kernel, rows=rows),
        grid=(nb, ATT_HEADS),
        in_specs=[spec(OFF_AQ), spec(OFF_AK), spec(OFF_AV), cache_spec, cache_spec,
                  pl.BlockSpec((None, WIN_ROWS, GRID_W, WIN_ROWS * GRID_W), lambda b, h: (h, 0, 0, 0))],
        out_specs=pl.BlockSpec((length, ATT_HEAD_DIM), lambda b, h: (b, h)),
        out_shape=jax.ShapeDtypeStruct((nb * length, ATT_DIM), jnp.bfloat16),
        scratch_shapes=[pltpu.VMEM((length, ATT_HEAD_DIM), jnp.bfloat16),
                        pltpu.VMEM((length, ATT_HEAD_DIM), jnp.bfloat16)],
        compiler_params=_params("arbitrary", "arbitrary"),
        name="neighbourhood_attention",
    )(proj, proj, proj, cache_k, cache_v, bias_win)


def _merge_kernel(ca_ref, cb_ref, cc_ref, la_ref, lb_ref, lc_ref, wa_ref, wb_ref, wc_ref,
                  ga_ref, gb_ref, gc_ref, o_ref):
    f32 = jnp.float32
    is_ctx = pl.program_id(0) < N_CTX // ROW_TILE

    def merge(ya_ref, yb_ref, yc_ref):
        acc = jax.nn.sigmoid(ga_ref[...]) * jnp.dot(ya_ref[...], wa_ref[...], preferred_element_type=f32)
        acc = acc + jax.nn.sigmoid(gb_ref[...]) * jnp.dot(yb_ref[...], wb_ref[...], preferred_element_type=f32)
        acc = acc + jax.nn.sigmoid(gc_ref[...]) * jnp.dot(yc_ref[...], wc_ref[...], preferred_element_type=f32)
        o_ref[...] = acc.astype(o_ref.dtype)

    @pl.when(is_ctx)
    def _():
        merge(ca_ref, cb_ref, cc_ref)

    @pl.when(jnp.logical_not(is_ctx))
    def _():
        merge(la_ref, lb_ref, lc_ref)


def _branch_merge(ys_ctx, ys_lat, ws, proj):
    tm, tn = ROW_TILE, MERGE_COLS
    kdim = ws[0].shape[0]
    cg = OFF_GATES // tn
    per = D_MODEL // tn
    ctx_tiles = N_CTX // tm
    ctx_spec = pl.BlockSpec((tm, kdim), lambda i, j: (jnp.minimum(i, ctx_tiles - 1), 0))
    lat_spec = pl.BlockSpec((tm, kdim), lambda i, j: (jnp.maximum(i - ctx_tiles, 0), 0))
    w_spec = pl.BlockSpec((kdim, tn), lambda i, j: (0, j))
    g_spec = lambda b: pl.BlockSpec((tm, tn), lambda i, j, b=b: (i, cg + b * per + j))
    return pl.pallas_call(
        _merge_kernel,
        grid=(N_TOK // tm, D_MODEL // tn),
        in_specs=[ctx_spec] * N_BRANCH + [lat_spec] * N_BRANCH + [w_spec] * N_BRANCH
                 + [g_spec(b) for b in range(N_BRANCH)],
        out_specs=pl.BlockSpec((tm, tn), lambda i, j: (i, j)),
        out_shape=jax.ShapeDtypeStruct((N_TOK, D_MODEL), jnp.bfloat16),
        compiler_params=_params("arbitrary", "arbitrary"),
        name="branch_merge",
    )(*ys_ctx, *ys_lat, *ws, proj, proj, proj)


def _out_residual_kernel(m_ref, w_ref, x_ref, gate_ref, o_ref):
    o_ref[...] = x_ref[...] + gate_ref[...] * jnp.dot(m_ref[...], w_ref[...], preferred_element_type=jnp.float32)


def _out_residual(merged, w, x, gate):
    tm, tn = ROW_TILE, COL_TILE
    return pl.pallas_call(
        _out_residual_kernel,
        grid=(N_TOK // tm, D_MODEL // tn),
        in_specs=[pl.BlockSpec((tm, D_MODEL), lambda i, j: (i, 0)),
                  pl.BlockSpec((D_MODEL, tn), lambda i, j: (0, j)),
                  pl.BlockSpec((tm, tn), lambda i, j: (i, j)),
                  pl.BlockSpec((None, 1, tn), lambda i, j: (_mod_row(i), 0, j))],
        out_specs=pl.BlockSpec((tm, tn), lambda i, j: (i, j)),
        out_shape=jax.ShapeDtypeStruct((N_TOK, D_MODEL), jnp.float32),
        compiler_params=_params("arbitrary", "arbitrary"),
        name="out_residual",
    )(merged, w, x, gate)


def _first_max(vals):
    best, idx = vals[0], jnp.zeros(vals[0].shape, jnp.int32)
    for k in range(1, len(vals)):
        better = vals[k] > best
        best = jnp.where(better, vals[k], best)
        idx = jnp.where(better, k, idx)
    return best, idx


def _pick(idx, vals):
    out = vals[0]
    for k in range(1, len(vals)):
        out = jnp.where(idx == k, vals[k], out)
    return out


def _norm_router_kernel(x_ref, g_ref, sc_ref, sh_ref, wr_hi_ref, wr_lo_ref, rb_ref,
                        h_ref, ids_ref, wts_ref, cnt_ref, lo_ref, base_ref):
    f32, bf16 = jnp.float32, jnp.bfloat16
    t = x_ref.shape[0]

    @pl.when(pl.program_id(0) == 0)
    def _():
        base_ref[...] = jnp.zeros_like(base_ref)

    def store(rows, h):
        h_hi = h.astype(bf16)
        h_ref[rows, :] = h_hi
        lo_ref[rows, :] = (h - h_hi.astype(f32)).astype(bf16)
    _modulated_norm(x_ref, g_ref, sc_ref, sh_ref, store)

    logits = (lax.dot_general(wr_hi_ref[...], h_ref[...], _NT, preferred_element_type=f32)
              + (lax.dot_general(wr_lo_ref[...], h_ref[...], _NT, preferred_element_type=f32)
                 + lax.dot_general(wr_hi_ref[...], lo_ref[...], _NT, preferred_element_type=f32)))
    scores = jax.nn.sigmoid(logits)
    sel = scores + rb_ref[...]
    row = lambda a, e: a[e:e + 1, :]

    group_scores = []
    for g in range(N_EXPERT_GROUPS):
        v = [row(sel, g * EXPERTS_PER_GROUP + k) for k in range(EXPERTS_PER_GROUP)]
        pair_sums = [v[a] + v[b] for a in range(EXPERTS_PER_GROUP) for b in range(a + 1, EXPERTS_PER_GROUP)]
        group_scores.append(functools.reduce(jnp.maximum, pair_sums))
    _, grp = _first_max(group_scores)

    in_sel = [_pick(grp, [row(sel, g * EXPERTS_PER_GROUP + k) for g in range(N_EXPERT_GROUPS)])
              for k in range(EXPERTS_PER_GROUP)]
    in_score = [_pick(grp, [row(scores, g * EXPERTS_PER_GROUP + k) for g in range(N_EXPERT_GROUPS)])
                for k in range(EXPERTS_PER_GROUP)]
    _, i1 = _first_max(in_sel)
    _, i2 = _first_max([jnp.where(i1 == k, -jnp.inf, in_sel[k]) for k in range(EXPERTS_PER_GROUP)])
    s1 = _pick(i1, in_score)
    s2 = _pick(i2, in_score)
    e1 = grp * EXPERTS_PER_GROUP + i1
    e2 = grp * EXPERTS_PER_GROUP + i2

    e_ids = lax.broadcasted_iota(jnp.int32, (N_EXPERTS, t), 0)
    hit1 = e_ids == e1
    hit2 = e_ids == e2
    cnt = jnp.where(hit1 | hit2, 1.0, 0.0).astype(bf16)
    before = (lax.broadcasted_iota(jnp.int32, (t, t), 0) < lax.broadcasted_iota(jnp.int32, (t, t), 1)).astype(bf16)
    prefix = jnp.dot(cnt, before, preferred_element_type=f32) + base_ref[:, 0:1]
    rank1 = jnp.sum(jnp.where(hit1, prefix, 0.0), axis=0, keepdims=True)
    rank2 = jnp.sum(jnp.where(hit2, prefix, 0.0), axis=0, keepdims=True)
    base_ref[...] = base_ref[...] + jnp.dot(cnt, jnp.ones((t, LANES), bf16), preferred_element_type=f32)
    cnt_ref[...] = base_ref[...]

    zeros = jnp.zeros((SUBLANES - 4, t), jnp.int32)
    ids_ref[...] = jnp.concatenate([e1, e2, rank1.astype(jnp.int32), rank2.astype(jnp.int32), zeros], axis=0)
    total = s1 + s2
    wts_ref[...] = jnp.concatenate([s1 / total, s2 / total, jnp.zeros((SUBLANES - 2, t), f32)], axis=0)


def _norm_router(x, gain, scale, shift, wr_hi_t, wr_lo_t, router_bias):
    t = ROUTER_TILE
    per = ROW_TILE // t
    mod_spec = pl.BlockSpec((None, 1, D_MODEL), lambda i: (_mod_row(i // per), 0, 0))
    return pl.pallas_call(
        _norm_router_kernel,
        grid=(N_TOK // t,),
        in_specs=[pl.BlockSpec((t, D_MODEL), lambda i: (i, 0)),
                  pl.BlockSpec((1, D_MODEL), lambda i: (0, 0)),
                  mod_spec, mod_spec,
                  pl.BlockSpec((N_EXPERTS, D_MODEL), lambda i: (0, 0)),
                  pl.BlockSpec((N_EXPERTS, D_MODEL), lambda i: (0, 0)),
                  pl.BlockSpec((N_EXPERTS, 1), lambda i: (0, 0))],
        out_specs=[pl.BlockSpec((t, D_MODEL), lambda i: (i, 0)),
                   pl.BlockSpec((SUBLANES, t), lambda i: (0, i)),
                   pl.BlockSpec((SUBLANES, t), lambda i: (0, i)),
                   pl.BlockSpec((N_EXPERTS, LANES), lambda i: (0, 0))],
        out_shape=[jax.ShapeDtypeStruct((N_TOK, D_MODEL), jnp.bfloat16),
                   jax.ShapeDtypeStruct((SUBLANES, N_TOK), jnp.int32),
                   jax.ShapeDtypeStruct((SUBLANES, N_TOK), jnp.float32),
                   jax.ShapeDtypeStruct((N_EXPERTS, LANES), jnp.float32)],
        scratch_shapes=[pltpu.VMEM((t, D_MODEL), jnp.bfloat16),
                        pltpu.VMEM((N_EXPERTS, LANES), jnp.float32)],
        compiler_params=_params("arbitrary"),
        name="norm_router",
    )(x, gain, scale, shift, wr_hi_t, wr_lo_t, router_bias)


def _expert_kernel(be_ref, na_ref, x_ref, w1_ref, w3_ref, w2_ref, o_ref):
    active = pl.program_id(0) < na_ref[0]

    @pl.when(active)
    def _():
        bf16 = jnp.bfloat16
        x = x_ref[...]
        a = jnp.dot(x, w1_ref[...].astype(bf16), preferred_element_type=jnp.float32)
        b = jnp.dot(x, w3_ref[...].astype(bf16), preferred_element_type=jnp.float32)
        hdn = (a * jax.nn.sigmoid(a)) * b
        o_ref[...] = jnp.dot(hdn.astype(bf16), w2_ref[...].astype(bf16),
                             preferred_element_type=jnp.float32).astype(o_ref.dtype)

    @pl.when(jnp.logical_not(active))
    def _():
        o_ref[...] = jnp.zeros_like(o_ref)


def _expert_blocks(buf, block_e, n_active, w1, w3, w2):
    n_blocks = buf.shape[0] // MOE_BLOCK
    w_spec = lambda r, c: pl.BlockSpec((None, r, c), lambda i, be, na: (be[i], 0, 0), pipeline_mode=pl.Buffered(1))
    grid_spec = pltpu.PrefetchScalarGridSpec(
        num_scalar_prefetch=2,
        grid=(n_blocks,),
        in_specs=[pl.BlockSpec((MOE_BLOCK, D_MODEL), lambda i, be, na: (i, 0)),
                  w_spec(D_MODEL, D_EXPERT), w_spec(D_MODEL, D_EXPERT), w_spec(D_EXPERT, D_MODEL)],
        out_specs=pl.BlockSpec((MOE_BLOCK, D_MODEL), lambda i, be, na: (i, 0)),
    )
    return pl.pallas_call(
        _expert_kernel,
        grid_spec=grid_spec,
        out_shape=jax.ShapeDtypeStruct((n_blocks * MOE_BLOCK, D_MODEL), jnp.float32),
        compiler_params=_params("arbitrary"),
        name="moe_experts",
    )(block_e, n_active, buf, w1, w3, w2)


def _row_copy(out_hbm, buf_ref, sem_ref, slot, k, t, row):
    return pltpu.make_async_copy(out_hbm.at[pl.ds(row, 1), :], buf_ref.at[slot, k, pl.ds(t, 1), :], sem_ref.at[slot])


def _combine_kernel(dcur_ref, dnext_ref, w_ref, x_ref, gate_ref, out_hbm, o_ref, buf_ref, sem_ref):
    t_rows = x_ref.shape[0]
    i = pl.program_id(0)
    n = pl.num_programs(0)
    slot = i % 2

    def issue(d_ref, s):
        def body(t, carry):
            for k in range(TOP_K):
                _row_copy(out_hbm, buf_ref, sem_ref, s, k, t, d_ref[k, t]).start()
            return carry
        lax.fori_loop(0, t_rows, body, 0, unroll=COMBINE_UNROLL)

    @pl.when(i == 0)
    def _():
        issue(dcur_ref, 0)

    @pl.when(i + 1 < n)
    def _():
        issue(dnext_ref, 1 - slot)

    def wait(t, carry):
        for k in range(TOP_K):
            _row_copy(out_hbm, buf_ref, sem_ref, slot, k, t, 0).wait()
        return carry
    lax.fori_loop(0, t_rows, wait, 0, unroll=COMBINE_UNROLL)

    w = w_ref[...]
    moe = w[:, 0:1] * buf_ref[slot, 0] + w[:, 1:2] * buf_ref[slot, 1]
    o_ref[...] = x_ref[...] + gate_ref[...] * moe


def _moe_combine(x, gate, out, dest, wts_t):
    t = COMBINE_ROWS
    steps = N_TOK // t
    per = ROW_TILE // t
    smem_spec = lambda shift: pl.BlockSpec((SUBLANES, t), lambda i: (0, jnp.minimum(i + shift, steps - 1)),
                                           memory_space=pltpu.SMEM)
    return pl.pallas_call(
        _combine_kernel,
        grid=(steps,),
        in_specs=[smem_spec(0), smem_spec(1),
                  pl.BlockSpec((t, LANES), lambda i: (i, 0)),
                  pl.BlockSpec((t, D_MODEL), lambda i: (i, 0)),
                  pl.BlockSpec((None, 1, D_MODEL), lambda i: (_mod_row(i // per), 0, 0)),
                  pl.BlockSpec(memory_space=pl.ANY)],
        out_specs=pl.BlockSpec((t, D_MODEL), lambda i: (i, 0)),
        out_shape=jax.ShapeDtypeStruct((N_TOK, D_MODEL), jnp.float32),
        scratch_shapes=[pltpu.VMEM((2, TOP_K, t, D_MODEL), jnp.float32),
                        pltpu.SemaphoreType.DMA((2,))],
        compiler_params=_params("arbitrary"),
        name="moe_combine",
    )(dest, dest, wts_t, x, gate, out)


def _moe(x, gate, h, ids, wts, counts, layer, w1, w3, w2):
    nk = N_TOK * TOP_K
    n_blocks = (nk + N_EXPERTS * (MOE_BLOCK - 1)) // MOE_BLOCK
    counts = counts[:, 0].astype(jnp.int32)
    padded = (counts + MOE_BLOCK - 1) // MOE_BLOCK * MOE_BLOCK
    pad_end = jnp.cumsum(padded)
    pad_start = pad_end - padded
    dest1 = pad_start[ids[0]] + ids[2]
    dest2 = pad_start[ids[1]] + ids[3]
    tok = jnp.arange(N_TOK, dtype=jnp.int32)
    src = jnp.zeros((n_blocks * MOE_BLOCK,), jnp.int32).at[jnp.concatenate([dest1, dest2])].set(
        jnp.concatenate([tok, tok]))
    block_e = jnp.minimum(jnp.searchsorted(pad_end, jnp.arange(n_blocks, dtype=jnp.int32) * MOE_BLOCK, side='right'),
                          N_EXPERTS - 1).astype(jnp.int32)
    n_active = (pad_end[-1:] // MOE_BLOCK).astype(jnp.int32)
    out = _expert_blocks(h[src], block_e + layer * N_EXPERTS, n_active, w1, w3, w2)
    dest = jnp.concatenate([dest1[None, :], dest2[None, :], jnp.zeros((SUBLANES - TOP_K, N_TOK), jnp.int32)], axis=0)
    wts_t = jnp.pad(wts[:TOP_K].T, ((0, 0), (0, LANES - TOP_K)))
    return _moe_combine(x, gate, out, dest, wts_t)


def _final_norm_kernel(x_ref, g_ref, o_ref):
    g = g_ref[...]

    def body(r, carry):
        rows = pl.ds(pl.multiple_of(r * NORM_ROWS, NORM_ROWS), NORM_ROWS)
        x = x_ref[rows, :]
        o_ref[rows, :] = (x * lax.rsqrt(jnp.mean(x * x, axis=-1, keepdims=True) + NORM_EPS)) * g
        return carry

    lax.fori_loop(0, x_ref.shape[0] // NORM_ROWS, body, 0)


def _final_norm(x, gain):
    return pl.pallas_call(
        _final_norm_kernel,
        grid=(N_TOK // ROW_TILE,),
        in_specs=[pl.BlockSpec((ROW_TILE, D_MODEL), lambda i: (i, 0)),
                  pl.BlockSpec((1, D_MODEL), lambda i: (0, 0))],
        out_specs=pl.BlockSpec((ROW_TILE, D_MODEL), lambda i: (i, 0)),
        out_shape=jax.ShapeDtypeStruct((N_TOK, D_MODEL), jnp.float32),
        compiler_params=_params("arbitrary"),
        name="final_norm",
    )(x, gain)


def _permute_w_in(w):
    c0 = SSD_DIM + SSD_CONV_DIM
    c1 = c0 + 2 * SSD_HEADS
    dt_cols = jnp.pad(w[:, c0:c1], ((0, 0), (0, DT_PAD - 2 * SSD_HEADS)))
    return jnp.concatenate([w[:, :c0], w[:, c1:], dt_cols], axis=1).astype(jnp.bfloat16)


def kernel(x_prompt, x_sample, cache_k, cache_v, state_ssd, state_hgrn, c, c_ctx, w_ada, b_ada, norm_mix, norm_moe, w_in, ssd_conv_w, ssd_conv_b, ssd_dt_bias, ssd_a_log, ssd_d, ssd_norm, hg_lb_logits, hg_norm, att_rpb, w_br_ssd, w_br_hg, w_br_att, w_out, w_router, router_bias, moe_w1, moe_w3, moe_w2, final_norm):
    bf16, f32 = jnp.bfloat16, jnp.float32
    lb_cum = jnp.cumsum(jax.nn.softmax(hg_lb_logits.astype(f32), axis=1), axis=1)
    lower_bounds = lb_cum - lb_cum[:, :1]

    x = jnp.concatenate([x_prompt.reshape(N_CTX, D_MODEL), x_sample.reshape(N_LAT, D_MODEL)], axis=0)

    n_mod = 1 + DEC_BATCH
    cond = jnp.concatenate([c_ctx[None, :], c], axis=0)
    cond = jnp.pad(jax.nn.silu(cond), ((0, 2 * SUBLANES - n_mod), (0, 0)))

    wr_t = w_router.T
    wr_hi = wr_t.astype(bf16)
    wr_lo = (wr_t - wr_hi.astype(f32)).astype(bf16)
    cache_k = cache_k.reshape(DEC_BATCH, DEPTH, -1, ATT_DIM)
    cache_v = cache_v.reshape(DEC_BATCH, DEPTH, -1, ATT_DIM)
    lat_row0 = N_CTX // DEC_SEQ
    zero_ssd = jnp.zeros((BATCH, 2, SSD_HEADS, SSD_HEAD_DIM, SSD_STATE), f32)
    zero_hg = jnp.zeros((BATCH, 2, HG_HEADS, HG_KDIM, HG_VDIM), f32)

    expert_w = tuple(w.reshape((DEPTH * N_EXPERTS,) + w.shape[2:]) for w in (moe_w1, moe_w3, moe_w2))

    new_k, new_v, new_ssd, new_hg = [], [], [], []
    for l in range(DEPTH):
        mod = _matmul(cond, w_ada[l].astype(bf16), 2 * SUBLANES, COL_TILE)[:n_mod] + b_ada[l]
        mod = mod.reshape(n_mod, 6, 1, D_MODEL)
        shift_m, scale_m, gate_m, shift_f, scale_f, gate_f = (mod[:, i] for i in range(6))

        proj = _norm_matmul(x, norm_mix[l][None, :], scale_m, shift_m, _permute_w_in(w_in[l]))
        new_k.append(proj[:N_CTX, OFF_AK:OFF_AK + ATT_DIM].reshape(BATCH, SEQ, ATT_HEADS, ATT_HEAD_DIM))
        new_v.append(proj[:N_CTX, OFF_AV:OFF_AV + ATT_DIM].reshape(BATCH, SEQ, ATT_HEADS, ATT_HEAD_DIM))

        conv_b = ssd_conv_b[l][None, :]
        a_neg = -jnp.exp(ssd_a_log[l].astype(f32))
        d_row = jnp.repeat(ssd_d[l], SSD_HEAD_DIM)[None, :]
        ssd_gain = ssd_norm[l][None, :]
        y_ssd = []
        for row0, nb, length, init in ((0, BATCH, SEQ, zero_ssd), (N_CTX, DEC_BATCH, DEC_SEQ, state_ssd[:, l])):
            xbc = _ssd_conv(proj, row0, nb * length, length, ssd_conv_w[l], conv_b)
            y_f, y_b, states = _ssd_scan(xbc, proj, row0, nb, length, ssd_dt_bias[l], a_neg, init)
            y_ssd.append(_ssd_finish(y_f, y_b, xbc, proj, row0, d_row, ssd_gain))
            if row0 == 0:
                new_ssd.append(states)

        lb = lower_bounds[:, l]
        hg_gain = hg_norm[l].reshape(1, HG_DIM)
        y_hg_ctx, states = _hgrn_mixer(proj, 0, BATCH, SEQ, lb, hg_gain, zero_hg)
        new_hg.append(states)
        y_hg_lat, _ = _hgrn_mixer(proj, lat_row0, DEC_BATCH, DEC_SEQ, lb, hg_gain, state_hgrn[:, l])

        y_att_ctx = _context_attention(proj, BATCH, SEQ)
        y_att_lat = _neighbourhood_attention(proj, lat_row0, DEC_BATCH, DEC_SEQ, cache_k, cache_v, l,
                                             _window_bias(att_rpb[l]))

        merged = _branch_merge((y_ssd[0], y_hg_ctx, y_att_ctx), (y_ssd[1], y_hg_lat, y_att_lat),
                               (w_br_ssd[l].astype(bf16), w_br_hg[l].astype(bf16), w_br_att[l].astype(bf16)), proj)
        x = _out_residual(merged, w_out[l].astype(bf16), x, gate_m)

        h2, ids, wts, counts = _norm_router(x, norm_moe[l][None, :], scale_f, shift_f, wr_hi, wr_lo,
                                            router_bias.astype(f32)[:, None])
        x = _moe(x, gate_f, h2, ids, wts, counts, l, *expert_w)

    y = _final_norm(x, final_norm[None, :])
    y_prompt = y[:N_CTX].reshape(BATCH, SEQ, D_MODEL)
    y_sample = y[N_CTX:].reshape(DEC_BATCH, DEC_SEQ, D_MODEL)
    return (y_prompt, y_sample, jnp.stack(new_k, axis=1), jnp.stack(new_v, axis=1),
            jnp.stack(new_ssd, axis=1), jnp.stack(new_hg, axis=1))
```

```python
import functools

import jax
import jax.numpy as jnp
from jax import lax
from jax.experimental import pallas as pl
from jax.experimental.pallas import tpu as pltpu

D_MODEL = 2048
BATCH = 32
SEQ = 256
DEPTH = 2
DEC_BATCH = 8
DEC_SEQ = 4096
GRID_W = 64
NORM_EPS = 1e-6
SSD_HEADS = 16
SSD_HEAD_DIM = 64
SSD_DIM = SSD_HEADS * SSD_HEAD_DIM
SSD_STATE = 64
SSD_GROUPS = 4
SSD_CONV = 5
SSD_CONV_DIM = SSD_DIM + 2 * SSD_GROUPS * SSD_STATE
HG_HEADS = 8
HG_KDIM = 128
HG_VDIM = 128
HG_FDIM = HG_HEADS * HG_KDIM
HG_DIM = HG_HEADS * HG_VDIM
ATT_HEADS = 8
ATT_HEAD_DIM = 128
ATT_DIM = ATT_HEADS * ATT_HEAD_DIM
WIN_ROWS = 8
WIN_COLS = 16
N_BRANCH = 3
N_EXPERTS = 16
N_EXPERT_GROUPS = 4
EXPERTS_PER_GROUP = N_EXPERTS // N_EXPERT_GROUPS
TOP_K = 2
D_EXPERT = 1024

N_CTX = BATCH * SEQ
N_LAT = DEC_BATCH * DEC_SEQ
N_TOK = N_CTX + N_LAT

VMEM_LIMIT_BYTES = 56 * 1024 * 1024
LANES = 128
SUBLANES = 8

OFF_Z = 0
OFF_XBC = OFF_Z + SSD_DIM
OFF_HQ = OFF_XBC + SSD_CONV_DIM
OFF_HF = OFF_HQ + HG_FDIM
OFF_HI = OFF_HF + 2 * HG_FDIM
OFF_HG = OFF_HI + HG_DIM
OFF_AQ = OFF_HG + HG_DIM
OFF_AK = OFF_AQ + ATT_DIM
OFF_AV = OFF_AK + ATT_DIM
OFF_GATES = OFF_AV + ATT_DIM
PROJ_DIM = OFF_GATES + N_BRANCH * D_MODEL
PROJ_TILE = 1536

ROW_TILE = 1024
COL_TILE = 1024
MERGE_COLS = 512
NORM_ROWS = 64
ROUTER_TILE = 512
MOE_BLOCK = 512
CONV_ROWS = 1024
CONV_SUB = 256
CONV_COLS = 512
SSD_C = 128
FINISH_ROWS = 256
HG_C = 128
HG_LEVELS = (16, 32, 64, 128)
CAST_ROWS = 512
COMBINE_ROWS = 256
COMBINE_UNROLL = 8
DIAG_BATCH = 4
ROW_UNROLL = 4
HG_CHUNKS_PER_STEP = 2
MASKED = -1e30

_NT = (((1,), (1,)), ((), ()))
_TN = (((0,), (0,)), ((), ()))


def _params(*semantics):
    return pltpu.CompilerParams(dimension_semantics=semantics, vmem_limit_bytes=VMEM_LIMIT_BYTES)


def _mod_row(i):
    ctx_tiles = N_CTX // ROW_TILE
    tiles_per_req = DEC_SEQ // ROW_TILE
    return jnp.where(i < ctx_tiles, 0, 1 + (i - ctx_tiles) // tiles_per_req)


def _split3(x):
    bf16, f32 = jnp.bfloat16, jnp.float32
    x1 = x.astype(bf16)
    r = x - x1.astype(f32)
    x2 = r.astype(bf16)
    x3 = (r - x2.astype(f32)).astype(bf16)
    return x1, x2, x3


def _tri_cumsum(tri, x):
    x1, x2, x3 = _split3(x)
    f32 = jnp.float32
    return (jnp.dot(tri, x1, preferred_element_type=f32)
            + (jnp.dot(tri, x2, preferred_element_type=f32) + jnp.dot(tri, x3, preferred_element_type=f32)))


def _mm_kernel(x_ref, w_ref, o_ref):
    o_ref[...] = jnp.dot(x_ref[...].astype(jnp.bfloat16), w_ref[...],
                         preferred_element_type=jnp.float32).astype(o_ref.dtype)


def _matmul(x, w, tm, tn):
    m, k = x.shape
    n = w.shape[1]
    return pl.pallas_call(
        _mm_kernel,
        grid=(m // tm, n // tn),
        in_specs=[pl.BlockSpec((tm, k), lambda i, j: (i, 0)),
                  pl.BlockSpec((k, tn), lambda i, j: (0, j))],
        out_specs=pl.BlockSpec((tm, tn), lambda i, j: (i, j)),
        out_shape=jax.ShapeDtypeStruct((m, n), jnp.float32),
        compiler_params=_params("arbitrary", "arbitrary"),
        name="matmul",
    )(x, w)


def _modulated_norm(x_ref, g_ref, sc_ref, sh_ref, store):
    g = g_ref[...]
    sc = 1.0 + sc_ref[...]
    sh = sh_ref[...]

    def body(r, carry):
        rows = pl.ds(pl.multiple_of(r * NORM_ROWS, NORM_ROWS), NORM_ROWS)
        x = x_ref[rows, :]
        y = x * lax.rsqrt(jnp.mean(x * x, axis=-1, keepdims=True) + NORM_EPS)
        store(rows, (y * g) * sc + sh)
        return carry

    lax.fori_loop(0, x_ref.shape[0] // NORM_ROWS, body, 0)


def _norm_mm_kernel(x_ref, g_ref, sc_ref, sh_ref, w_ref, wdt_ref, o_ref, dt_ref, h_ref):
    @pl.when(pl.program_id(1) == 0)
    def _():
        def store(rows, h):
            h_ref[rows, :] = h.astype(jnp.bfloat16)
        _modulated_norm(x_ref, g_ref, sc_ref, sh_ref, store)
        dt_ref[...] = jnp.dot(h_ref[...], wdt_ref[...], preferred_element_type=jnp.float32)

    o_ref[...] = jnp.dot(h_ref[...], w_ref[...], preferred_element_type=jnp.float32)


def _norm_matmul(x, gain, scale, shift, w, w_dt):
    mod_spec = pl.BlockSpec((None, 1, D_MODEL), lambda i, j: (_mod_row(i), 0, 0))
    return pl.pallas_call(
        _norm_mm_kernel,
        grid=(N_TOK // ROW_TILE, PROJ_DIM // PROJ_TILE),
        in_specs=[pl.BlockSpec((ROW_TILE, D_MODEL), lambda i, j: (i, 0)),
                  pl.BlockSpec((1, D_MODEL), lambda i, j: (0, 0)),
                  mod_spec, mod_spec,
                  pl.BlockSpec((D_MODEL, PROJ_TILE), lambda i, j: (0, j)),
                  pl.BlockSpec((D_MODEL, LANES), lambda i, j: (0, 0))],
        out_specs=[pl.BlockSpec((ROW_TILE, PROJ_TILE), lambda i, j: (i, j)),
                   pl.BlockSpec((ROW_TILE, LANES), lambda i, j: (i, 0))],
        out_shape=[jax.ShapeDtypeStruct((N_TOK, PROJ_DIM), jnp.float32),
                   jax.ShapeDtypeStruct((N_TOK, LANES), jnp.float32)],
        scratch_shapes=[pltpu.VMEM((ROW_TILE, D_MODEL), jnp.bfloat16)],
        compiler_params=_params("arbitrary", "arbitrary"),
        name="norm_in_proj",
    )(x, gain, scale, shift, w, w_dt)


def _conv_kernel(prev_ref, x_ref, next_ref, w_ref, b_ref, o_ref, ext_ref, *, tiles_per_seq):
    i = pl.program_id(0)
    t = x_ref.shape[0]
    pad = SSD_CONV // 2
    first = (i % tiles_per_seq) == 0
    last = (i % tiles_per_seq) == tiles_per_seq - 1
    ext_ref[0:SUBLANES, :] = jnp.where(first, 0.0, prev_ref[...])
    ext_ref[SUBLANES:SUBLANES + t, :] = x_ref[...]
    ext_ref[SUBLANES + t:2 * SUBLANES + t, :] = jnp.where(last, 0.0, next_ref[...])
    for r0 in range(0, t, CONV_SUB):
        y = jnp.broadcast_to(b_ref[...], (CONV_SUB, CONV_COLS))
        for j in range(SSD_CONV):
            start = SUBLANES - pad + j + r0
            y = y + ext_ref[start:start + CONV_SUB, :] * w_ref[j:j + 1, :]
        o_ref[r0:r0 + CONV_SUB, :] = y * jax.nn.sigmoid(y)


def _ssd_conv(proj, row0, n_rows, seq_len, w, b):
    t = min(seq_len, CONV_ROWS)
    r0 = row0 // t
    c0 = OFF_XBC // CONV_COLS
    sub = t // SUBLANES
    n_sub = proj.shape[0] // SUBLANES
    return pl.pallas_call(
        functools.partial(_conv_kernel, tiles_per_seq=seq_len // t),
        grid=(n_rows // t, SSD_CONV_DIM // CONV_COLS),
        in_specs=[pl.BlockSpec((SUBLANES, CONV_COLS), lambda i, j: (jnp.maximum((r0 + i) * sub - 1, 0), c0 + j)),
                  pl.BlockSpec((t, CONV_COLS), lambda i, j: (r0 + i, c0 + j)),
                  pl.BlockSpec((SUBLANES, CONV_COLS),
                               lambda i, j: (jnp.minimum((r0 + i + 1) * sub, n_sub - 1), c0 + j)),
                  pl.BlockSpec((SSD_CONV, CONV_COLS), lambda i, j: (0, j)),
                  pl.BlockSpec((1, CONV_COLS), lambda i, j: (0, j))],
        out_specs=pl.BlockSpec((t, CONV_COLS), lambda i, j: (i, j)),
        out_shape=jax.ShapeDtypeStruct((n_rows, SSD_CONV_DIM), jnp.float32),
        scratch_shapes=[pltpu.VMEM((t + 2 * SUBLANES, CONV_COLS), jnp.float32)],
        compiler_params=_params("arbitrary", "arbitrary"),
        name="ssd_conv",
    )(proj, proj, proj, w, b)


def _softplus(x):
    return jnp.maximum(x, 0.0) + jnp.log1p(jnp.exp(-jnp.abs(x)))


def _expand(xs, sel, terms):
    parts = [jnp.concatenate(_split3(x)[:terms], axis=1) for x in xs]
    out = jnp.dot(jnp.concatenate(parts, axis=0), sel, preferred_element_type=jnp.float32)
    rows = xs[0].shape[0]
    return [out[i * rows:(i + 1) * rows] for i in range(len(xs))]


def _ssd_kernel(xf_ref, xb_ref, dtf_ref, dtb_ref, dtbias_ref, aneg_ref, init_ref, yf_ref, yb_ref, st_ref,
                s_ref, *, nc):
    bf16, f32 = jnp.bfloat16, jnp.float32
    c = SSD_C
    hd, ns = SSD_HEAD_DIM, SSD_STATE
    pair_w = 2 * hd
    j = pl.program_id(1)

    @pl.when(j == 0)
    def _():
        s_ref[...] = init_ref[...]

    t_ids = lax.broadcasted_iota(jnp.int32, (c, c), 0)
    s_ids = lax.broadcasted_iota(jnp.int32, (c, c), 1)
    eye = (lax.broadcasted_iota(jnp.int32, (SSD_HEADS, SSD_HEADS), 0)
           == lax.broadcasted_iota(jnp.int32, (SSD_HEADS, SSD_HEADS), 1)).astype(bf16)
    head_of = lambda terms, n, w: (lax.broadcasted_iota(jnp.int32, (terms * SSD_HEADS, n), 1) // w
                                   == lax.broadcasted_iota(jnp.int32, (terms * SSD_HEADS, n), 0) % SSD_HEADS
                                   ).astype(bf16)
    sel_x = head_of(2, SSD_DIM, hd)
    sel_c = head_of(3, SSD_HEADS * c, c)
    low_lanes = lax.broadcasted_iota(jnp.int32, (c, pair_w), 1) < hd
    low_rows = lax.broadcasted_iota(jnp.int32, (pair_w, ns), 0) < hd

    for d, (x_ref, dt_ref, y_ref) in enumerate(((xf_ref, dtf_ref, yf_ref), (xb_ref, dtb_ref, yb_ref))):
        rev = d == 1
        causal = (s_ids >= t_ids) if rev else (s_ids <= t_ids)
        tri = causal.astype(bf16)
        dt = _softplus(dt_ref[:, d * SSD_HEADS:(d + 1) * SSD_HEADS] + dtbias_ref[d:d + 1, :])
        acs = _tri_cumsum(tri, dt * aneg_ref[d:d + 1, :])
        acs_t = sum(lax.dot_general(eye, part, _NT, preferred_element_type=f32) for part in _split3(acs))
        end = acs[0:1, :] if rev else acs[c - 1:c, :]
        dt_x, out_x, in_x = _expand([dt, jnp.exp(end - acs), jnp.exp(acs)], sel_x, 2)
        acs_c, = _expand([acs], sel_c, 3)
        end_decay = jnp.exp(end)
        for g in range(SSD_GROUPS):
            bg = x_ref[:, SSD_DIM + g * ns:SSD_DIM + (g + 1) * ns].astype(bf16)
            cg = x_ref[:, SSD_DIM + (SSD_GROUPS + g) * ns:SSD_DIM + (SSD_GROUPS + g + 1) * ns].astype(bf16)
            cb = lax.dot_general(cg, bg, _NT, preferred_element_type=f32)
            for p in range(g * 2, g * 2 + 2):
                lanes = slice(p * pair_w, (p + 1) * pair_w)
                scores = []
                for h in (2 * p, 2 * p + 1):
                    decay = jnp.exp(jnp.minimum(acs_c[:, h * c:(h + 1) * c] - acs_t[h:h + 1, :], 0.0))
                    scores.append(jnp.where(causal, cb * decay, 0.0).astype(bf16))
                xdt = x_ref[:, lanes] * dt_x[:, lanes]
                rhs = jnp.concatenate([jnp.where(low_lanes, xdt, 0.0), jnp.where(low_lanes, 0.0, xdt)],
                                      axis=0).astype(bf16)
                s_p = s_ref[d, p]
                y = (jnp.dot(jnp.concatenate(scores, axis=1), rhs, preferred_element_type=f32)
                     + lax.dot_general(cg, s_p.astype(bf16), _NT, preferred_element_type=f32) * in_x[:, lanes])
                y_ref[:, lanes] = y
                keep = jnp.where(low_rows, end_decay[:, 2 * p:2 * p + 1], end_decay[:, 2 * p + 1:2 * p + 2])
                s_ref[d, p] = s_p * keep + lax.dot_general((xdt * out_x[:, lanes]).astype(bf16), bg, _TN,
                                                           preferred_element_type=f32)

    @pl.when(j == nc - 1)
    def _():
        st_ref[...] = s_ref[...]


def _ssd_scan(xbc, dt_logits, row0, nb, length, dt_bias, a_neg, init):
    c = SSD_C
    nc = length // c
    r0 = row0 // c
    pair_state = (nb, 2, SSD_HEADS // 2, 2 * SSD_HEAD_DIM, SSD_STATE)
    state_spec = pl.BlockSpec((None,) + pair_state[1:], lambda b, j: (b, 0, 0, 0, 0))
    y_f, y_b, states = pl.pallas_call(
        functools.partial(_ssd_kernel, nc=nc),
        grid=(nb, nc),
        in_specs=[pl.BlockSpec((c, SSD_CONV_DIM), lambda b, j: (b * nc + j, 0)),
                  pl.BlockSpec((c, SSD_CONV_DIM), lambda b, j: (b * nc + nc - 1 - j, 0)),
                  pl.BlockSpec((c, LANES), lambda b, j: (r0 + b * nc + j, 0)),
                  pl.BlockSpec((c, LANES), lambda b, j: (r0 + b * nc + nc - 1 - j, 0)),
                  pl.BlockSpec((2, SSD_HEADS), lambda b, j: (0, 0)),
                  pl.BlockSpec((2, SSD_HEADS), lambda b, j: (0, 0)),
                  state_spec],
        out_specs=[pl.BlockSpec((c, SSD_DIM), lambda b, j: (b * nc + j, 0)),
                   pl.BlockSpec((c, SSD_DIM), lambda b, j: (b * nc + nc - 1 - j, 0)),
                   state_spec],
        out_shape=[jax.ShapeDtypeStruct((nb * length, SSD_DIM), jnp.float32),
                   jax.ShapeDtypeStruct((nb * length, SSD_DIM), jnp.float32),
                   jax.ShapeDtypeStruct(pair_state, jnp.float32)],
        scratch_shapes=[pltpu.VMEM(pair_state[1:], jnp.float32)],
        compiler_params=_params("arbitrary", "arbitrary"),
        name="ssd_scan",
    )(xbc, xbc, dt_logits, dt_logits, dt_bias, a_neg, init.reshape(pair_state))
    return y_f, y_b, states.reshape(nb, 2, SSD_HEADS, SSD_HEAD_DIM, SSD_STATE)


def _ssd_finish_kernel(yf_ref, yb_ref, x_ref, z_ref, d_ref, g_ref, o_ref):
    z = z_ref[...]
    y = (yf_ref[...] + yb_ref[...] + d_ref[...] * x_ref[...]) * (z * jax.nn.sigmoid(z))
    o_ref[...] = (y * lax.rsqrt(jnp.mean(y * y, axis=-1, keepdims=True) + NORM_EPS) * g_ref[...]).astype(o_ref.dtype)


def _ssd_finish(y_f, y_b, xbc, proj, row0, d_row, gain):
    n = y_f.shape[0]
    t = FINISH_ROWS
    r0 = row0 // t
    row_spec = pl.BlockSpec((t, SSD_DIM), lambda i: (i, 0))
    vec_spec = pl.BlockSpec((1, SSD_DIM), lambda i: (0, 0))
    return pl.pallas_call(
        _ssd_finish_kernel,
        grid=(n // t,),
        in_specs=[row_spec, row_spec, row_spec,
                  pl.BlockSpec((t, SSD_DIM), lambda i: (r0 + i, OFF_Z // SSD_DIM)),
                  vec_spec, vec_spec],
        out_specs=row_spec,
        out_shape=jax.ShapeDtypeStruct((n, SSD_DIM), jnp.bfloat16),
        compiler_params=_params("arbitrary"),
        name="ssd_finish",
    )(y_f, y_b, xbc, proj, d_row, gain)


def _hgrn_chunk(q, g, kk, v, state, rev, tri, lane_mod, diag, level_masks, b_ref, kk_ref):
    bf16, f32 = jnp.bfloat16, jnp.float32
    c = q.shape[0]
    ng = c // SUBLANES
    b = _tri_cumsum(tri, g)
    yield
    b_ref[...] = b
    kk_ref[...] = kk
    row = lambda a_ref, r: a_ref[r:r + 1, :]
    grp = lambda a, i: a[i * SUBLANES:(i + 1) * SUBLANES, :]
    b_end = row(b_ref, 0) if rev else row(b_ref, c - 1)

    ones = jnp.ones((HG_KDIM, c), bf16)
    diag_rows = []
    for i0 in range(0, ng, DIAG_BATCH):
        tiles = []
        for i in range(i0, i0 + DIAG_BATCH):
            qg, bg = grp(q, i), grp(b, i)
            tiles += [qg * jnp.exp(bg - row(b_ref, i * SUBLANES + j)) * row(kk_ref, i * SUBLANES + j)
                      for j in range(SUBLANES)]
        sums = jnp.dot(jnp.concatenate(tiles, axis=0).astype(bf16), ones, preferred_element_type=f32)
        for n in range(DIAG_BATCH):
            base = n * SUBLANES * SUBLANES
            acc = sums[base:base + SUBLANES, :]
            for j in range(1, SUBLANES):
                acc = jnp.where(lane_mod[j], sums[base + j * SUBLANES:base + (j + 1) * SUBLANES, :], acc)
            diag_rows.append(acc)
        yield
    att = jnp.where(diag, jnp.concatenate(diag_rows, axis=0), 0.0)

    for m, mask in zip(HG_LEVELS, level_masks):
        half = m // 2
        q_side, k_side = [], []
        for i in range(ng):
            start = (i * SUBLANES) // m * m
            later = (i * SUBLANES) % m >= half
            ref = row(b_ref, start + half if rev else start + half - 1)
            if later != rev:
                q_side.append(grp(q, i) * jnp.exp(grp(b, i) - ref))
                k_side.append(jnp.zeros((SUBLANES, HG_KDIM), f32))
            else:
                q_side.append(jnp.zeros((SUBLANES, HG_KDIM), f32))
                k_side.append(grp(kk, i) * jnp.exp(ref - grp(b, i)))
        a_m = lax.dot_general(jnp.concatenate(q_side, axis=0).astype(bf16),
                              jnp.concatenate(k_side, axis=0).astype(bf16), _NT, preferred_element_type=f32)
        att = att + (a_m if m == c else jnp.where(mask, a_m, 0.0))
        yield

    q_in = (q * jnp.exp(b)).astype(bf16)
    k_out = (kk * jnp.exp(b_end - b)).astype(bf16)
    yield
    s_t = state[0]
    o = (lax.dot_general(q_in, s_t.astype(bf16), _NT, preferred_element_type=f32)
         + jnp.dot(att.astype(bf16), v.astype(bf16), preferred_element_type=f32))
    state[0] = s_t * jnp.exp(b_end) + lax.dot_general(v.astype(bf16), k_out, _TN, preferred_element_type=f32)
    return o


def _in_lockstep(gens):
    results = [None] * len(gens)
    live = list(range(len(gens)))
    while live:
        for i in list(live):
            try:
                next(gens[i])
            except StopIteration as stop:
                results[i] = stop.value
                live.remove(i)
    return results


def _hgrn_kernel(q_ref, ff_ref, fb_ref, v_ref, gate_ref, lb_ref, gn_ref, init_ref, o_ref, st_ref,
                 acc_ref, s_ref, b_ref, kk_ref, *, nc):
    c = HG_C
    t_ids = lax.broadcasted_iota(jnp.int32, (c, c), 0)
    s_ids = lax.broadcasted_iota(jnp.int32, (c, c), 1)
    causal = (s_ids <= t_ids, s_ids >= t_ids)
    tri = tuple(m.astype(jnp.bfloat16) for m in causal)
    same_group = (t_ids // SUBLANES) == (s_ids // SUBLANES)
    diag = tuple(m & same_group for m in causal)
    lane_mod = [(s_ids[:SUBLANES] % SUBLANES) == j for j in range(SUBLANES)]

    def level_mask(m, rev):
        same = (t_ids // m) == (s_ids // m)
        t_late = (t_ids % m) >= m // 2
        s_late = (s_ids % m) >= m // 2
        return same & (t_late != s_late) & (t_late != rev)
    masks = tuple([level_mask(m, rev) for m in HG_LEVELS] for rev in (False, True))

    s_ref[0] = init_ref[0].T
    s_ref[1] = init_ref[1].T
    gain = gn_ref[...]

    per_step = HG_CHUNKS_PER_STEP if (nc // 2) % HG_CHUNKS_PER_STEP == 0 else 1
    half_steps = nc // 2 // per_step

    def run_step(j):
        states = [[s_ref[0]], [s_ref[1]]]
        rows, gens = [], []
        for u in range(per_step):
            for d in (0, 1):
                chunk = j * per_step + u if d == 0 else nc - 1 - (j * per_step + u)
                r = pl.ds(pl.multiple_of(chunk * c, c), c)
                x = q_ref[r, :]
                q = x * jax.nn.sigmoid(x)
                lb = lb_ref[d:d + 1, :]
                f = lb + (1.0 - lb) * jax.nn.sigmoid((ff_ref, fb_ref)[d][r, :])
                rows.append(r)
                gens.append(_hgrn_chunk(q, jnp.log(f), 1.0 - f, v_ref[r, :], states[d], d == 1, tri[d], lane_mod,
                                        diag[d], masks[d], b_ref.at[2 * u + d], kk_ref.at[2 * u + d]))
        outs = _in_lockstep(gens)
        for d in (0, 1):
            s_ref[d] = states[d][0]
        return rows, outs

    def first_half(j, carry):
        rows, outs = run_step(j)
        for r, o in zip(rows, outs):
            acc_ref[r, :] = o
        return carry

    def second_half(j, carry):
        rows, outs = run_step(j)
        for r, o in zip(rows, outs):
            o = o + acc_ref[r, :]
            y = o * lax.rsqrt(jnp.mean(o * o, axis=-1, keepdims=True) + NORM_EPS) * gain
            gate = gate_ref[r, :]
            o_ref[r, :] = (y * (gate * jax.nn.sigmoid(gate))).astype(o_ref.dtype)
        return carry

    lax.fori_loop(0, half_steps, first_half, 0)
    lax.fori_loop(half_steps, 2 * half_steps, second_half, 0)
    st_ref[0] = s_ref[0].T
    st_ref[1] = s_ref[1].T


def _hgrn_mixer(proj, row0, nb, length, lb, gain, init):
    nc = length // HG_C
    assert nc % 2 == 0
    seq = lambda off: pl.BlockSpec((length, HG_KDIM), lambda b, h, col=off // HG_KDIM: (row0 + b, col + h))
    state_spec = pl.BlockSpec((None, 2, None, HG_KDIM, HG_VDIM), lambda b, h: (b, 0, h, 0, 0))
    return pl.pallas_call(
        functools.partial(_hgrn_kernel, nc=nc),
        grid=(nb, HG_HEADS),
        in_specs=[seq(OFF_HQ), seq(OFF_HF), seq(OFF_HF + HG_FDIM), seq(OFF_HI), seq(OFF_HG),
                  pl.BlockSpec((2, HG_KDIM), lambda b, h: (0, h)),
                  pl.BlockSpec((1, HG_VDIM), lambda b, h: (0, h)),
                  state_spec],
        out_specs=[pl.BlockSpec((length, HG_VDIM), lambda b, h: (b, h)), state_spec],
        out_shape=[jax.ShapeDtypeStruct((nb * length, HG_DIM), jnp.bfloat16),
                   jax.ShapeDtypeStruct((nb, 2, HG_HEADS, HG_KDIM, HG_VDIM), jnp.float32)],
        scratch_shapes=[pltpu.VMEM((length, HG_VDIM), jnp.float32),
                        pltpu.VMEM((2, HG_VDIM, HG_KDIM), jnp.float32),
                        pltpu.VMEM((2 * HG_CHUNKS_PER_STEP, HG_C, HG_KDIM), jnp.float32),
                        pltpu.VMEM((2 * HG_CHUNKS_PER_STEP, HG_C, HG_KDIM), jnp.float32)],
        compiler_params=_params("arbitrary", "arbitrary"),
        name="hgrn_mixer",
    )(proj, proj, proj, proj, proj, lb, gain, init)


def _softmax_av(scores, values):
    f32, bf16 = jnp.float32, jnp.bfloat16
    m = functools.reduce(jnp.maximum, [jnp.max(s, axis=-1, keepdims=True) for s in scores])
    ps = [jnp.exp(s - m) for s in scores]
    denom = functools.reduce(jnp.add, [jnp.sum(p, axis=-1, keepdims=True) for p in ps])
    acc = functools.reduce(jnp.add, [jnp.dot(p.astype(bf16), v, preferred_element_type=f32)
                                     for p, v in zip(ps, values)])
    return acc / denom


def _ctx_attn_kernel(q_ref, k_ref, v_ref, nk_in_ref, nv_in_ref, o_ref, nk_ref, nv_ref):
    bf16 = jnp.bfloat16
    scale = ATT_HEAD_DIM ** -0.5
    k = k_ref[...]
    v = v_ref[...]
    nk_ref[...] = k
    nv_ref[...] = v
    s = lax.dot_general(q_ref[...].astype(bf16), k.astype(bf16), _NT, preferred_element_type=jnp.float32) * scale
    o_ref[...] = _softmax_av([s], [v.astype(bf16)]).astype(o_ref.dtype)


def _context_attention(proj, nb, length, new_k, new_v, layer):
    spec = lambda off: pl.BlockSpec((length, ATT_HEAD_DIM), lambda b, h, col=off // ATT_HEAD_DIM: (b, col + h))
    cache_spec = pl.BlockSpec((None, None, length, ATT_HEAD_DIM), lambda b, h: (b, layer, 0, h))
    any_spec = pl.BlockSpec(memory_space=pl.ANY)
    return pl.pallas_call(
        _ctx_attn_kernel,
        grid=(nb, ATT_HEADS),
        in_specs=[spec(OFF_AQ), spec(OFF_AK), spec(OFF_AV), any_spec, any_spec],
        out_specs=[pl.BlockSpec((length, ATT_HEAD_DIM), lambda b, h: (b, h)), cache_spec, cache_spec],
        out_shape=[jax.ShapeDtypeStruct((nb * length, ATT_DIM), jnp.bfloat16),
                   jax.ShapeDtypeStruct(new_k.shape, new_k.dtype),
                   jax.ShapeDtypeStruct(new_v.shape, new_v.dtype)],
        input_output_aliases={3: 1, 4: 2},
        compiler_params=_params("arbitrary", "arbitrary"),
        name="context_attention",
    )(proj, proj, proj, new_k, new_v)


def _window_bias(rpb):
    col = jnp.arange(GRID_W)
    cs = jnp.clip(col - WIN_COLS // 2, 0, GRID_W - WIN_COLS)
    col_mask = (col[None, :] >= cs[:, None]) & (col[None, :] < cs[:, None] + WIN_COLS)
    dc_idx = jnp.clip(col[None, :] - col[:, None] + WIN_COLS - 1, 0, 2 * WIN_COLS - 2)
    bias = jnp.where(col_mask, rpb[:, :, dc_idx].astype(jnp.float32), MASKED)
    wins = [bias[:, d0:d0 + WIN_ROWS].transpose(0, 2, 1, 3).reshape(rpb.shape[0], GRID_W, WIN_ROWS * GRID_W)
            for d0 in range(WIN_ROWS)]
    return jnp.stack(wins, axis=1)


def _natten_kernel(q_ref, k_ref, v_ref, kc_ref, vc_ref, bias_ref, o_ref, kb_ref, vb_ref, *, rows):
    bf16, f32 = jnp.bfloat16, jnp.float32
    scale = ATT_HEAD_DIM ** -0.5
    win = WIN_ROWS * GRID_W

    def cast(i, carry):
        sl = pl.ds(pl.multiple_of(i * CAST_ROWS, CAST_ROWS), CAST_ROWS)
        kb_ref[sl, :] = k_ref[sl, :].astype(bf16)
        vb_ref[sl, :] = v_ref[sl, :].astype(bf16)
        return carry
    lax.fori_loop(0, rows * GRID_W // CAST_ROWS, cast, 0)

    kc = kc_ref[...].astype(bf16)
    vc = vc_ref[...].astype(bf16)

    def row_block(r):
        rs = jnp.clip(r - WIN_ROWS // 2, 0, rows - WIN_ROWS)
        d0 = rs - r + WIN_ROWS - 1
        q = q_ref[pl.ds(pl.multiple_of(r * GRID_W, GRID_W), GRID_W), :].astype(bf16)
        keys = pl.ds(pl.multiple_of(rs * GRID_W, GRID_W), win)
        s_lat = lax.dot_general(q, kb_ref[keys, :], _NT, preferred_element_type=f32) * scale + bias_ref[d0]
        s_ctx = lax.dot_general(q, kc, _NT, preferred_element_type=f32) * scale
        yield
        m = jnp.maximum(jnp.max(s_lat, axis=-1, keepdims=True), jnp.max(s_ctx, axis=-1, keepdims=True))
        yield
        p_lat = jnp.exp(s_lat - m)
        p_ctx = jnp.exp(s_ctx - m)
        denom = jnp.sum(p_lat, axis=-1, keepdims=True) + jnp.sum(p_ctx, axis=-1, keepdims=True)
        acc = (jnp.dot(p_lat.astype(bf16), vb_ref[keys, :], preferred_element_type=f32)
               + jnp.dot(p_ctx.astype(bf16), vc, preferred_element_type=f32))
        yield
        o_ref[pl.ds(pl.multiple_of(r * GRID_W, GRID_W), GRID_W), :] = (acc / denom).astype(o_ref.dtype)

    def row_group(g, carry):
        _in_lockstep([row_block(g * ROW_UNROLL + u) for u in range(ROW_UNROLL)])
        return carry
    lax.fori_loop(0, rows // ROW_UNROLL, row_group, 0)


def _neighbourhood_attention(proj, row0, nb, length, cache_k, cache_v, layer, bias_win):
    rows = length // GRID_W
    past = cache_k.shape[2]
    spec = lambda off: pl.BlockSpec((length, ATT_HEAD_DIM),
                                    lambda b, h, col=off // ATT_HEAD_DIM: (row0 + b, col + h))
    cache_spec = pl.BlockSpec((None, None, past, ATT_HEAD_DIM), lambda b, h: (b, layer, 0, h))
    return pl.pallas_call(
        functools.partial(_natten_kernel, rows=rows),
        grid=(nb, ATT_HEADS),
        in_specs=[spec(OFF_AQ), spec(OFF_AK), spec(OFF_AV), cache_spec, cache_spec,
                  pl.BlockSpec((None, WIN_ROWS, GRID_W, WIN_ROWS * GRID_W), lambda b, h: (h, 0, 0, 0))],
        out_specs=pl.BlockSpec((length, ATT_HEAD_DIM), lambda b, h: (b, h)),
        out_shape=jax.ShapeDtypeStruct((nb * length, ATT_DIM), jnp.bfloat16),
        scratch_shapes=[pltpu.VMEM((length, ATT_HEAD_DIM), jnp.bfloat16),
                        pltpu.VMEM((length, ATT_HEAD_DIM), jnp.bfloat16)],
        compiler_params=_params("arbitrary", "arbitrary"),
        name="neighbourhood_attention",
    )(proj, proj, proj, cache_k, cache_v, bias_win)


def _merge_kernel(ca_ref, cb_ref, cc_ref, la_ref, lb_ref, lc_ref, wa_ref, wb_ref, wc_ref,
                  ga_ref, gb_ref, gc_ref, o_ref):
    f32 = jnp.float32
    is_ctx = pl.program_id(0) < N_CTX // ROW_TILE

    def merge(ya_ref, yb_ref, yc_ref):
        acc = jax.nn.sigmoid(ga_ref[...]) * jnp.dot(ya_ref[...], wa_ref[...], preferred_element_type=f32)
        acc = acc + jax.nn.sigmoid(gb_ref[...]) * jnp.dot(yb_ref[...], wb_ref[...], preferred_element_type=f32)
        acc = acc + jax.nn.sigmoid(gc_ref[...]) * jnp.dot(yc_ref[...], wc_ref[...], preferred_element_type=f32)
        o_ref[...] = acc.astype(o_ref.dtype)

    @pl.when(is_ctx)
    def _():
        merge(ca_ref, cb_ref, cc_ref)

    @pl.when(jnp.logical_not(is_ctx))
    def _():
        merge(la_ref, lb_ref, lc_ref)


def _branch_merge(ys_ctx, ys_lat, ws, proj):
    tm, tn = ROW_TILE, MERGE_COLS
    kdim = ws[0].shape[0]
    cg = OFF_GATES // tn
    per = D_MODEL // tn
    ctx_tiles = N_CTX // tm
    ctx_spec = pl.BlockSpec((tm, kdim), lambda i, j: (jnp.minimum(i, ctx_tiles - 1), 0))
    lat_spec = pl.BlockSpec((tm, kdim), lambda i, j: (jnp.maximum(i - ctx_tiles, 0), 0))
    w_spec = pl.BlockSpec((kdim, tn), lambda i, j: (0, j))
    g_spec = lambda b: pl.BlockSpec((tm, tn), lambda i, j, b=b: (i, cg + b * per + j))
    return pl.pallas_call(
        _merge_kernel,
        grid=(N_TOK // tm, D_MODEL // tn),
        in_specs=[ctx_spec] * N_BRANCH + [lat_spec] * N_BRANCH + [w_spec] * N_BRANCH
                 + [g_spec(b) for b in range(N_BRANCH)],
        out_specs=pl.BlockSpec((tm, tn), lambda i, j: (i, j)),
        out_shape=jax.ShapeDtypeStruct((N_TOK, D_MODEL), jnp.bfloat16),
        compiler_params=_params("arbitrary", "arbitrary"),
        name="branch_merge",
    )(*ys_ctx, *ys_lat, *ws, proj, proj, proj)


def _out_residual_kernel(m_ref, w_ref, x_ref, gate_ref, o_ref):
    o_ref[...] = x_ref[...] + gate_ref[...] * jnp.dot(m_ref[...], w_ref[...], preferred_element_type=jnp.float32)


def _out_residual(merged, w, x, gate):
    tm, tn = ROW_TILE, COL_TILE
    return pl.pallas_call(
        _out_residual_kernel,
        grid=(N_TOK // tm, D_MODEL // tn),
        in_specs=[pl.BlockSpec((tm, D_MODEL), lambda i, j: (i, 0)),
                  pl.BlockSpec((D_MODEL, tn), lambda i, j: (0, j)),
                  pl.BlockSpec((tm, tn), lambda i, j: (i, j)),
                  pl.BlockSpec((None, 1, tn), lambda i, j: (_mod_row(i), 0, j))],
        out_specs=pl.BlockSpec((tm, tn), lambda i, j: (i, j)),
        out_shape=jax.ShapeDtypeStruct((N_TOK, D_MODEL), jnp.float32),
        compiler_params=_params("arbitrary", "arbitrary"),
        name="out_residual",
    )(merged, w, x, gate)


def _first_max(vals):
    best, idx = vals[0], jnp.zeros(vals[0].shape, jnp.int32)
    for k in range(1, len(vals)):
        better = vals[k] > best
        best = jnp.where(better, vals[k], best)
        idx = jnp.where(better, k, idx)
    return best, idx


def _pick(idx, vals):
    out = vals[0]
    for k in range(1, len(vals)):
        out = jnp.where(idx == k, vals[k], out)
    return out


def _norm_router_kernel(x_ref, g_ref, sc_ref, sh_ref, wr_hi_ref, wr_lo_ref, rb_ref,
                        h_ref, ids_ref, wts_ref, cnt_ref, lo_ref, base_ref):
    f32, bf16 = jnp.float32, jnp.bfloat16
    t = x_ref.shape[0]

    @pl.when(pl.program_id(0) == 0)
    def _():
        base_ref[...] = jnp.zeros_like(base_ref)

    def store(rows, h):
        h_hi = h.astype(bf16)
        h_ref[rows, :] = h_hi
        lo_ref[rows, :] = (h - h_hi.astype(f32)).astype(bf16)
    _modulated_norm(x_ref, g_ref, sc_ref, sh_ref, store)

    logits = (jnp.dot(h_ref[...], wr_hi_ref[...], preferred_element_type=f32)
              + (jnp.dot(h_ref[...], wr_lo_ref[...], preferred_element_type=f32)
                 + jnp.dot(lo_ref[...], wr_hi_ref[...], preferred_element_type=f32)))
    logits = logits.T[:N_EXPERTS]
    scores = jax.nn.sigmoid(logits)
    sel = scores + rb_ref[...]
    row = lambda a, e: a[e:e + 1, :]

    group_scores = []
    for g in range(N_EXPERT_GROUPS):
        v = [row(sel, g * EXPERTS_PER_GROUP + k) for k in range(EXPERTS_PER_GROUP)]
        pair_sums = [v[a] + v[b] for a in range(EXPERTS_PER_GROUP) for b in range(a + 1, EXPERTS_PER_GROUP)]
        group_scores.append(functools.reduce(jnp.maximum, pair_sums))
    _, grp = _first_max(group_scores)

    in_sel = [_pick(grp, [row(sel, g * EXPERTS_PER_GROUP + k) for g in range(N_EXPERT_GROUPS)])
              for k in range(EXPERTS_PER_GROUP)]
    in_score = [_pick(grp, [row(scores, g * EXPERTS_PER_GROUP + k) for g in range(N_EXPERT_GROUPS)])
                for k in range(EXPERTS_PER_GROUP)]
    _, i1 = _first_max(in_sel)
    _, i2 = _first_max([jnp.where(i1 == k, -jnp.inf, in_sel[k]) for k in range(EXPERTS_PER_GROUP)])
    s1 = _pick(i1, in_score)
    s2 = _pick(i2, in_score)
    e1 = grp * EXPERTS_PER_GROUP + i1
    e2 = grp * EXPERTS_PER_GROUP + i2

    e_ids = lax.broadcasted_iota(jnp.int32, (N_EXPERTS, t), 0)
    hit1 = e_ids == e1
    hit2 = e_ids == e2
    cnt = jnp.where(hit1 | hit2, 1.0, 0.0).astype(bf16)
    before = (lax.broadcasted_iota(jnp.int32, (t, t), 0) < lax.broadcasted_iota(jnp.int32, (t, t), 1)).astype(bf16)
    prefix = jnp.dot(cnt, before, preferred_element_type=f32) + base_ref[:, 0:1]
    rank1 = jnp.sum(jnp.where(hit1, prefix, 0.0), axis=0, keepdims=True)
    rank2 = jnp.sum(jnp.where(hit2, prefix, 0.0), axis=0, keepdims=True)
    base_ref[...] = base_ref[...] + jnp.dot(cnt, jnp.ones((t, LANES), bf16), preferred_element_type=f32)
    cnt_ref[...] = base_ref[...]

    zeros = jnp.zeros((SUBLANES - 4, t), jnp.int32)
    ids_ref[...] = jnp.concatenate([e1, e2, rank1.astype(jnp.int32), rank2.astype(jnp.int32), zeros], axis=0)
    total = s1 + s2
    wts_ref[...] = jnp.concatenate([s1 / total, s2 / total, jnp.zeros((SUBLANES - 2, t), f32)], axis=0)


def _norm_router(x, gain, scale, shift, wr_hi, wr_lo, router_bias):
    t = ROUTER_TILE
    per = ROW_TILE // t
    mod_spec = pl.BlockSpec((None, 1, D_MODEL), lambda i: (_mod_row(i // per), 0, 0))
    return pl.pallas_call(
        _norm_router_kernel,
        grid=(N_TOK // t,),
        in_specs=[pl.BlockSpec((t, D_MODEL), lambda i: (i, 0)),
                  pl.BlockSpec((1, D_MODEL), lambda i: (0, 0)),
                  mod_spec, mod_spec,
                  pl.BlockSpec((D_MODEL, LANES), lambda i: (0, 0)),
                  pl.BlockSpec((D_MODEL, LANES), lambda i: (0, 0)),
                  pl.BlockSpec((N_EXPERTS, 1), lambda i: (0, 0))],
        out_specs=[pl.BlockSpec((t, D_MODEL), lambda i: (i, 0)),
                   pl.BlockSpec((SUBLANES, t), lambda i: (0, i)),
                   pl.BlockSpec((SUBLANES, t), lambda i: (0, i)),
                   pl.BlockSpec((N_EXPERTS, LANES), lambda i: (0, 0))],
        out_shape=[jax.ShapeDtypeStruct((N_TOK, D_MODEL), jnp.bfloat16),
                   jax.ShapeDtypeStruct((SUBLANES, N_TOK), jnp.int32),
                   jax.ShapeDtypeStruct((SUBLANES, N_TOK), jnp.float32),
                   jax.ShapeDtypeStruct((N_EXPERTS, LANES), jnp.float32)],
        scratch_shapes=[pltpu.VMEM((t, D_MODEL), jnp.bfloat16),
                        pltpu.VMEM((N_EXPERTS, LANES), jnp.float32)],
        compiler_params=_params("arbitrary"),
        name="norm_router",
    )(x, gain, scale, shift, wr_hi, wr_lo, router_bias)


def _expert_kernel(be_ref, na_ref, x_ref, w1_ref, w3_ref, w2_ref, o_ref):
    active = pl.program_id(0) < na_ref[0]

    @pl.when(active)
    def _():
        bf16 = jnp.bfloat16
        x = x_ref[...]
        a = jnp.dot(x, w1_ref[...].astype(bf16), preferred_element_type=jnp.float32)
        b = jnp.dot(x, w3_ref[...].astype(bf16), preferred_element_type=jnp.float32)
        hdn = (a * jax.nn.sigmoid(a)) * b
        o_ref[...] = jnp.dot(hdn.astype(bf16), w2_ref[...].astype(bf16),
                             preferred_element_type=jnp.float32).astype(o_ref.dtype)

    @pl.when(jnp.logical_not(active))
    def _():
        o_ref[...] = jnp.zeros_like(o_ref)


def _expert_blocks(buf, block_e, n_active, w1, w3, w2):
    n_blocks = buf.shape[0] // MOE_BLOCK
    w_spec = lambda r, c: pl.BlockSpec((None, r, c), lambda i, be, na: (be[i], 0, 0), pipeline_mode=pl.Buffered(1))
    grid_spec = pltpu.PrefetchScalarGridSpec(
        num_scalar_prefetch=2,
        grid=(n_blocks,),
        in_specs=[pl.BlockSpec((MOE_BLOCK, D_MODEL), lambda i, be, na: (i, 0)),
                  w_spec(D_MODEL, D_EXPERT), w_spec(D_MODEL, D_EXPERT), w_spec(D_EXPERT, D_MODEL)],
        out_specs=pl.BlockSpec((MOE_BLOCK, D_MODEL), lambda i, be, na: (i, 0)),
    )
    return pl.pallas_call(
        _expert_kernel,
        grid_spec=grid_spec,
        out_shape=jax.ShapeDtypeStruct((n_blocks * MOE_BLOCK, D_MODEL), jnp.float32),
        compiler_params=_params("arbitrary"),
        name="moe_experts",
    )(block_e, n_active, buf, w1, w3, w2)


def _row_copy(out_hbm, buf_ref, sem_ref, slot, k, t, row):
    return pltpu.make_async_copy(out_hbm.at[pl.ds(row, 1), :], buf_ref.at[slot, k, pl.ds(t, 1), :], sem_ref.at[slot])


def _combine_kernel(dcur_ref, dnext_ref, w_ref, x_ref, gate_ref, out_hbm, o_ref, buf_ref, sem_ref):
    t_rows = x_ref.shape[0]
    i = pl.program_id(0)
    n = pl.num_programs(0)
    slot = i % 2

    def issue(d_ref, s):
        def body(t, carry):
            for k in range(TOP_K):
                _row_copy(out_hbm, buf_ref, sem_ref, s, k, t, d_ref[k, t]).start()
            return carry
        lax.fori_loop(0, t_rows, body, 0, unroll=COMBINE_UNROLL)

    @pl.when(i == 0)
    def _():
        issue(dcur_ref, 0)

    @pl.when(i + 1 < n)
    def _():
        issue(dnext_ref, 1 - slot)

    def wait(t, carry):
        for k in range(TOP_K):
            _row_copy(out_hbm, buf_ref, sem_ref, slot, k, t, 0).wait()
        return carry
    lax.fori_loop(0, t_rows, wait, 0, unroll=COMBINE_UNROLL)

    w = w_ref[...]
    moe = w[:, 0:1] * buf_ref[slot, 0] + w[:, 1:2] * buf_ref[slot, 1]
    o_ref[...] = x_ref[...] + gate_ref[...] * moe


def _moe_combine(x, gate, out, dest, wts_t):
    t = COMBINE_ROWS
    steps = N_TOK // t
    per = ROW_TILE // t
    smem_spec = lambda shift: pl.BlockSpec((SUBLANES, t), lambda i: (0, jnp.minimum(i + shift, steps - 1)),
                                           memory_space=pltpu.SMEM)
    return pl.pallas_call(
        _combine_kernel,
        grid=(steps,),
        in_specs=[smem_spec(0), smem_spec(1),
                  pl.BlockSpec((t, LANES), lambda i: (i, 0)),
                  pl.BlockSpec((t, D_MODEL), lambda i: (i, 0)),
                  pl.BlockSpec((None, 1, D_MODEL), lambda i: (_mod_row(i // per), 0, 0)),
                  pl.BlockSpec(memory_space=pl.ANY)],
        out_specs=pl.BlockSpec((t, D_MODEL), lambda i: (i, 0)),
        out_shape=jax.ShapeDtypeStruct((N_TOK, D_MODEL), jnp.float32),
        scratch_shapes=[pltpu.VMEM((2, TOP_K, t, D_MODEL), jnp.float32),
                        pltpu.SemaphoreType.DMA((2,))],
        compiler_params=_params("arbitrary"),
        name="moe_combine",
    )(dest, dest, wts_t, x, gate, out)


def _moe(x, gate, h, ids, wts, counts, layer, w1, w3, w2):
    nk = N_TOK * TOP_K
    n_blocks = (nk + N_EXPERTS * (MOE_BLOCK - 1)) // MOE_BLOCK
    counts = counts[:, 0].astype(jnp.int32)
    padded = (counts + MOE_BLOCK - 1) // MOE_BLOCK * MOE_BLOCK
    pad_end = jnp.cumsum(padded)
    pad_start = pad_end - padded
    dest1 = pad_start[ids[0]] + ids[2]
    dest2 = pad_start[ids[1]] + ids[3]
    tok = jnp.arange(N_TOK, dtype=jnp.int32)
    src = jnp.zeros((n_blocks * MOE_BLOCK,), jnp.int32).at[jnp.concatenate([dest1, dest2])].set(
        jnp.concatenate([tok, tok]))
    block_e = jnp.minimum(jnp.searchsorted(pad_end, jnp.arange(n_blocks, dtype=jnp.int32) * MOE_BLOCK, side='right'),
                          N_EXPERTS - 1).astype(jnp.int32)
    n_active = (pad_end[-1:] // MOE_BLOCK).astype(jnp.int32)
    out = _expert_blocks(h[src], block_e + layer * N_EXPERTS, n_active, w1, w3, w2)
    dest = jnp.concatenate([dest1[None, :], dest2[None, :], jnp.zeros((SUBLANES - TOP_K, N_TOK), jnp.int32)], axis=0)
    wts_t = jnp.pad(wts[:TOP_K].T, ((0, 0), (0, LANES - TOP_K)))
    return _moe_combine(x, gate, out, dest, wts_t)


def _final_norm_kernel(x_ref, g_ref, o_ref):
    g = g_ref[...]

    def body(r, carry):
        rows = pl.ds(pl.multiple_of(r * NORM_ROWS, NORM_ROWS), NORM_ROWS)
        x = x_ref[rows, :]
        o_ref[rows, :] = (x * lax.rsqrt(jnp.mean(x * x, axis=-1, keepdims=True) + NORM_EPS)) * g
        return carry

    lax.fori_loop(0, x_ref.shape[0] // NORM_ROWS, body, 0)


def _final_norm(x, gain):
    return pl.pallas_call(
        _final_norm_kernel,
        grid=(N_TOK // ROW_TILE,),
        in_specs=[pl.BlockSpec((ROW_TILE, D_MODEL), lambda i: (i, 0)),
                  pl.BlockSpec((1, D_MODEL), lambda i: (0, 0))],
        out_specs=pl.BlockSpec((ROW_TILE, D_MODEL), lambda i: (i, 0)),
        out_shape=jax.ShapeDtypeStruct((N_TOK, D_MODEL), jnp.float32),
        compiler_params=_params("arbitrary"),
        name="final_norm",
    )(x, gain)


def _split_w_in(w):
    c0 = SSD_DIM + SSD_CONV_DIM
    c1 = c0 + 2 * SSD_HEADS
    dt_cols = jnp.pad(w[:, c0:c1], ((0, 0), (0, LANES - 2 * SSD_HEADS)))
    return (jnp.concatenate([w[:, :c0], w[:, c1:]], axis=1).astype(jnp.bfloat16), dt_cols.astype(jnp.bfloat16))


def kernel(x_prompt, x_sample, cache_k, cache_v, state_ssd, state_hgrn, c, c_ctx, w_ada, b_ada, norm_mix, norm_moe, w_in, ssd_conv_w, ssd_conv_b, ssd_dt_bias, ssd_a_log, ssd_d, ssd_norm, hg_lb_logits, hg_norm, att_rpb, w_br_ssd, w_br_hg, w_br_att, w_out, w_router, router_bias, moe_w1, moe_w3, moe_w2, final_norm):
    bf16, f32 = jnp.bfloat16, jnp.float32
    lb_cum = jnp.cumsum(jax.nn.softmax(hg_lb_logits.astype(f32), axis=1), axis=1)
    lower_bounds = lb_cum - lb_cum[:, :1]

    x = jnp.concatenate([x_prompt.reshape(N_CTX, D_MODEL), x_sample.reshape(N_LAT, D_MODEL)], axis=0)

    n_mod = 1 + DEC_BATCH
    cond = jnp.concatenate([c_ctx[None, :], c], axis=0)
    cond = jnp.pad(jax.nn.silu(cond), ((0, 2 * SUBLANES - n_mod), (0, 0)))

    wr = jnp.pad(w_router, ((0, 0), (0, LANES - N_EXPERTS)))
    wr_hi = wr.astype(bf16)
    wr_lo = (wr - wr_hi.astype(f32)).astype(bf16)
    cache_k = cache_k.reshape(DEC_BATCH, DEPTH, -1, ATT_DIM)
    cache_v = cache_v.reshape(DEC_BATCH, DEPTH, -1, ATT_DIM)
    lat_row0 = N_CTX // DEC_SEQ
    zero_ssd = jnp.zeros((BATCH, 2, SSD_HEADS, SSD_HEAD_DIM, SSD_STATE), f32)
    zero_hg = jnp.zeros((BATCH, 2, HG_HEADS, HG_KDIM, HG_VDIM), f32)

    expert_w = tuple(w.reshape((DEPTH * N_EXPERTS,) + w.shape[2:]) for w in (moe_w1, moe_w3, moe_w2))

    new_k = jnp.zeros((BATCH, DEPTH, SEQ, ATT_DIM), f32)
    new_v = jnp.zeros((BATCH, DEPTH, SEQ, ATT_DIM), f32)
    new_ssd, new_hg = [], []
    for l in range(DEPTH):
        mod = _matmul(cond, w_ada[l].astype(bf16), 2 * SUBLANES, COL_TILE)[:n_mod] + b_ada[l]
        mod = mod.reshape(n_mod, 6, 1, D_MODEL)
        shift_m, scale_m, gate_m, shift_f, scale_f, gate_f = (mod[:, i] for i in range(6))

        proj, dt_logits = _norm_matmul(x, norm_mix[l][None, :], scale_m, shift_m, *_split_w_in(w_in[l]))

        conv_b = ssd_conv_b[l][None, :]
        a_neg = -jnp.exp(ssd_a_log[l].astype(f32))
        d_row = jnp.repeat(ssd_d[l], SSD_HEAD_DIM)[None, :]
        ssd_gain = ssd_norm[l][None, :]
        y_ssd = []
        for row0, nb, length, init in ((0, BATCH, SEQ, zero_ssd), (N_CTX, DEC_BATCH, DEC_SEQ, state_ssd[:, l])):
            xbc = _ssd_conv(proj, row0, nb * length, length, ssd_conv_w[l], conv_b)
            y_f, y_b, states = _ssd_scan(xbc, dt_logits, row0, nb, length, ssd_dt_bias[l], a_neg, init)
            y_ssd.append(_ssd_finish(y_f, y_b, xbc, proj, row0, d_row, ssd_gain))
            if row0 == 0:
                new_ssd.append(states)

        lb = lower_bounds[:, l]
        hg_gain = hg_norm[l].reshape(1, HG_DIM)
        y_hg_ctx, states = _hgrn_mixer(proj, 0, BATCH, SEQ, lb, hg_gain, zero_hg)
        new_hg.append(states)
        y_hg_lat, _ = _hgrn_mixer(proj, lat_row0, DEC_BATCH, DEC_SEQ, lb, hg_gain, state_hgrn[:, l])

        y_att_ctx, new_k, new_v = _context_attention(proj, BATCH, SEQ, new_k, new_v, l)
        y_att_lat = _neighbourhood_attention(proj, lat_row0, DEC_BATCH, DEC_SEQ, cache_k, cache_v, l,
                                             _window_bias(att_rpb[l]))

        merged = _branch_merge((y_ssd[0], y_hg_ctx, y_att_ctx), (y_ssd[1], y_hg_lat, y_att_lat),
                               (w_br_ssd[l].astype(bf16), w_br_hg[l].astype(bf16), w_br_att[l].astype(bf16)), proj)
        x = _out_residual(merged, w_out[l].astype(bf16), x, gate_m)

        h2, ids, wts, counts = _norm_router(x, norm_moe[l][None, :], scale_f, shift_f, wr_hi, wr_lo,
                                            router_bias.astype(f32)[:, None])
        x = _moe(x, gate_f, h2, ids, wts, counts, l, *expert_w)

    y = _final_norm(x, final_norm[None, :])
    y_prompt = y[:N_CTX].reshape(BATCH, SEQ, D_MODEL)
    y_sample = y[N_CTX:].reshape(DEC_BATCH, DEC_SEQ, D_MODEL)
    cache_shape = (BATCH, DEPTH, SEQ, ATT_HEADS, ATT_HEAD_DIM)
    return (y_prompt, y_sample, new_k.reshape(cache_shape), new_v.reshape(cache_shape),
            jnp.stack(new_ssd, axis=1), jnp.stack(new_hg, axis=1))
```

```python
import functools

import jax
import jax.numpy as jnp
from jax import lax
from jax.experimental import pallas as pl
from jax.experimental.pallas import tpu as pltpu

D_MODEL = 2048
BATCH = 32
SEQ = 256
DEPTH = 2
DEC_BATCH = 8
DEC_SEQ = 4096
GRID_W = 64
NORM_EPS = 1e-6
SSD_HEADS = 16
SSD_HEAD_DIM = 64
SSD_DIM = SSD_HEADS * SSD_HEAD_DIM
SSD_STATE = 64
SSD_GROUPS = 4
SSD_CONV = 5
SSD_CONV_DIM = SSD_DIM + 2 * SSD_GROUPS * SSD_STATE
HG_HEADS = 8
HG_KDIM = 128
HG_VDIM = 128
HG_FDIM = HG_HEADS * HG_KDIM
HG_DIM = HG_HEADS * HG_VDIM
ATT_HEADS = 8
ATT_HEAD_DIM = 128
ATT_DIM = ATT_HEADS * ATT_HEAD_DIM
WIN_ROWS = 8
WIN_COLS = 16
N_BRANCH = 3
N_EXPERTS = 16
N_EXPERT_GROUPS = 4
EXPERTS_PER_GROUP = N_EXPERTS // N_EXPERT_GROUPS
TOP_K = 2
D_EXPERT = 1024

N_CTX = BATCH * SEQ
N_LAT = DEC_BATCH * DEC_SEQ
N_TOK = N_CTX + N_LAT

VMEM_LIMIT_BYTES = 56 * 1024 * 1024
LANES = 128
SUBLANES = 8

OFF_Z = 0
OFF_XBC = OFF_Z + SSD_DIM
OFF_HQ = OFF_XBC + SSD_CONV_DIM
OFF_HF = OFF_HQ + HG_FDIM
OFF_HI = OFF_HF + 2 * HG_FDIM
OFF_HG = OFF_HI + HG_DIM
OFF_AQ = OFF_HG + HG_DIM
OFF_AK = OFF_AQ + ATT_DIM
OFF_AV = OFF_AK + ATT_DIM
OFF_GATES = OFF_AV + ATT_DIM
PROJ_DIM = OFF_GATES + N_BRANCH * D_MODEL
PROJ_TILE = 1536

ROW_TILE = 1024
COL_TILE = 1024
MERGE_COLS = 512
NORM_ROWS = 64
ROUTER_TILE = 512
MOE_BLOCK = 512
CONV_ROWS = 1024
CONV_SUB = 256
CONV_COLS = 512
SSD_C = 128
FINISH_ROWS = 256
HG_C = 128
HG_LEVELS = (16, 32, 64, 128)
CAST_ROWS = 512
COMBINE_ROWS = 256
COMBINE_UNROLL = 8
DIAG_BATCH = 4
ROW_UNROLL = 8
HG_CHUNKS_PER_STEP = 2
MASKED = -1e30

_NT = (((1,), (1,)), ((), ()))
_TN = (((0,), (0,)), ((), ()))


def _params(*semantics):
    return pltpu.CompilerParams(dimension_semantics=semantics, vmem_limit_bytes=VMEM_LIMIT_BYTES)


def _mod_row(i):
    ctx_tiles = N_CTX // ROW_TILE
    tiles_per_req = DEC_SEQ // ROW_TILE
    return jnp.where(i < ctx_tiles, 0, 1 + (i - ctx_tiles) // tiles_per_req)


def _split3(x):
    bf16, f32 = jnp.bfloat16, jnp.float32
    x1 = x.astype(bf16)
    r = x - x1.astype(f32)
    x2 = r.astype(bf16)
    x3 = (r - x2.astype(f32)).astype(bf16)
    return x1, x2, x3


def _tri_cumsum(tri, x):
    x1, x2, x3 = _split3(x)
    f32 = jnp.float32
    return (jnp.dot(tri, x1, preferred_element_type=f32)
            + (jnp.dot(tri, x2, preferred_element_type=f32) + jnp.dot(tri, x3, preferred_element_type=f32)))


def _mm_kernel(x_ref, w_ref, o_ref):
    o_ref[...] = jnp.dot(x_ref[...].astype(jnp.bfloat16), w_ref[...].astype(jnp.bfloat16),
                         preferred_element_type=jnp.float32).astype(o_ref.dtype)


def _matmul(x, w, tm, tn):
    m, k = x.shape
    n = w.shape[1]
    return pl.pallas_call(
        _mm_kernel,
        grid=(m // tm, n // tn),
        in_specs=[pl.BlockSpec((tm, k), lambda i, j: (i, 0)),
                  pl.BlockSpec((k, tn), lambda i, j: (0, j))],
        out_specs=pl.BlockSpec((tm, tn), lambda i, j: (i, j)),
        out_shape=jax.ShapeDtypeStruct((m, n), jnp.float32),
        compiler_params=_params("arbitrary", "arbitrary"),
        name="matmul",
    )(x, w)


def _modulated_norm(x_ref, g_ref, sc_ref, sh_ref, store):
    g = g_ref[...]
    sc = 1.0 + sc_ref[...]
    sh = sh_ref[...]

    def body(r, carry):
        rows = pl.ds(pl.multiple_of(r * NORM_ROWS, NORM_ROWS), NORM_ROWS)
        x = x_ref[rows, :]
        y = x * lax.rsqrt(jnp.mean(x * x, axis=-1, keepdims=True) + NORM_EPS)
        store(rows, (y * g) * sc + sh)
        return carry

    lax.fori_loop(0, x_ref.shape[0] // NORM_ROWS, body, 0)


def _norm_mm_kernel(x_ref, g_ref, sc_ref, sh_ref, w_ref, wdt_ref, o_ref, dt_ref, h_ref):
    @pl.when(pl.program_id(1) == 0)
    def _():
        def store(rows, h):
            h_ref[rows, :] = h.astype(jnp.bfloat16)
        _modulated_norm(x_ref, g_ref, sc_ref, sh_ref, store)
        dt_ref[...] = jnp.dot(h_ref[...], wdt_ref[...], preferred_element_type=jnp.float32)

    o_ref[...] = jnp.dot(h_ref[...], w_ref[...], preferred_element_type=jnp.float32)


def _norm_matmul(x, gain, scale, shift, w, w_dt):
    mod_spec = pl.BlockSpec((None, 1, D_MODEL), lambda i, j: (_mod_row(i), 0, 0))
    return pl.pallas_call(
        _norm_mm_kernel,
        grid=(N_TOK // ROW_TILE, PROJ_DIM // PROJ_TILE),
        in_specs=[pl.BlockSpec((ROW_TILE, D_MODEL), lambda i, j: (i, 0)),
                  pl.BlockSpec((1, D_MODEL), lambda i, j: (0, 0)),
                  mod_spec, mod_spec,
                  pl.BlockSpec((D_MODEL, PROJ_TILE), lambda i, j: (0, j)),
                  pl.BlockSpec((D_MODEL, LANES), lambda i, j: (0, 0))],
        out_specs=[pl.BlockSpec((ROW_TILE, PROJ_TILE), lambda i, j: (i, j)),
                   pl.BlockSpec((ROW_TILE, LANES), lambda i, j: (i, 0))],
        out_shape=[jax.ShapeDtypeStruct((N_TOK, PROJ_DIM), jnp.float32),
                   jax.ShapeDtypeStruct((N_TOK, LANES), jnp.float32)],
        scratch_shapes=[pltpu.VMEM((ROW_TILE, D_MODEL), jnp.bfloat16)],
        compiler_params=_params("arbitrary", "arbitrary"),
        name="norm_in_proj",
    )(x, gain, scale, shift, w, w_dt)


def _conv_kernel(prev_ref, x_ref, next_ref, w_ref, b_ref, o_ref, ext_ref, *, tiles_per_seq):
    i = pl.program_id(0)
    t = x_ref.shape[0]
    pad = SSD_CONV // 2
    first = (i % tiles_per_seq) == 0
    last = (i % tiles_per_seq) == tiles_per_seq - 1
    ext_ref[0:SUBLANES, :] = jnp.where(first, 0.0, prev_ref[...])
    ext_ref[SUBLANES:SUBLANES + t, :] = x_ref[...]
    ext_ref[SUBLANES + t:2 * SUBLANES + t, :] = jnp.where(last, 0.0, next_ref[...])
    for r0 in range(0, t, CONV_SUB):
        y = jnp.broadcast_to(b_ref[...], (CONV_SUB, CONV_COLS))
        for j in range(SSD_CONV):
            start = SUBLANES - pad + j + r0
            y = y + ext_ref[start:start + CONV_SUB, :] * w_ref[j:j + 1, :]
        o_ref[r0:r0 + CONV_SUB, :] = y * jax.nn.sigmoid(y)


def _ssd_conv(proj, row0, n_rows, seq_len, w, b):
    t = min(seq_len, CONV_ROWS)
    r0 = row0 // t
    c0 = OFF_XBC // CONV_COLS
    sub = t // SUBLANES
    n_sub = proj.shape[0] // SUBLANES
    return pl.pallas_call(
        functools.partial(_conv_kernel, tiles_per_seq=seq_len // t),
        grid=(n_rows // t, SSD_CONV_DIM // CONV_COLS),
        in_specs=[pl.BlockSpec((SUBLANES, CONV_COLS), lambda i, j: (jnp.maximum((r0 + i) * sub - 1, 0), c0 + j)),
                  pl.BlockSpec((t, CONV_COLS), lambda i, j: (r0 + i, c0 + j)),
                  pl.BlockSpec((SUBLANES, CONV_COLS),
                               lambda i, j: (jnp.minimum((r0 + i + 1) * sub, n_sub - 1), c0 + j)),
                  pl.BlockSpec((SSD_CONV, CONV_COLS), lambda i, j: (0, j)),
                  pl.BlockSpec((1, CONV_COLS), lambda i, j: (0, j))],
        out_specs=pl.BlockSpec((t, CONV_COLS), lambda i, j: (i, j)),
        out_shape=jax.ShapeDtypeStruct((n_rows, SSD_CONV_DIM), jnp.float32),
        scratch_shapes=[pltpu.VMEM((t + 2 * SUBLANES, CONV_COLS), jnp.float32)],
        compiler_params=_params("arbitrary", "arbitrary"),
        name="ssd_conv",
    )(proj, proj, proj, w, b)


def _softplus(x):
    return jnp.maximum(x, 0.0) + jnp.log1p(jnp.exp(-jnp.abs(x)))


def _expand(xs, sel, terms):
    parts = [jnp.concatenate(_split3(x)[:terms], axis=1) for x in xs]
    out = jnp.dot(jnp.concatenate(parts, axis=0), sel, preferred_element_type=jnp.float32)
    rows = xs[0].shape[0]
    return [out[i * rows:(i + 1) * rows] for i in range(len(xs))]


def _ssd_kernel(xf_ref, xb_ref, dtf_ref, dtb_ref, dtbias_ref, aneg_ref, init_ref, yf_ref, yb_ref, st_ref,
                s_ref, *, nc):
    bf16, f32 = jnp.bfloat16, jnp.float32
    c = SSD_C
    hd, ns = SSD_HEAD_DIM, SSD_STATE
    pair_w = 2 * hd
    j = pl.program_id(1)

    @pl.when(j == 0)
    def _():
        s_ref[...] = init_ref[...]

    t_ids = lax.broadcasted_iota(jnp.int32, (c, c), 0)
    s_ids = lax.broadcasted_iota(jnp.int32, (c, c), 1)
    eye = (lax.broadcasted_iota(jnp.int32, (SSD_HEADS, SSD_HEADS), 0)
           == lax.broadcasted_iota(jnp.int32, (SSD_HEADS, SSD_HEADS), 1)).astype(bf16)
    head_of = lambda terms, n, w: (lax.broadcasted_iota(jnp.int32, (terms * SSD_HEADS, n), 1) // w
                                   == lax.broadcasted_iota(jnp.int32, (terms * SSD_HEADS, n), 0) % SSD_HEADS
                                   ).astype(bf16)
    sel_x = head_of(2, SSD_DIM, hd)
    sel_c = head_of(3, SSD_HEADS * c, c)
    low_lanes = lax.broadcasted_iota(jnp.int32, (c, pair_w), 1) < hd
    low_rows = lax.broadcasted_iota(jnp.int32, (pair_w, ns), 0) < hd

    for d, (x_ref, dt_ref, y_ref) in enumerate(((xf_ref, dtf_ref, yf_ref), (xb_ref, dtb_ref, yb_ref))):
        rev = d == 1
        causal = (s_ids >= t_ids) if rev else (s_ids <= t_ids)
        tri = causal.astype(bf16)
        dt = _softplus(dt_ref[:, d * SSD_HEADS:(d + 1) * SSD_HEADS] + dtbias_ref[d:d + 1, :])
        acs = _tri_cumsum(tri, dt * aneg_ref[d:d + 1, :])
        acs_t = sum(lax.dot_general(eye, part, _NT, preferred_element_type=f32) for part in _split3(acs))
        end = acs[0:1, :] if rev else acs[c - 1:c, :]
        dt_x, out_x, in_x = _expand([dt, jnp.exp(end - acs), jnp.exp(acs)], sel_x, 2)
        acs_c, = _expand([acs], sel_c, 3)
        end_decay = jnp.exp(end)
        for g in range(SSD_GROUPS):
            bg = x_ref[:, SSD_DIM + g * ns:SSD_DIM + (g + 1) * ns].astype(bf16)
            cg = x_ref[:, SSD_DIM + (SSD_GROUPS + g) * ns:SSD_DIM + (SSD_GROUPS + g + 1) * ns].astype(bf16)
            cb = lax.dot_general(cg, bg, _NT, preferred_element_type=f32)
            for p in range(g * 2, g * 2 + 2):
                lanes = slice(p * pair_w, (p + 1) * pair_w)
                scores = []
                for h in (2 * p, 2 * p + 1):
                    decay = jnp.exp(jnp.minimum(acs_c[:, h * c:(h + 1) * c] - acs_t[h:h + 1, :], 0.0))
                    scores.append(jnp.where(causal, cb * decay, 0.0).astype(bf16))
                xdt = x_ref[:, lanes] * dt_x[:, lanes]
                rhs = jnp.concatenate([jnp.where(low_lanes, xdt, 0.0), jnp.where(low_lanes, 0.0, xdt)],
                                      axis=0).astype(bf16)
                s_p = s_ref[d, p]
                y = (jnp.dot(jnp.concatenate(scores, axis=1), rhs, preferred_element_type=f32)
                     + lax.dot_general(cg, s_p.astype(bf16), _NT, preferred_element_type=f32) * in_x[:, lanes])
                y_ref[:, lanes] = y
                keep = jnp.where(low_rows, end_decay[:, 2 * p:2 * p + 1], end_decay[:, 2 * p + 1:2 * p + 2])
                s_ref[d, p] = s_p * keep + lax.dot_general((xdt * out_x[:, lanes]).astype(bf16), bg, _TN,
                                                           preferred_element_type=f32)

    @pl.when(j == nc - 1)
    def _():
        st_ref[...] = s_ref[...]


def _ssd_scan(xbc, dt_logits, row0, nb, length, dt_bias, a_neg, init):
    c = SSD_C
    nc = length // c
    r0 = row0 // c
    pair_state = (nb, 2, SSD_HEADS // 2, 2 * SSD_HEAD_DIM, SSD_STATE)
    state_spec = pl.BlockSpec((None,) + pair_state[1:], lambda b, j: (b, 0, 0, 0, 0))
    y_f, y_b, states = pl.pallas_call(
        functools.partial(_ssd_kernel, nc=nc),
        grid=(nb, nc),
        in_specs=[pl.BlockSpec((c, SSD_CONV_DIM), lambda b, j: (b * nc + j, 0)),
                  pl.BlockSpec((c, SSD_CONV_DIM), lambda b, j: (b * nc + nc - 1 - j, 0)),
                  pl.BlockSpec((c, LANES), lambda b, j: (r0 + b * nc + j, 0)),
                  pl.BlockSpec((c, LANES), lambda b, j: (r0 + b * nc + nc - 1 - j, 0)),
                  pl.BlockSpec((2, SSD_HEADS), lambda b, j: (0, 0)),
                  pl.BlockSpec((2, SSD_HEADS), lambda b, j: (0, 0)),
                  state_spec],
        out_specs=[pl.BlockSpec((c, SSD_DIM), lambda b, j: (b * nc + j, 0)),
                   pl.BlockSpec((c, SSD_DIM), lambda b, j: (b * nc + nc - 1 - j, 0)),
                   state_spec],
        out_shape=[jax.ShapeDtypeStruct((nb * length, SSD_DIM), jnp.float32),
                   jax.ShapeDtypeStruct((nb * length, SSD_DIM), jnp.float32),
                   jax.ShapeDtypeStruct(pair_state, jnp.float32)],
        scratch_shapes=[pltpu.VMEM(pair_state[1:], jnp.float32)],
        compiler_params=_params("arbitrary", "arbitrary"),
        name="ssd_scan",
    )(xbc, xbc, dt_logits, dt_logits, dt_bias, a_neg, init.reshape(pair_state))
    return y_f, y_b, states.reshape(nb, 2, SSD_HEADS, SSD_HEAD_DIM, SSD_STATE)


def _ssd_finish_kernel(yf_ref, yb_ref, x_ref, z_ref, d_ref, g_ref, o_ref):
    z = z_ref[...]
    y = (yf_ref[...] + yb_ref[...] + d_ref[...] * x_ref[...]) * (z * jax.nn.sigmoid(z))
    o_ref[...] = (y * lax.rsqrt(jnp.mean(y * y, axis=-1, keepdims=True) + NORM_EPS) * g_ref[...]).astype(o_ref.dtype)


def _ssd_finish(y_f, y_b, xbc, proj, row0, d_row, gain):
    n = y_f.shape[0]
    t = FINISH_ROWS
    r0 = row0 // t
    row_spec = pl.BlockSpec((t, SSD_DIM), lambda i: (i, 0))
    vec_spec = pl.BlockSpec((1, SSD_DIM), lambda i: (0, 0))
    return pl.pallas_call(
        _ssd_finish_kernel,
        grid=(n // t,),
        in_specs=[row_spec, row_spec, row_spec,
                  pl.BlockSpec((t, SSD_DIM), lambda i: (r0 + i, OFF_Z // SSD_DIM)),
                  vec_spec, vec_spec],
        out_specs=row_spec,
        out_shape=jax.ShapeDtypeStruct((n, SSD_DIM), jnp.bfloat16),
        compiler_params=_params("arbitrary"),
        name="ssd_finish",
    )(y_f, y_b, xbc, proj, d_row, gain)


def _hgrn_chunk(q, g, kk, v, state, rev, tri, lane_mod, diag, level_masks, b_ref, kk_ref):
    bf16, f32 = jnp.bfloat16, jnp.float32
    c = q.shape[0]
    ng = c // SUBLANES
    b = _tri_cumsum(tri, g)
    yield
    b_ref[...] = b
    kk_ref[...] = kk
    row = lambda a_ref, r: a_ref[r:r + 1, :]
    grp = lambda a, i: a[i * SUBLANES:(i + 1) * SUBLANES, :]
    b_end = row(b_ref, 0) if rev else row(b_ref, c - 1)

    ones = jnp.ones((HG_KDIM, c), bf16)
    diag_rows = []
    for i0 in range(0, ng, DIAG_BATCH):
        tiles = []
        for i in range(i0, i0 + DIAG_BATCH):
            qg, bg = grp(q, i), grp(b, i)
            tiles += [qg * jnp.exp(bg - row(b_ref, i * SUBLANES + j)) * row(kk_ref, i * SUBLANES + j)
                      for j in range(SUBLANES)]
        sums = jnp.dot(jnp.concatenate(tiles, axis=0).astype(bf16), ones, preferred_element_type=f32)
        for n in range(DIAG_BATCH):
            base = n * SUBLANES * SUBLANES
            acc = sums[base:base + SUBLANES, :]
            for j in range(1, SUBLANES):
                acc = jnp.where(lane_mod[j], sums[base + j * SUBLANES:base + (j + 1) * SUBLANES, :], acc)
            diag_rows.append(acc)
        yield
    att = jnp.where(diag, jnp.concatenate(diag_rows, axis=0), 0.0)

    for m, mask in zip(HG_LEVELS, level_masks):
        half = m // 2
        q_side, k_side = [], []
        for i in range(ng):
            start = (i * SUBLANES) // m * m
            later = (i * SUBLANES) % m >= half
            ref = row(b_ref, start + half if rev else start + half - 1)
            if later != rev:
                q_side.append(grp(q, i) * jnp.exp(grp(b, i) - ref))
                k_side.append(jnp.zeros((SUBLANES, HG_KDIM), f32))
            else:
                q_side.append(jnp.zeros((SUBLANES, HG_KDIM), f32))
                k_side.append(grp(kk, i) * jnp.exp(ref - grp(b, i)))
        a_m = lax.dot_general(jnp.concatenate(q_side, axis=0).astype(bf16),
                              jnp.concatenate(k_side, axis=0).astype(bf16), _NT, preferred_element_type=f32)
        att = att + (a_m if m == c else jnp.where(mask, a_m, 0.0))
        yield

    q_in = (q * jnp.exp(b)).astype(bf16)
    k_out = (kk * jnp.exp(b_end - b)).astype(bf16)
    yield
    s_t = state[0]
    o = (lax.dot_general(q_in, s_t.astype(bf16), _NT, preferred_element_type=f32)
         + jnp.dot(att.astype(bf16), v.astype(bf16), preferred_element_type=f32))
    state[0] = s_t * jnp.exp(b_end) + lax.dot_general(v.astype(bf16), k_out, _TN, preferred_element_type=f32)
    return o


def _in_lockstep(gens):
    results = [None] * len(gens)
    live = list(range(len(gens)))
    while live:
        for i in list(live):
            try:
                next(gens[i])
            except StopIteration as stop:
                results[i] = stop.value
                live.remove(i)
    return results


def _hgrn_kernel(q_ref, ff_ref, fb_ref, v_ref, gate_ref, lb_ref, gn_ref, init_ref, o_ref, st_ref,
                 acc_ref, s_ref, b_ref, kk_ref, *, nc):
    c = HG_C
    t_ids = lax.broadcasted_iota(jnp.int32, (c, c), 0)
    s_ids = lax.broadcasted_iota(jnp.int32, (c, c), 1)
    causal = (s_ids <= t_ids, s_ids >= t_ids)
    tri = tuple(m.astype(jnp.bfloat16) for m in causal)
    same_group = (t_ids // SUBLANES) == (s_ids // SUBLANES)
    diag = tuple(m & same_group for m in causal)
    lane_mod = [(s_ids[:SUBLANES] % SUBLANES) == j for j in range(SUBLANES)]

    def level_mask(m, rev):
        same = (t_ids // m) == (s_ids // m)
        t_late = (t_ids % m) >= m // 2
        s_late = (s_ids % m) >= m // 2
        return same & (t_late != s_late) & (t_late != rev)
    masks = tuple([level_mask(m, rev) for m in HG_LEVELS] for rev in (False, True))

    s_ref[0] = init_ref[0].T
    s_ref[1] = init_ref[1].T
    gain = gn_ref[...]

    per_step = HG_CHUNKS_PER_STEP if (nc // 2) % HG_CHUNKS_PER_STEP == 0 else 1
    half_steps = nc // 2 // per_step

    def run_step(j):
        states = [[s_ref[0]], [s_ref[1]]]
        rows, gens = [], []
        for u in range(per_step):
            for d in (0, 1):
                chunk = j * per_step + u if d == 0 else nc - 1 - (j * per_step + u)
                r = pl.ds(pl.multiple_of(chunk * c, c), c)
                x = q_ref[r, :]
                q = x * jax.nn.sigmoid(x)
                lb = lb_ref[d:d + 1, :]
                f = lb + (1.0 - lb) * jax.nn.sigmoid((ff_ref, fb_ref)[d][r, :])
                rows.append(r)
                gens.append(_hgrn_chunk(q, jnp.log(f), 1.0 - f, v_ref[r, :], states[d], d == 1, tri[d], lane_mod,
                                        diag[d], masks[d], b_ref.at[2 * u + d], kk_ref.at[2 * u + d]))
        outs = _in_lockstep(gens)
        for d in (0, 1):
            s_ref[d] = states[d][0]
        return rows, outs

    def first_half(j, carry):
        rows, outs = run_step(j)
        for r, o in zip(rows, outs):
            acc_ref[r, :] = o
        return carry

    def second_half(j, carry):
        rows, outs = run_step(j)
        for r, o in zip(rows, outs):
            o = o + acc_ref[r, :]
            y = o * lax.rsqrt(jnp.mean(o * o, axis=-1, keepdims=True) + NORM_EPS) * gain
            gate = gate_ref[r, :]
            o_ref[r, :] = (y * (gate * jax.nn.sigmoid(gate))).astype(o_ref.dtype)
        return carry

    lax.fori_loop(0, half_steps, first_half, 0)
    lax.fori_loop(half_steps, 2 * half_steps, second_half, 0)
    st_ref[0] = s_ref[0].T
    st_ref[1] = s_ref[1].T


def _hgrn_mixer(proj, row0, nb, length, lb, gain, init):
    nc = length // HG_C
    assert nc % 2 == 0
    seq = lambda off: pl.BlockSpec((length, HG_KDIM), lambda b, h, col=off // HG_KDIM: (row0 + b, col + h))
    state_spec = pl.BlockSpec((None, 2, None, HG_KDIM, HG_VDIM), lambda b, h: (b, 0, h, 0, 0))
    return pl.pallas_call(
        functools.partial(_hgrn_kernel, nc=nc),
        grid=(nb, HG_HEADS),
        in_specs=[seq(OFF_HQ), seq(OFF_HF), seq(OFF_HF + HG_FDIM), seq(OFF_HI), seq(OFF_HG),
                  pl.BlockSpec((2, HG_KDIM), lambda b, h: (0, h)),
                  pl.BlockSpec((1, HG_VDIM), lambda b, h: (0, h)),
                  state_spec],
        out_specs=[pl.BlockSpec((length, HG_VDIM), lambda b, h: (b, h)), state_spec],
        out_shape=[jax.ShapeDtypeStruct((nb * length, HG_DIM), jnp.bfloat16),
                   jax.ShapeDtypeStruct((nb, 2, HG_HEADS, HG_KDIM, HG_VDIM), jnp.float32)],
        scratch_shapes=[pltpu.VMEM((length, HG_VDIM), jnp.float32),
                        pltpu.VMEM((2, HG_VDIM, HG_KDIM), jnp.float32),
                        pltpu.VMEM((2 * HG_CHUNKS_PER_STEP, HG_C, HG_KDIM), jnp.float32),
                        pltpu.VMEM((2 * HG_CHUNKS_PER_STEP, HG_C, HG_KDIM), jnp.float32)],
        compiler_params=_params("arbitrary", "arbitrary"),
        name="hgrn_mixer",
    )(proj, proj, proj, proj, proj, lb, gain, init)


def _softmax_av(scores, values):
    f32, bf16 = jnp.float32, jnp.bfloat16
    m = functools.reduce(jnp.maximum, [jnp.max(s, axis=-1, keepdims=True) for s in scores])
    ps = [jnp.exp(s - m) for s in scores]
    denom = functools.reduce(jnp.add, [jnp.sum(p, axis=-1, keepdims=True) for p in ps])
    acc = functools.reduce(jnp.add, [jnp.dot(p.astype(bf16), v, preferred_element_type=f32)
                                     for p, v in zip(ps, values)])
    return acc / denom


def _ctx_attn_kernel(q_ref, k_ref, v_ref, nk_in_ref, nv_in_ref, o_ref, nk_ref, nv_ref):
    bf16 = jnp.bfloat16
    scale = ATT_HEAD_DIM ** -0.5
    k = k_ref[...]
    v = v_ref[...]
    nk_ref[...] = k
    nv_ref[...] = v
    s = lax.dot_general(q_ref[...].astype(bf16), k.astype(bf16), _NT, preferred_element_type=jnp.float32) * scale
    o_ref[...] = _softmax_av([s], [v.astype(bf16)]).astype(o_ref.dtype)


def _context_attention(proj, nb, length, new_k, new_v, layer):
    spec = lambda off: pl.BlockSpec((length, ATT_HEAD_DIM), lambda b, h, col=off // ATT_HEAD_DIM: (b, col + h))
    cache_spec = pl.BlockSpec((None, None, length, ATT_HEAD_DIM), lambda b, h: (b, layer, 0, h))
    any_spec = pl.BlockSpec(memory_space=pl.ANY)
    return pl.pallas_call(
        _ctx_attn_kernel,
        grid=(nb, ATT_HEADS),
        in_specs=[spec(OFF_AQ), spec(OFF_AK), spec(OFF_AV), any_spec, any_spec],
        out_specs=[pl.BlockSpec((length, ATT_HEAD_DIM), lambda b, h: (b, h)), cache_spec, cache_spec],
        out_shape=[jax.ShapeDtypeStruct((nb * length, ATT_DIM), jnp.bfloat16),
                   jax.ShapeDtypeStruct(new_k.shape, new_k.dtype),
                   jax.ShapeDtypeStruct(new_v.shape, new_v.dtype)],
        input_output_aliases={3: 1, 4: 2},
        compiler_params=_params("arbitrary", "arbitrary"),
        name="context_attention",
    )(proj, proj, proj, new_k, new_v)


def _window_bias(rpb):
    col = jnp.arange(GRID_W)
    cs = jnp.clip(col - WIN_COLS // 2, 0, GRID_W - WIN_COLS)
    col_mask = (col[None, :] >= cs[:, None]) & (col[None, :] < cs[:, None] + WIN_COLS)
    dc_idx = jnp.clip(col[None, :] - col[:, None] + WIN_COLS - 1, 0, 2 * WIN_COLS - 2)
    bias = jnp.where(col_mask, rpb[:, :, dc_idx].astype(jnp.float32), MASKED)
    wins = [bias[:, d0:d0 + WIN_ROWS].transpose(0, 2, 1, 3).reshape(rpb.shape[0], GRID_W, WIN_ROWS * GRID_W)
            for d0 in range(WIN_ROWS)]
    return jnp.stack(wins, axis=1)


def _natten_kernel(q_ref, k_ref, v_ref, kc_ref, vc_ref, bias_ref, o_ref, kb_ref, vb_ref, *, rows):
    bf16, f32 = jnp.bfloat16, jnp.float32
    scale = ATT_HEAD_DIM ** -0.5
    win = WIN_ROWS * GRID_W

    def cast(i, carry):
        sl = pl.ds(pl.multiple_of(i * CAST_ROWS, CAST_ROWS), CAST_ROWS)
        kb_ref[sl, :] = k_ref[sl, :].astype(bf16)
        vb_ref[sl, :] = v_ref[sl, :].astype(bf16)
        return carry
    lax.fori_loop(0, rows * GRID_W // CAST_ROWS, cast, 0)

    kc = kc_ref[...].astype(bf16)
    vc = vc_ref[...].astype(bf16)

    def row_block(r):
        rs = jnp.clip(r - WIN_ROWS // 2, 0, rows - WIN_ROWS)
        d0 = rs - r + WIN_ROWS - 1
        q = q_ref[pl.ds(pl.multiple_of(r * GRID_W, GRID_W), GRID_W), :].astype(bf16)
        keys = pl.ds(pl.multiple_of(rs * GRID_W, GRID_W), win)
        s_lat = lax.dot_general(q, kb_ref[keys, :], _NT, preferred_element_type=f32) * scale + bias_ref[d0]
        s_ctx = lax.dot_general(q, kc, _NT, preferred_element_type=f32) * scale
        yield
        m = jnp.maximum(jnp.max(s_lat, axis=-1, keepdims=True), jnp.max(s_ctx, axis=-1, keepdims=True))
        yield
        p_lat = jnp.exp(s_lat - m)
        p_ctx = jnp.exp(s_ctx - m)
        denom = jnp.sum(p_lat, axis=-1, keepdims=True) + jnp.sum(p_ctx, axis=-1, keepdims=True)
        acc = (jnp.dot(p_lat.astype(bf16), vb_ref[keys, :], preferred_element_type=f32)
               + jnp.dot(p_ctx.astype(bf16), vc, preferred_element_type=f32))
        yield
        o_ref[pl.ds(pl.multiple_of(r * GRID_W, GRID_W), GRID_W), :] = (acc / denom).astype(o_ref.dtype)

    def row_group(g, carry):
        _in_lockstep([row_block(g * ROW_UNROLL + u) for u in range(ROW_UNROLL)])
        return carry
    lax.fori_loop(0, rows // ROW_UNROLL, row_group, 0)


def _neighbourhood_attention(proj, row0, nb, length, cache_k, cache_v, layer, bias_win):
    rows = length // GRID_W
    past = cache_k.shape[2]
    spec = lambda off: pl.BlockSpec((length, ATT_HEAD_DIM),
                                    lambda b, h, col=off // ATT_HEAD_DIM: (row0 + b, col + h))
    cache_spec = pl.BlockSpec((None, None, past, ATT_HEAD_DIM), lambda b, h: (b, layer, 0, h))
    return pl.pallas_call(
        functools.partial(_natten_kernel, rows=rows),
        grid=(nb, ATT_HEADS),
        in_specs=[spec(OFF_AQ), spec(OFF_AK), spec(OFF_AV), cache_spec, cache_spec,
                  pl.BlockSpec((None, WIN_ROWS, GRID_W, WIN_ROWS * GRID_W), lambda b, h: (h, 0, 0, 0))],
        out_specs=pl.BlockSpec((length, ATT_HEAD_DIM), lambda b, h: (b, h)),
        out_shape=jax.ShapeDtypeStruct((nb * length, ATT_DIM), jnp.bfloat16),
        scratch_shapes=[pltpu.VMEM((length, ATT_HEAD_DIM), jnp.bfloat16),
                        pltpu.VMEM((length, ATT_HEAD_DIM), jnp.bfloat16)],
        compiler_params=_params("arbitrary", "arbitrary"),
        name="neighbourhood_attention",
    )(proj, proj, proj, cache_k, cache_v, bias_win)


def _merge_kernel(ca_ref, cb_ref, cc_ref, la_ref, lb_ref, lc_ref, wa_ref, wb_ref, wc_ref,
                  ga_ref, gb_ref, gc_ref, o_ref):
    f32 = jnp.float32
    is_ctx = pl.program_id(0) < N_CTX // ROW_TILE

    def merge(ya_ref, yb_ref, yc_ref):
        acc = jax.nn.sigmoid(ga_ref[...]) * jnp.dot(ya_ref[...], wa_ref[...], preferred_element_type=f32)
        acc = acc + jax.nn.sigmoid(gb_ref[...]) * jnp.dot(yb_ref[...], wb_ref[...], preferred_element_type=f32)
        acc = acc + jax.nn.sigmoid(gc_ref[...]) * jnp.dot(yc_ref[...], wc_ref[...], preferred_element_type=f32)
        o_ref[...] = acc.astype(o_ref.dtype)

    @pl.when(is_ctx)
    def _():
        merge(ca_ref, cb_ref, cc_ref)

    @pl.when(jnp.logical_not(is_ctx))
    def _():
        merge(la_ref, lb_ref, lc_ref)


def _branch_merge(ys_ctx, ys_lat, ws, proj):
    tm, tn = ROW_TILE, MERGE_COLS
    kdim = ws[0].shape[0]
    cg = OFF_GATES // tn
    per = D_MODEL // tn
    ctx_tiles = N_CTX // tm
    ctx_spec = pl.BlockSpec((tm, kdim), lambda i, j: (jnp.minimum(i, ctx_tiles - 1), 0))
    lat_spec = pl.BlockSpec((tm, kdim), lambda i, j: (jnp.maximum(i - ctx_tiles, 0), 0))
    w_spec = pl.BlockSpec((kdim, tn), lambda i, j: (0, j))
    g_spec = lambda b: pl.BlockSpec((tm, tn), lambda i, j, b=b: (i, cg + b * per + j))
    return pl.pallas_call(
        _merge_kernel,
        grid=(N_TOK // tm, D_MODEL // tn),
        in_specs=[ctx_spec] * N_BRANCH + [lat_spec] * N_BRANCH + [w_spec] * N_BRANCH
                 + [g_spec(b) for b in range(N_BRANCH)],
        out_specs=pl.BlockSpec((tm, tn), lambda i, j: (i, j)),
        out_shape=jax.ShapeDtypeStruct((N_TOK, D_MODEL), jnp.bfloat16),
        compiler_params=_params("arbitrary", "arbitrary"),
        name="branch_merge",
    )(*ys_ctx, *ys_lat, *ws, proj, proj, proj)


def _out_residual_kernel(m_ref, w_ref, x_ref, gate_ref, o_ref):
    o_ref[...] = x_ref[...] + gate_ref[...] * jnp.dot(m_ref[...], w_ref[...], preferred_element_type=jnp.float32)


def _out_residual(merged, w, x, gate):
    tm, tn = ROW_TILE, COL_TILE
    return pl.pallas_call(
        _out_residual_kernel,
        grid=(N_TOK // tm, D_MODEL // tn),
        in_specs=[pl.BlockSpec((tm, D_MODEL), lambda i, j: (i, 0)),
                  pl.BlockSpec((D_MODEL, tn), lambda i, j: (0, j)),
                  pl.BlockSpec((tm, tn), lambda i, j: (i, j)),
                  pl.BlockSpec((None, 1, tn), lambda i, j: (_mod_row(i), 0, j))],
        out_specs=pl.BlockSpec((tm, tn), lambda i, j: (i, j)),
        out_shape=jax.ShapeDtypeStruct((N_TOK, D_MODEL), jnp.float32),
        compiler_params=_params("arbitrary", "arbitrary"),
        name="out_residual",
    )(merged, w, x, gate)


def _first_max(vals):
    best, idx = vals[0], jnp.zeros(vals[0].shape, jnp.int32)
    for k in range(1, len(vals)):
        better = vals[k] > best
        best = jnp.where(better, vals[k], best)
        idx = jnp.where(better, k, idx)
    return best, idx


def _pick(idx, vals):
    out = vals[0]
    for k in range(1, len(vals)):
        out = jnp.where(idx == k, vals[k], out)
    return out


def _norm_router_kernel(x_ref, g_ref, sc_ref, sh_ref, wr_hi_ref, wr_lo_ref, rb_ref,
                        h_ref, ids_ref, wts_ref, cnt_ref, lo_ref, base_ref):
    f32, bf16 = jnp.float32, jnp.bfloat16
    t = x_ref.shape[0]

    @pl.when(pl.program_id(0) == 0)
    def _():
        base_ref[...] = jnp.zeros_like(base_ref)

    def store(rows, h):
        h_hi = h.astype(bf16)
        h_ref[rows, :] = h_hi
        lo_ref[rows, :] = (h - h_hi.astype(f32)).astype(bf16)
    _modulated_norm(x_ref, g_ref, sc_ref, sh_ref, store)

    logits = (jnp.dot(h_ref[...], wr_hi_ref[...], preferred_element_type=f32)
              + (jnp.dot(h_ref[...], wr_lo_ref[...], preferred_element_type=f32)
                 + jnp.dot(lo_ref[...], wr_hi_ref[...], preferred_element_type=f32)))
    logits = logits.T[:N_EXPERTS]
    scores = jax.nn.sigmoid(logits)
    sel = scores + rb_ref[...]
    row = lambda a, e: a[e:e + 1, :]

    group_scores = []
    for g in range(N_EXPERT_GROUPS):
        v = [row(sel, g * EXPERTS_PER_GROUP + k) for k in range(EXPERTS_PER_GROUP)]
        pair_sums = [v[a] + v[b] for a in range(EXPERTS_PER_GROUP) for b in range(a + 1, EXPERTS_PER_GROUP)]
        group_scores.append(functools.reduce(jnp.maximum, pair_sums))
    _, grp = _first_max(group_scores)

    in_sel = [_pick(grp, [row(sel, g * EXPERTS_PER_GROUP + k) for g in range(N_EXPERT_GROUPS)])
              for k in range(EXPERTS_PER_GROUP)]
    in_score = [_pick(grp, [row(scores, g * EXPERTS_PER_GROUP + k) for g in range(N_EXPERT_GROUPS)])
                for k in range(EXPERTS_PER_GROUP)]
    _, i1 = _first_max(in_sel)
    _, i2 = _first_max([jnp.where(i1 == k, -jnp.inf, in_sel[k]) for k in range(EXPERTS_PER_GROUP)])
    s1 = _pick(i1, in_score)
    s2 = _pick(i2, in_score)
    e1 = grp * EXPERTS_PER_GROUP + i1
    e2 = grp * EXPERTS_PER_GROUP + i2

    e_ids = lax.broadcasted_iota(jnp.int32, (N_EXPERTS, t), 0)
    hit1 = e_ids == e1
    hit2 = e_ids == e2
    cnt = jnp.where(hit1 | hit2, 1.0, 0.0).astype(bf16)
    before = (lax.broadcasted_iota(jnp.int32, (t, t), 0) < lax.broadcasted_iota(jnp.int32, (t, t), 1)).astype(bf16)
    prefix = jnp.dot(cnt, before, preferred_element_type=f32) + base_ref[:, 0:1]
    rank1 = jnp.sum(jnp.where(hit1, prefix, 0.0), axis=0, keepdims=True)
    rank2 = jnp.sum(jnp.where(hit2, prefix, 0.0), axis=0, keepdims=True)
    base_ref[...] = base_ref[...] + jnp.dot(cnt, jnp.ones((t, LANES), bf16), preferred_element_type=f32)
    cnt_ref[...] = base_ref[...]

    zeros = jnp.zeros((SUBLANES - 4, t), jnp.int32)
    ids_ref[...] = jnp.concatenate([e1, e2, rank1.astype(jnp.int32), rank2.astype(jnp.int32), zeros], axis=0)
    total = s1 + s2
    wts_ref[...] = jnp.concatenate([s1 / total, s2 / total, jnp.zeros((SUBLANES - 2, t), f32)], axis=0)


def _norm_router(x, gain, scale, shift, wr_hi, wr_lo, router_bias):
    t = ROUTER_TILE
    per = ROW_TILE // t
    mod_spec = pl.BlockSpec((None, 1, D_MODEL), lambda i: (_mod_row(i // per), 0, 0))
    return pl.pallas_call(
        _norm_router_kernel,
        grid=(N_TOK // t,),
        in_specs=[pl.BlockSpec((t, D_MODEL), lambda i: (i, 0)),
                  pl.BlockSpec((1, D_MODEL), lambda i: (0, 0)),
                  mod_spec, mod_spec,
                  pl.BlockSpec((D_MODEL, LANES), lambda i: (0, 0)),
                  pl.BlockSpec((D_MODEL, LANES), lambda i: (0, 0)),
                  pl.BlockSpec((N_EXPERTS, 1), lambda i: (0, 0))],
        out_specs=[pl.BlockSpec((t, D_MODEL), lambda i: (i, 0)),
                   pl.BlockSpec((SUBLANES, t), lambda i: (0, i)),
                   pl.BlockSpec((SUBLANES, t), lambda i: (0, i)),
                   pl.BlockSpec((N_EXPERTS, LANES), lambda i: (0, 0))],
        out_shape=[jax.ShapeDtypeStruct((N_TOK, D_MODEL), jnp.bfloat16),
                   jax.ShapeDtypeStruct((SUBLANES, N_TOK), jnp.int32),
                   jax.ShapeDtypeStruct((SUBLANES, N_TOK), jnp.float32),
                   jax.ShapeDtypeStruct((N_EXPERTS, LANES), jnp.float32)],
        scratch_shapes=[pltpu.VMEM((t, D_MODEL), jnp.bfloat16),
                        pltpu.VMEM((N_EXPERTS, LANES), jnp.float32)],
        compiler_params=_params("arbitrary"),
        name="norm_router",
    )(x, gain, scale, shift, wr_hi, wr_lo, router_bias)


def _expert_kernel(be_ref, na_ref, x_ref, w1_ref, w3_ref, w2_ref, o_ref):
    active = pl.program_id(0) < na_ref[0]

    @pl.when(active)
    def _():
        bf16 = jnp.bfloat16
        x = x_ref[...]
        a = jnp.dot(x, w1_ref[...].astype(bf16), preferred_element_type=jnp.float32)
        b = jnp.dot(x, w3_ref[...].astype(bf16), preferred_element_type=jnp.float32)
        hdn = (a * jax.nn.sigmoid(a)) * b
        o_ref[...] = jnp.dot(hdn.astype(bf16), w2_ref[...].astype(bf16),
                             preferred_element_type=jnp.float32).astype(o_ref.dtype)

    @pl.when(jnp.logical_not(active))
    def _():
        o_ref[...] = jnp.zeros_like(o_ref)


def _expert_blocks(buf, block_e, n_active, w1, w3, w2):
    n_blocks = buf.shape[0] // MOE_BLOCK
    w_spec = lambda r, c: pl.BlockSpec((None, r, c), lambda i, be, na: (be[i], 0, 0), pipeline_mode=pl.Buffered(1))
    grid_spec = pltpu.PrefetchScalarGridSpec(
        num_scalar_prefetch=2,
        grid=(n_blocks,),
        in_specs=[pl.BlockSpec((MOE_BLOCK, D_MODEL), lambda i, be, na: (i, 0)),
                  w_spec(D_MODEL, D_EXPERT), w_spec(D_MODEL, D_EXPERT), w_spec(D_EXPERT, D_MODEL)],
        out_specs=pl.BlockSpec((MOE_BLOCK, D_MODEL), lambda i, be, na: (i, 0)),
    )
    return pl.pallas_call(
        _expert_kernel,
        grid_spec=grid_spec,
        out_shape=jax.ShapeDtypeStruct((n_blocks * MOE_BLOCK, D_MODEL), jnp.float32),
        compiler_params=_params("arbitrary"),
        name="moe_experts",
    )(block_e, n_active, buf, w1, w3, w2)


def _row_copy(out_hbm, buf_ref, sem_ref, slot, k, t, row):
    return pltpu.make_async_copy(out_hbm.at[pl.ds(row, 1), :], buf_ref.at[slot, k, pl.ds(t, 1), :], sem_ref.at[slot])


def _combine_kernel(dcur_ref, dnext_ref, w_ref, x_ref, gate_ref, out_hbm, o_ref, buf_ref, sem_ref):
    t_rows = x_ref.shape[0]
    i = pl.program_id(0)
    n = pl.num_programs(0)
    slot = i % 2

    def issue(d_ref, s):
        for t in range(t_rows):
            for k in range(TOP_K):
                _row_copy(out_hbm, buf_ref, sem_ref, s, k, t, d_ref[k, t]).start()

    @pl.when(i == 0)
    def _():
        issue(dcur_ref, 0)

    @pl.when(i + 1 < n)
    def _():
        issue(dnext_ref, 1 - slot)

    def wait(t, carry):
        for k in range(TOP_K):
            _row_copy(out_hbm, buf_ref, sem_ref, slot, k, t, 0).wait()
        return carry
    lax.fori_loop(0, t_rows, wait, 0, unroll=COMBINE_UNROLL)

    w = w_ref[...]
    moe = w[:, 0:1] * buf_ref[slot, 0] + w[:, 1:2] * buf_ref[slot, 1]
    o_ref[...] = x_ref[...] + gate_ref[...] * moe


def _moe_combine(x, gate, out, dest, wts_t):
    t = COMBINE_ROWS
    steps = N_TOK // t
    per = ROW_TILE // t
    smem_spec = lambda shift: pl.BlockSpec((SUBLANES, t), lambda i: (0, jnp.minimum(i + shift, steps - 1)),
                                           memory_space=pltpu.SMEM)
    return pl.pallas_call(
        _combine_kernel,
        grid=(steps,),
        in_specs=[smem_spec(0), smem_spec(1),
                  pl.BlockSpec((t, LANES), lambda i: (i, 0)),
                  pl.BlockSpec((t, D_MODEL), lambda i: (i, 0)),
                  pl.BlockSpec((None, 1, D_MODEL), lambda i: (_mod_row(i // per), 0, 0)),
                  pl.BlockSpec(memory_space=pl.ANY)],
        out_specs=pl.BlockSpec((t, D_MODEL), lambda i: (i, 0)),
        out_shape=jax.ShapeDtypeStruct((N_TOK, D_MODEL), jnp.float32),
        scratch_shapes=[pltpu.VMEM((2, TOP_K, t, D_MODEL), jnp.float32),
                        pltpu.SemaphoreType.DMA((2,))],
        compiler_params=_params("arbitrary"),
        name="moe_combine",
    )(dest, dest, wts_t, x, gate, out)


def _moe(x, gate, h, ids, wts, counts, layer, w1, w3, w2):
    nk = N_TOK * TOP_K
    n_blocks = (nk + N_EXPERTS * (MOE_BLOCK - 1)) // MOE_BLOCK
    counts = counts[:, 0].astype(jnp.int32)
    padded = (counts + MOE_BLOCK - 1) // MOE_BLOCK * MOE_BLOCK
    pad_end = jnp.cumsum(padded)
    pad_start = pad_end - padded
    dest1 = pad_start[ids[0]] + ids[2]
    dest2 = pad_start[ids[1]] + ids[3]
    tok = jnp.arange(N_TOK, dtype=jnp.int32)
    src = jnp.zeros((n_blocks * MOE_BLOCK,), jnp.int32).at[jnp.concatenate([dest1, dest2])].set(
        jnp.concatenate([tok, tok]))
    block_e = jnp.minimum(jnp.searchsorted(pad_end, jnp.arange(n_blocks, dtype=jnp.int32) * MOE_BLOCK, side='right'),
                          N_EXPERTS - 1).astype(jnp.int32)
    n_active = (pad_end[-1:] // MOE_BLOCK).astype(jnp.int32)
    out = _expert_blocks(h[src], block_e + layer * N_EXPERTS, n_active, w1, w3, w2)
    dest = jnp.concatenate([dest1[None, :], dest2[None, :], jnp.zeros((SUBLANES - TOP_K, N_TOK), jnp.int32)], axis=0)
    wts_t = jnp.pad(wts[:TOP_K].T, ((0, 0), (0, LANES - TOP_K)))
    return _moe_combine(x, gate, out, dest, wts_t)


def _final_norm_kernel(x_ref, g_ref, ctx_ref, lat_ref):
    g = g_ref[...]

    def norm_into(o_ref):
        def body(r, carry):
            rows = pl.ds(pl.multiple_of(r * NORM_ROWS, NORM_ROWS), NORM_ROWS)
            x = x_ref[rows, :]
            o_ref[rows, :] = (x * lax.rsqrt(jnp.mean(x * x, axis=-1, keepdims=True) + NORM_EPS)) * g
            return carry
        lax.fori_loop(0, x_ref.shape[0] // NORM_ROWS, body, 0)

    is_ctx = pl.program_id(0) < N_CTX // ROW_TILE

    @pl.when(is_ctx)
    def _():
        norm_into(ctx_ref)

    @pl.when(jnp.logical_not(is_ctx))
    def _():
        norm_into(lat_ref)


def _final_norm(x, gain):
    ctx_tiles = N_CTX // ROW_TILE
    return pl.pallas_call(
        _final_norm_kernel,
        grid=(N_TOK // ROW_TILE,),
        in_specs=[pl.BlockSpec((ROW_TILE, D_MODEL), lambda i: (i, 0)),
                  pl.BlockSpec((1, D_MODEL), lambda i: (0, 0))],
        out_specs=[pl.BlockSpec((ROW_TILE, D_MODEL), lambda i: (jnp.minimum(i, ctx_tiles - 1), 0)),
                   pl.BlockSpec((ROW_TILE, D_MODEL), lambda i: (jnp.maximum(i - ctx_tiles, 0), 0))],
        out_shape=[jax.ShapeDtypeStruct((N_CTX, D_MODEL), jnp.float32),
                   jax.ShapeDtypeStruct((N_LAT, D_MODEL), jnp.float32)],
        compiler_params=_params("arbitrary"),
        name="final_norm",
    )(x, gain)


def _split_w_in(w):
    c0 = SSD_DIM + SSD_CONV_DIM
    c1 = c0 + 2 * SSD_HEADS
    dt_cols = jnp.pad(w[:, c0:c1], ((0, 0), (0, LANES - 2 * SSD_HEADS)))
    return (jnp.concatenate([w[:, :c0], w[:, c1:]], axis=1).astype(jnp.bfloat16), dt_cols.astype(jnp.bfloat16))


def kernel(x_prompt, x_sample, cache_k, cache_v, state_ssd, state_hgrn, c, c_ctx, w_ada, b_ada, norm_mix, norm_moe, w_in, ssd_conv_w, ssd_conv_b, ssd_dt_bias, ssd_a_log, ssd_d, ssd_norm, hg_lb_logits, hg_norm, att_rpb, w_br_ssd, w_br_hg, w_br_att, w_out, w_router, router_bias, moe_w1, moe_w3, moe_w2, final_norm):
    bf16, f32 = jnp.bfloat16, jnp.float32
    lb_cum = jnp.cumsum(jax.nn.softmax(hg_lb_logits.astype(f32), axis=1), axis=1)
    lower_bounds = lb_cum - lb_cum[:, :1]

    x = jnp.concatenate([x_prompt.reshape(N_CTX, D_MODEL), x_sample.reshape(N_LAT, D_MODEL)], axis=0)

    n_mod = 1 + DEC_BATCH
    cond = jnp.concatenate([c_ctx[None, :], c], axis=0)
    cond = jnp.pad(jax.nn.silu(cond), ((0, 2 * SUBLANES - n_mod), (0, 0)))

    wr = jnp.pad(w_router, ((0, 0), (0, LANES - N_EXPERTS)))
    wr_hi = wr.astype(bf16)
    wr_lo = (wr - wr_hi.astype(f32)).astype(bf16)
    cache_k = cache_k.reshape(DEC_BATCH, DEPTH, -1, ATT_DIM)
    cache_v = cache_v.reshape(DEC_BATCH, DEPTH, -1, ATT_DIM)
    lat_row0 = N_CTX // DEC_SEQ
    zero_ssd = jnp.zeros((BATCH, 2, SSD_HEADS, SSD_HEAD_DIM, SSD_STATE), f32)
    zero_hg = jnp.zeros((BATCH, 2, HG_HEADS, HG_KDIM, HG_VDIM), f32)

    expert_w = tuple(w.reshape((DEPTH * N_EXPERTS,) + w.shape[2:]) for w in (moe_w1, moe_w3, moe_w2))

    new_k = jnp.zeros((BATCH, DEPTH, SEQ, ATT_DIM), f32)
    new_v = jnp.zeros((BATCH, DEPTH, SEQ, ATT_DIM), f32)
    new_ssd, new_hg = [], []
    for l in range(DEPTH):
        mod = _matmul(cond, w_ada[l], 2 * SUBLANES, COL_TILE)[:n_mod] + b_ada[l]
        mod = mod.reshape(n_mod, 6, 1, D_MODEL)
        shift_m, scale_m, gate_m, shift_f, scale_f, gate_f = (mod[:, i] for i in range(6))

        proj, dt_logits = _norm_matmul(x, norm_mix[l][None, :], scale_m, shift_m, *_split_w_in(w_in[l]))

        conv_b = ssd_conv_b[l][None, :]
        a_neg = -jnp.exp(ssd_a_log[l].astype(f32))
        d_row = jnp.repeat(ssd_d[l], SSD_HEAD_DIM)[None, :]
        ssd_gain = ssd_norm[l][None, :]
        y_ssd = []
        for row0, nb, length, init in ((0, BATCH, SEQ, zero_ssd), (N_CTX, DEC_BATCH, DEC_SEQ, state_ssd[:, l])):
            xbc = _ssd_conv(proj, row0, nb * length, length, ssd_conv_w[l], conv_b)
            y_f, y_b, states = _ssd_scan(xbc, dt_logits, row0, nb, length, ssd_dt_bias[l], a_neg, init)
            y_ssd.append(_ssd_finish(y_f, y_b, xbc, proj, row0, d_row, ssd_gain))
            if row0 == 0:
                new_ssd.append(states)

        lb = lower_bounds[:, l]
        hg_gain = hg_norm[l].reshape(1, HG_DIM)
        y_hg_ctx, states = _hgrn_mixer(proj, 0, BATCH, SEQ, lb, hg_gain, zero_hg)
        new_hg.append(states)
        y_hg_lat, _ = _hgrn_mixer(proj, lat_row0, DEC_BATCH, DEC_SEQ, lb, hg_gain, state_hgrn[:, l])

        y_att_ctx, new_k, new_v = _context_attention(proj, BATCH, SEQ, new_k, new_v, l)
        y_att_lat = _neighbourhood_attention(proj, lat_row0, DEC_BATCH, DEC_SEQ, cache_k, cache_v, l,
                                             _window_bias(att_rpb[l]))

        merged = _branch_merge((y_ssd[0], y_hg_ctx, y_att_ctx), (y_ssd[1], y_hg_lat, y_att_lat),
                               (w_br_ssd[l].astype(bf16), w_br_hg[l].astype(bf16), w_br_att[l].astype(bf16)), proj)
        x = _out_residual(merged, w_out[l].astype(bf16), x, gate_m)

        h2, ids, wts, counts = _norm_router(x, norm_moe[l][None, :], scale_f, shift_f, wr_hi, wr_lo,
                                            router_bias.astype(f32)[:, None])
        x = _moe(x, gate_f, h2, ids, wts, counts, l, *expert_w)

    y_ctx, y_lat = _final_norm(x, final_norm[None, :])
    y_prompt = y_ctx.reshape(BATCH, SEQ, D_MODEL)
    y_sample = y_lat.reshape(DEC_BATCH, DEC_SEQ, D_MODEL)
    cache_shape = (BATCH, DEPTH, SEQ, ATT_HEADS, ATT_HEAD_DIM)
    return (y_prompt, y_sample, new_k.reshape(cache_shape), new_v.reshape(cache_shape),
            jnp.stack(new_ssd, axis=1), jnp.stack(new_hg, axis=1))
```

```python
import functools

import jax
import jax.numpy as jnp
from jax import lax
from jax.experimental import pallas as pl
from jax.experimental.pallas import tpu as pltpu

D_MODEL = 2048
BATCH = 32
SEQ = 256
DEPTH = 2
DEC_BATCH = 8
DEC_SEQ = 4096
GRID_W = 64
NORM_EPS = 1e-6
SSD_HEADS = 16
SSD_HEAD_DIM = 64
SSD_DIM = SSD_HEADS * SSD_HEAD_DIM
SSD_STATE = 64
SSD_GROUPS = 4
SSD_CONV = 5
SSD_CONV_DIM = SSD_DIM + 2 * SSD_GROUPS * SSD_STATE
HG_HEADS = 8
HG_KDIM = 128
HG_VDIM = 128
HG_FDIM = HG_HEADS * HG_KDIM
HG_DIM = HG_HEADS * HG_VDIM
ATT_HEADS = 8
ATT_HEAD_DIM = 128
ATT_DIM = ATT_HEADS * ATT_HEAD_DIM
WIN_ROWS = 8
WIN_COLS = 16
N_BRANCH = 3
N_EXPERTS = 16
N_EXPERT_GROUPS = 4
EXPERTS_PER_GROUP = N_EXPERTS // N_EXPERT_GROUPS
TOP_K = 2
D_EXPERT = 1024

N_CTX = BATCH * SEQ
N_LAT = DEC_BATCH * DEC_SEQ
N_TOK = N_CTX + N_LAT

VMEM_LIMIT_BYTES = 56 * 1024 * 1024
LANES = 128
SUBLANES = 8

OFF_Z = 0
OFF_XBC = OFF_Z + SSD_DIM
OFF_HQ = OFF_XBC + SSD_CONV_DIM
OFF_HF = OFF_HQ + HG_FDIM
OFF_HI = OFF_HF + 2 * HG_FDIM
OFF_HG = OFF_HI + HG_DIM
OFF_AQ = OFF_HG + HG_DIM
OFF_AK = OFF_AQ + ATT_DIM
OFF_AV = OFF_AK + ATT_DIM
OFF_GATES = OFF_AV + ATT_DIM
PROJ_DIM = OFF_GATES + N_BRANCH * D_MODEL
PROJ_TILE = 1536

ROW_TILE = 1024
COL_TILE = 1024
MERGE_COLS = 512
NORM_ROWS = 64
ROUTER_TILE = 512
MOE_BLOCK = 512
CONV_ROWS = 1024
CONV_SUB = 256
CONV_COLS = 512
SSD_C = 128
FINISH_ROWS = 256
HG_C = 128
HG_LEVELS = (16, 32, 64, 128)
CAST_ROWS = 512
COMBINE_ROWS = 256
COMBINE_UNROLL = 8
DIAG_BATCH = 4
ROW_UNROLL = 8
HG_LOCKSTEP = 4
HG_CTX_HEADS = 2
MASKED = -1e30

_NT = (((1,), (1,)), ((), ()))
_TN = (((0,), (0,)), ((), ()))


def _params(*semantics):
    return pltpu.CompilerParams(dimension_semantics=semantics, vmem_limit_bytes=VMEM_LIMIT_BYTES)


def _mod_row(i):
    ctx_tiles = N_CTX // ROW_TILE
    tiles_per_req = DEC_SEQ // ROW_TILE
    return jnp.where(i < ctx_tiles, 0, 1 + (i - ctx_tiles) // tiles_per_req)


def _split3(x):
    bf16, f32 = jnp.bfloat16, jnp.float32
    x1 = x.astype(bf16)
    r = x - x1.astype(f32)
    x2 = r.astype(bf16)
    x3 = (r - x2.astype(f32)).astype(bf16)
    return x1, x2, x3


def _tri_cumsum(tri, x):
    x1, x2, x3 = _split3(x)
    f32 = jnp.float32
    return (jnp.dot(tri, x1, preferred_element_type=f32)
            + (jnp.dot(tri, x2, preferred_element_type=f32) + jnp.dot(tri, x3, preferred_element_type=f32)))


def _mm_kernel(x_ref, w_ref, o_ref):
    o_ref[...] = jnp.dot(x_ref[...].astype(jnp.bfloat16), w_ref[...].astype(jnp.bfloat16),
                         preferred_element_type=jnp.float32).astype(o_ref.dtype)


def _matmul(x, w, layer, tm, tn):
    m, k = x.shape
    n = w.shape[2]
    return pl.pallas_call(
        _mm_kernel,
        grid=(m // tm, n // tn),
        in_specs=[pl.BlockSpec((tm, k), lambda i, j: (i, 0)),
                  pl.BlockSpec((None, k, tn), lambda i, j: (layer, 0, j))],
        out_specs=pl.BlockSpec((tm, tn), lambda i, j: (i, j)),
        out_shape=jax.ShapeDtypeStruct((m, n), jnp.float32),
        compiler_params=_params("arbitrary", "arbitrary"),
        name="matmul",
    )(x, w)


def _modulated_norm(x_ref, g_ref, sc_ref, sh_ref, store):
    g = g_ref[...]
    sc = 1.0 + sc_ref[...]
    sh = sh_ref[...]

    def body(r, carry):
        rows = pl.ds(pl.multiple_of(r * NORM_ROWS, NORM_ROWS), NORM_ROWS)
        x = x_ref[rows, :]
        y = x * lax.rsqrt(jnp.mean(x * x, axis=-1, keepdims=True) + NORM_EPS)
        store(rows, (y * g) * sc + sh)
        return carry

    lax.fori_loop(0, x_ref.shape[0] // NORM_ROWS, body, 0)


def _norm_mm_kernel(x_ref, g_ref, sc_ref, sh_ref, w_ref, wdt_ref, o_ref, dt_ref, h_ref):
    @pl.when(pl.program_id(1) == 0)
    def _():
        def store(rows, h):
            h_ref[rows, :] = h.astype(jnp.bfloat16)
        _modulated_norm(x_ref, g_ref, sc_ref, sh_ref, store)
        dt_ref[...] = jnp.dot(h_ref[...], wdt_ref[...], preferred_element_type=jnp.float32)

    o_ref[...] = jnp.dot(h_ref[...], w_ref[...], preferred_element_type=jnp.float32)


def _norm_matmul(x, gain, scale, shift, w, w_dt):
    mod_spec = pl.BlockSpec((None, 1, D_MODEL), lambda i, j: (_mod_row(i), 0, 0))
    return pl.pallas_call(
        _norm_mm_kernel,
        grid=(N_TOK // ROW_TILE, PROJ_DIM // PROJ_TILE),
        in_specs=[pl.BlockSpec((ROW_TILE, D_MODEL), lambda i, j: (i, 0)),
                  pl.BlockSpec((1, D_MODEL), lambda i, j: (0, 0)),
                  mod_spec, mod_spec,
                  pl.BlockSpec((D_MODEL, PROJ_TILE), lambda i, j: (0, j)),
                  pl.BlockSpec((D_MODEL, LANES), lambda i, j: (0, 0))],
        out_specs=[pl.BlockSpec((ROW_TILE, PROJ_TILE), lambda i, j: (i, j)),
                   pl.BlockSpec((ROW_TILE, LANES), lambda i, j: (i, 0))],
        out_shape=[jax.ShapeDtypeStruct((N_TOK, PROJ_DIM), jnp.float32),
                   jax.ShapeDtypeStruct((N_TOK, LANES), jnp.float32)],
        scratch_shapes=[pltpu.VMEM((ROW_TILE, D_MODEL), jnp.bfloat16)],
        compiler_params=_params("arbitrary", "arbitrary"),
        name="norm_in_proj",
    )(x, gain, scale, shift, w, w_dt)


def _conv_kernel(prev_ref, x_ref, next_ref, w_ref, b_ref, o_ref, ext_ref, *, tiles_per_seq):
    i = pl.program_id(0)
    t = x_ref.shape[0]
    pad = SSD_CONV // 2
    first = (i % tiles_per_seq) == 0
    last = (i % tiles_per_seq) == tiles_per_seq - 1
    ext_ref[0:SUBLANES, :] = jnp.where(first, 0.0, prev_ref[...])
    ext_ref[SUBLANES:SUBLANES + t, :] = x_ref[...]
    ext_ref[SUBLANES + t:2 * SUBLANES + t, :] = jnp.where(last, 0.0, next_ref[...])
    for r0 in range(0, t, CONV_SUB):
        y = jnp.broadcast_to(b_ref[...], (CONV_SUB, CONV_COLS))
        for j in range(SSD_CONV):
            start = SUBLANES - pad + j + r0
            y = y + ext_ref[start:start + CONV_SUB, :] * w_ref[j:j + 1, :]
        o_ref[r0:r0 + CONV_SUB, :] = y * jax.nn.sigmoid(y)


def _ssd_conv(proj, row0, n_rows, seq_len, w, b):
    t = min(seq_len, CONV_ROWS)
    r0 = row0 // t
    c0 = OFF_XBC // CONV_COLS
    sub = t // SUBLANES
    n_sub = proj.shape[0] // SUBLANES
    return pl.pallas_call(
        functools.partial(_conv_kernel, tiles_per_seq=seq_len // t),
        grid=(n_rows // t, SSD_CONV_DIM // CONV_COLS),
        in_specs=[pl.BlockSpec((SUBLANES, CONV_COLS), lambda i, j: (jnp.maximum((r0 + i) * sub - 1, 0), c0 + j)),
                  pl.BlockSpec((t, CONV_COLS), lambda i, j: (r0 + i, c0 + j)),
                  pl.BlockSpec((SUBLANES, CONV_COLS),
                               lambda i, j: (jnp.minimum((r0 + i + 1) * sub, n_sub - 1), c0 + j)),
                  pl.BlockSpec((SSD_CONV, CONV_COLS), lambda i, j: (0, j)),
                  pl.BlockSpec((1, CONV_COLS), lambda i, j: (0, j))],
        out_specs=pl.BlockSpec((t, CONV_COLS), lambda i, j: (i, j)),
        out_shape=jax.ShapeDtypeStruct((n_rows, SSD_CONV_DIM), jnp.float32),
        scratch_shapes=[pltpu.VMEM((t + 2 * SUBLANES, CONV_COLS), jnp.float32)],
        compiler_params=_params("arbitrary", "arbitrary"),
        name="ssd_conv",
    )(proj, proj, proj, w, b)


def _softplus(x):
    return jnp.maximum(x, 0.0) + jnp.log1p(jnp.exp(-jnp.abs(x)))


def _expand(xs, sel, terms):
    parts = [jnp.concatenate(_split3(x)[:terms], axis=1) for x in xs]
    out = jnp.dot(jnp.concatenate(parts, axis=0), sel, preferred_element_type=jnp.float32)
    rows = xs[0].shape[0]
    return [out[i * rows:(i + 1) * rows] for i in range(len(xs))]


def _ssd_kernel(xf_ref, xb_ref, dtf_ref, dtb_ref, dtbias_ref, aneg_ref, init_ref, yf_ref, yb_ref, st_ref,
                s_ref, *, nc):
    bf16, f32 = jnp.bfloat16, jnp.float32
    c = SSD_C
    hd, ns = SSD_HEAD_DIM, SSD_STATE
    pair_w = 2 * hd
    j = pl.program_id(1)

    @pl.when(j == 0)
    def _():
        s_ref[...] = init_ref[...]

    t_ids = lax.broadcasted_iota(jnp.int32, (c, c), 0)
    s_ids = lax.broadcasted_iota(jnp.int32, (c, c), 1)
    eye = (lax.broadcasted_iota(jnp.int32, (SSD_HEADS, SSD_HEADS), 0)
           == lax.broadcasted_iota(jnp.int32, (SSD_HEADS, SSD_HEADS), 1)).astype(bf16)
    head_of = lambda terms, n, w: (lax.broadcasted_iota(jnp.int32, (terms * SSD_HEADS, n), 1) // w
                                   == lax.broadcasted_iota(jnp.int32, (terms * SSD_HEADS, n), 0) % SSD_HEADS
                                   ).astype(bf16)
    sel_x = head_of(2, SSD_DIM, hd)
    sel_c = head_of(3, SSD_HEADS * c, c)
    low_lanes = lax.broadcasted_iota(jnp.int32, (c, pair_w), 1) < hd
    low_rows = lax.broadcasted_iota(jnp.int32, (pair_w, ns), 0) < hd

    for d, (x_ref, dt_ref, y_ref) in enumerate(((xf_ref, dtf_ref, yf_ref), (xb_ref, dtb_ref, yb_ref))):
        rev = d == 1
        causal = (s_ids >= t_ids) if rev else (s_ids <= t_ids)
        tri = causal.astype(bf16)
        dt = _softplus(dt_ref[:, d * SSD_HEADS:(d + 1) * SSD_HEADS] + dtbias_ref[d:d + 1, :])
        acs = _tri_cumsum(tri, dt * aneg_ref[d:d + 1, :])
        acs_t = sum(lax.dot_general(eye, part, _NT, preferred_element_type=f32) for part in _split3(acs))
        end = acs[0:1, :] if rev else acs[c - 1:c, :]
        dt_x, out_x, in_x = _expand([dt, jnp.exp(end - acs), jnp.exp(acs)], sel_x, 2)
        acs_c, = _expand([acs], sel_c, 3)
        end_decay = jnp.exp(end)
        for g in range(SSD_GROUPS):
            bg = x_ref[:, SSD_DIM + g * ns:SSD_DIM + (g + 1) * ns].astype(bf16)
            cg = x_ref[:, SSD_DIM + (SSD_GROUPS + g) * ns:SSD_DIM + (SSD_GROUPS + g + 1) * ns].astype(bf16)
            cb = lax.dot_general(cg, bg, _NT, preferred_element_type=f32)
            for p in range(g * 2, g * 2 + 2):
                lanes = slice(p * pair_w, (p + 1) * pair_w)
                scores = []
                for h in (2 * p, 2 * p + 1):
                    decay = jnp.exp(jnp.minimum(acs_c[:, h * c:(h + 1) * c] - acs_t[h:h + 1, :], 0.0))
                    scores.append(jnp.where(causal, cb * decay, 0.0).astype(bf16))
                xdt = x_ref[:, lanes] * dt_x[:, lanes]
                rhs = jnp.concatenate([jnp.where(low_lanes, xdt, 0.0), jnp.where(low_lanes, 0.0, xdt)],
                                      axis=0).astype(bf16)
                s_p = s_ref[d, p]
                y = (jnp.dot(jnp.concatenate(scores, axis=1), rhs, preferred_element_type=f32)
                     + lax.dot_general(cg, s_p.astype(bf16), _NT, preferred_element_type=f32) * in_x[:, lanes])
                y_ref[:, lanes] = y
                keep = jnp.where(low_rows, end_decay[:, 2 * p:2 * p + 1], end_decay[:, 2 * p + 1:2 * p + 2])
                s_ref[d, p] = s_p * keep + lax.dot_general((xdt * out_x[:, lanes]).astype(bf16), bg, _TN,
                                                           preferred_element_type=f32)

    @pl.when(j == nc - 1)
    def _():
        st_ref[...] = s_ref[...]


def _ssd_scan(xbc, dt_logits, row0, nb, length, dt_bias, a_neg, init):
    c = SSD_C
    nc = length // c
    r0 = row0 // c
    pair_state = (nb, 2, SSD_HEADS // 2, 2 * SSD_HEAD_DIM, SSD_STATE)
    state_spec = pl.BlockSpec((None,) + pair_state[1:], lambda b, j: (b, 0, 0, 0, 0))
    y_f, y_b, states = pl.pallas_call(
        functools.partial(_ssd_kernel, nc=nc),
        grid=(nb, nc),
        in_specs=[pl.BlockSpec((c, SSD_CONV_DIM), lambda b, j: (b * nc + j, 0)),
                  pl.BlockSpec((c, SSD_CONV_DIM), lambda b, j: (b * nc + nc - 1 - j, 0)),
                  pl.BlockSpec((c, LANES), lambda b, j: (r0 + b * nc + j, 0)),
                  pl.BlockSpec((c, LANES), lambda b, j: (r0 + b * nc + nc - 1 - j, 0)),
                  pl.BlockSpec((2, SSD_HEADS), lambda b, j: (0, 0)),
                  pl.BlockSpec((2, SSD_HEADS), lambda b, j: (0, 0)),
                  state_spec],
        out_specs=[pl.BlockSpec((c, SSD_DIM), lambda b, j: (b * nc + j, 0)),
                   pl.BlockSpec((c, SSD_DIM), lambda b, j: (b * nc + nc - 1 - j, 0)),
                   state_spec],
        out_shape=[jax.ShapeDtypeStruct((nb * length, SSD_DIM), jnp.float32),
                   jax.ShapeDtypeStruct((nb * length, SSD_DIM), jnp.float32),
                   jax.ShapeDtypeStruct(pair_state, jnp.float32)],
        scratch_shapes=[pltpu.VMEM(pair_state[1:], jnp.float32)],
        compiler_params=_params("arbitrary", "arbitrary"),
        name="ssd_scan",
    )(xbc, xbc, dt_logits, dt_logits, dt_bias, a_neg, init.reshape(pair_state))
    return y_f, y_b, states.reshape(nb, 2, SSD_HEADS, SSD_HEAD_DIM, SSD_STATE)


def _ssd_finish_kernel(yf_ref, yb_ref, x_ref, z_ref, d_ref, g_ref, o_ref):
    z = z_ref[...]
    y = (yf_ref[...] + yb_ref[...] + d_ref[...] * x_ref[...]) * (z * jax.nn.sigmoid(z))
    o_ref[...] = (y * lax.rsqrt(jnp.mean(y * y, axis=-1, keepdims=True) + NORM_EPS) * g_ref[...]).astype(o_ref.dtype)


def _ssd_finish(y_f, y_b, xbc, proj, row0, d_row, gain):
    n = y_f.shape[0]
    t = FINISH_ROWS
    r0 = row0 // t
    row_spec = pl.BlockSpec((t, SSD_DIM), lambda i: (i, 0))
    vec_spec = pl.BlockSpec((1, SSD_DIM), lambda i: (0, 0))
    return pl.pallas_call(
        _ssd_finish_kernel,
        grid=(n // t,),
        in_specs=[row_spec, row_spec, row_spec,
                  pl.BlockSpec((t, SSD_DIM), lambda i: (r0 + i, OFF_Z // SSD_DIM)),
                  vec_spec, vec_spec],
        out_specs=row_spec,
        out_shape=jax.ShapeDtypeStruct((n, SSD_DIM), jnp.bfloat16),
        compiler_params=_params("arbitrary"),
        name="ssd_finish",
    )(y_f, y_b, xbc, proj, d_row, gain)


def _hgrn_chunk(q, g, kk, v, state, rev, tri, lane_mod, diag, level_masks, b_ref, kk_ref):
    bf16, f32 = jnp.bfloat16, jnp.float32
    c = q.shape[0]
    ng = c // SUBLANES
    b = _tri_cumsum(tri, g)
    yield
    b_ref[...] = b
    kk_ref[...] = kk
    row = lambda a_ref, r: a_ref[r:r + 1, :]
    grp = lambda a, i: a[i * SUBLANES:(i + 1) * SUBLANES, :]
    b_end = row(b_ref, 0) if rev else row(b_ref, c - 1)

    ones = jnp.ones((HG_KDIM, c), bf16)
    diag_rows = []
    for i0 in range(0, ng, DIAG_BATCH):
        tiles = []
        for i in range(i0, i0 + DIAG_BATCH):
            qg, bg = grp(q, i), grp(b, i)
            tiles += [qg * jnp.exp(bg - row(b_ref, i * SUBLANES + j)) * row(kk_ref, i * SUBLANES + j)
                      for j in range(SUBLANES)]
        sums = jnp.dot(jnp.concatenate(tiles, axis=0).astype(bf16), ones, preferred_element_type=f32)
        for n in range(DIAG_BATCH):
            base = n * SUBLANES * SUBLANES
            acc = sums[base:base + SUBLANES, :]
            for j in range(1, SUBLANES):
                acc = jnp.where(lane_mod[j], sums[base + j * SUBLANES:base + (j + 1) * SUBLANES, :], acc)
            diag_rows.append(acc)
        yield
    att = jnp.where(diag, jnp.concatenate(diag_rows, axis=0), 0.0)

    for m, mask in zip(HG_LEVELS, level_masks):
        half = m // 2
        q_side, k_side = [], []
        for i in range(ng):
            start = (i * SUBLANES) // m * m
            later = (i * SUBLANES) % m >= half
            ref = row(b_ref, start + half if rev else start + half - 1)
            if later != rev:
                q_side.append(grp(q, i) * jnp.exp(grp(b, i) - ref))
                k_side.append(jnp.zeros((SUBLANES, HG_KDIM), f32))
            else:
                q_side.append(jnp.zeros((SUBLANES, HG_KDIM), f32))
                k_side.append(grp(kk, i) * jnp.exp(ref - grp(b, i)))
        a_m = lax.dot_general(jnp.concatenate(q_side, axis=0).astype(bf16),
                              jnp.concatenate(k_side, axis=0).astype(bf16), _NT, preferred_element_type=f32)
        att = att + (a_m if m == c else jnp.where(mask, a_m, 0.0))
        yield

    q_in = (q * jnp.exp(b)).astype(bf16)
    k_out = (kk * jnp.exp(b_end - b)).astype(bf16)
    yield
    s_t = state[0]
    o = (lax.dot_general(q_in, s_t.astype(bf16), _NT, preferred_element_type=f32)
         + jnp.dot(att.astype(bf16), v.astype(bf16), preferred_element_type=f32))
    state[0] = s_t * jnp.exp(b_end) + lax.dot_general(v.astype(bf16), k_out, _TN, preferred_element_type=f32)
    return o


def _in_lockstep(gens):
    results = [None] * len(gens)
    live = list(range(len(gens)))
    while live:
        for i in list(live):
            try:
                next(gens[i])
            except StopIteration as stop:
                results[i] = stop.value
                live.remove(i)
    return results


def _hgrn_kernel(q_ref, ff_ref, fb_ref, v_ref, gate_ref, lb_ref, gn_ref, init_ref, o_ref, st_ref,
                 acc_ref, s_ref, b_ref, kk_ref, *, nc):
    c = HG_C
    t_ids = lax.broadcasted_iota(jnp.int32, (c, c), 0)
    s_ids = lax.broadcasted_iota(jnp.int32, (c, c), 1)
    causal = (s_ids <= t_ids, s_ids >= t_ids)
    tri = tuple(m.astype(jnp.bfloat16) for m in causal)
    same_group = (t_ids // SUBLANES) == (s_ids // SUBLANES)
    diag = tuple(m & same_group for m in causal)
    lane_mod = [(s_ids[:SUBLANES] % SUBLANES) == j for j in range(SUBLANES)]

    def level_mask(m, rev):
        same = (t_ids // m) == (s_ids // m)
        t_late = (t_ids % m) >= m // 2
        s_late = (s_ids % m) >= m // 2
        return same & (t_late != s_late) & (t_late != rev)
    masks = tuple([level_mask(m, rev) for m in HG_LEVELS] for rev in (False, True))

    heads = init_ref.shape[1]
    for a in range(heads):
        for d in (0, 1):
            s_ref[2 * a + d] = init_ref[d, a].T

    per_step = HG_LOCKSTEP // (2 * heads)
    if (nc // 2) % per_step:
        per_step = 1
    half_steps = nc // 2 // per_step

    def run_step(j):
        states = [[s_ref[i]] for i in range(2 * heads)]
        where, gens = [], []
        for u in range(per_step):
            for a in range(heads):
                lanes = slice(a * HG_KDIM, (a + 1) * HG_KDIM)
                for d in (0, 1):
                    chunk = j * per_step + u if d == 0 else nc - 1 - (j * per_step + u)
                    r = pl.ds(pl.multiple_of(chunk * c, c), c)
                    x = q_ref[r, lanes]
                    q = x * jax.nn.sigmoid(x)
                    lb = lb_ref[d:d + 1, lanes]
                    f = lb + (1.0 - lb) * jax.nn.sigmoid((ff_ref, fb_ref)[d][r, lanes])
                    slot = (u * heads + a) * 2 + d
                    where.append((r, lanes))
                    gens.append(_hgrn_chunk(q, jnp.log(f), 1.0 - f, v_ref[r, lanes], states[2 * a + d], d == 1,
                                            tri[d], lane_mod, diag[d], masks[d], b_ref.at[slot], kk_ref.at[slot]))
        outs = _in_lockstep(gens)
        for i in range(2 * heads):
            s_ref[i] = states[i][0]
        return where, outs

    def first_half(j, carry):
        where, outs = run_step(j)
        for (r, lanes), o in zip(where, outs):
            acc_ref[r, lanes] = o
        return carry

    def second_half(j, carry):
        where, outs = run_step(j)
        for (r, lanes), o in zip(where, outs):
            o = o + acc_ref[r, lanes]
            y = o * lax.rsqrt(jnp.mean(o * o, axis=-1, keepdims=True) + NORM_EPS) * gn_ref[:, lanes]
            gate = gate_ref[r, lanes]
            o_ref[r, lanes] = (y * (gate * jax.nn.sigmoid(gate))).astype(o_ref.dtype)
        return carry

    lax.fori_loop(0, half_steps, first_half, 0)
    lax.fori_loop(half_steps, 2 * half_steps, second_half, 0)
    for a in range(heads):
        for d in (0, 1):
            st_ref[d, a] = s_ref[2 * a + d].T


def _hgrn_mixer(proj, row0, nb, length, lb, gain, init, heads):
    nc = length // HG_C
    assert nc % 2 == 0
    width = heads * HG_KDIM
    seq = lambda off: pl.BlockSpec((length, width), lambda b, h, col=off // width: (row0 + b, col + h))
    state_spec = pl.BlockSpec((None, 2, heads, HG_KDIM, HG_VDIM), lambda b, h: (b, 0, h, 0, 0))
    return pl.pallas_call(
        functools.partial(_hgrn_kernel, nc=nc),
        grid=(nb, HG_HEADS // heads),
        in_specs=[seq(OFF_HQ), seq(OFF_HF), seq(OFF_HF + HG_FDIM), seq(OFF_HI), seq(OFF_HG),
                  pl.BlockSpec((2, width), lambda b, h: (0, h)),
                  pl.BlockSpec((1, width), lambda b, h: (0, h)),
                  state_spec],
        out_specs=[pl.BlockSpec((length, width), lambda b, h: (b, h)), state_spec],
        out_shape=[jax.ShapeDtypeStruct((nb * length, HG_DIM), jnp.bfloat16),
                   jax.ShapeDtypeStruct((nb, 2, HG_HEADS, HG_KDIM, HG_VDIM), jnp.float32)],
        scratch_shapes=[pltpu.VMEM((length, width), jnp.float32),
                        pltpu.VMEM((2 * heads, HG_VDIM, HG_KDIM), jnp.float32),
                        pltpu.VMEM((HG_LOCKSTEP, HG_C, HG_KDIM), jnp.float32),
                        pltpu.VMEM((HG_LOCKSTEP, HG_C, HG_KDIM), jnp.float32)],
        compiler_params=_params("arbitrary", "arbitrary"),
        name="hgrn_mixer",
    )(proj, proj, proj, proj, proj, lb, gain, init)


def _softmax_av(scores, values):
    f32, bf16 = jnp.float32, jnp.bfloat16
    m = functools.reduce(jnp.maximum, [jnp.max(s, axis=-1, keepdims=True) for s in scores])
    ps = [jnp.exp(s - m) for s in scores]
    denom = functools.reduce(jnp.add, [jnp.sum(p, axis=-1, keepdims=True) for p in ps])
    acc = functools.reduce(jnp.add, [jnp.dot(p.astype(bf16), v, preferred_element_type=f32)
                                     for p, v in zip(ps, values)])
    return acc / denom


def _ctx_attn_kernel(q_ref, k_ref, v_ref, nk_in_ref, nv_in_ref, o_ref, nk_ref, nv_ref):
    bf16 = jnp.bfloat16
    scale = ATT_HEAD_DIM ** -0.5
    k = k_ref[...]
    v = v_ref[...]
    nk_ref[...] = k
    nv_ref[...] = v
    s = lax.dot_general(q_ref[...].astype(bf16), k.astype(bf16), _NT, preferred_element_type=jnp.float32) * scale
    o_ref[...] = _softmax_av([s], [v.astype(bf16)]).astype(o_ref.dtype)


def _context_attention(proj, nb, length, new_k, new_v, layer):
    spec = lambda off: pl.BlockSpec((length, ATT_HEAD_DIM), lambda b, h, col=off // ATT_HEAD_DIM: (b, col + h))
    cache_spec = pl.BlockSpec((None, None, length, ATT_HEAD_DIM), lambda b, h: (b, layer, 0, h))
    any_spec = pl.BlockSpec(memory_space=pl.ANY)
    return pl.pallas_call(
        _ctx_attn_kernel,
        grid=(nb, ATT_HEADS),
        in_specs=[spec(OFF_AQ), spec(OFF_AK), spec(OFF_AV), any_spec, any_spec],
        out_specs=[pl.BlockSpec((length, ATT_HEAD_DIM), lambda b, h: (b, h)), cache_spec, cache_spec],
        out_shape=[jax.ShapeDtypeStruct((nb * length, ATT_DIM), jnp.bfloat16),
                   jax.ShapeDtypeStruct(new_k.shape, new_k.dtype),
                   jax.ShapeDtypeStruct(new_v.shape, new_v.dtype)],
        input_output_aliases={3: 1, 4: 2},
        compiler_params=_params("arbitrary", "arbitrary"),
        name="context_attention",
    )(proj, proj, proj, new_k, new_v)


def _window_bias(rpb):
    col = jnp.arange(GRID_W)
    cs = jnp.clip(col - WIN_COLS // 2, 0, GRID_W - WIN_COLS)
    col_mask = (col[None, :] >= cs[:, None]) & (col[None, :] < cs[:, None] + WIN_COLS)
    dc_idx = jnp.clip(col[None, :] - col[:, None] + WIN_COLS - 1, 0, 2 * WIN_COLS - 2)
    bias = jnp.where(col_mask, rpb[:, :, dc_idx].astype(jnp.float32), MASKED)
    wins = [bias[:, d0:d0 + WIN_ROWS].transpose(0, 2, 1, 3).reshape(rpb.shape[0], GRID_W, WIN_ROWS * GRID_W)
            for d0 in range(WIN_ROWS)]
    return jnp.stack(wins, axis=1)


def _natten_kernel(q_ref, k_ref, v_ref, kc_ref, vc_ref, bias_ref, o_ref, kb_ref, vb_ref, *, rows):
    bf16, f32 = jnp.bfloat16, jnp.float32
    scale = ATT_HEAD_DIM ** -0.5
    win = WIN_ROWS * GRID_W

    def cast(i, carry):
        sl = pl.ds(pl.multiple_of(i * CAST_ROWS, CAST_ROWS), CAST_ROWS)
        kb_ref[sl, :] = k_ref[sl, :].astype(bf16)
        vb_ref[sl, :] = v_ref[sl, :].astype(bf16)
        return carry
    lax.fori_loop(0, rows * GRID_W // CAST_ROWS, cast, 0)

    kc = kc_ref[...].astype(bf16)
    vc = vc_ref[...].astype(bf16)

    def row_block(r):
        rs = jnp.clip(r - WIN_ROWS // 2, 0, rows - WIN_ROWS)
        d0 = rs - r + WIN_ROWS - 1
        q = q_ref[pl.ds(pl.multiple_of(r * GRID_W, GRID_W), GRID_W), :].astype(bf16)
        keys = pl.ds(pl.multiple_of(rs * GRID_W, GRID_W), win)
        s_lat = lax.dot_general(q, kb_ref[keys, :], _NT, preferred_element_type=f32) * scale + bias_ref[d0]
        s_ctx = lax.dot_general(q, kc, _NT, preferred_element_type=f32) * scale
        yield
        m = jnp.maximum(jnp.max(s_lat, axis=-1, keepdims=True), jnp.max(s_ctx, axis=-1, keepdims=True))
        yield
        p_lat = jnp.exp(s_lat - m)
        p_ctx = jnp.exp(s_ctx - m)
        denom = jnp.sum(p_lat, axis=-1, keepdims=True) + jnp.sum(p_ctx, axis=-1, keepdims=True)
        acc = (jnp.dot(p_lat.astype(bf16), vb_ref[keys, :], preferred_element_type=f32)
               + jnp.dot(p_ctx.astype(bf16), vc, preferred_element_type=f32))
        yield
        o_ref[pl.ds(pl.multiple_of(r * GRID_W, GRID_W), GRID_W), :] = (acc / denom).astype(o_ref.dtype)

    def row_group(g, carry):
        _in_lockstep([row_block(g * ROW_UNROLL + u) for u in range(ROW_UNROLL)])
        return carry
    lax.fori_loop(0, rows // ROW_UNROLL, row_group, 0)


def _neighbourhood_attention(proj, row0, nb, length, cache_k, cache_v, layer, bias_win):
    rows = length // GRID_W
    past = cache_k.shape[2]
    spec = lambda off: pl.BlockSpec((length, ATT_HEAD_DIM),
                                    lambda b, h, col=off // ATT_HEAD_DIM: (row0 + b, col + h))
    cache_spec = pl.BlockSpec((None, None, past, ATT_HEAD_DIM), lambda b, h: (b, layer, 0, h))
    return pl.pallas_call(
        functools.partial(_natten_kernel, rows=rows),
        grid=(nb, ATT_HEADS),
        in_specs=[spec(OFF_AQ), spec(OFF_AK), spec(OFF_AV), cache_spec, cache_spec,
                  pl.BlockSpec((None, WIN_ROWS, GRID_W, WIN_ROWS * GRID_W), lambda b, h: (h, 0, 0, 0))],
        out_specs=pl.BlockSpec((length, ATT_HEAD_DIM), lambda b, h: (b, h)),
        out_shape=jax.ShapeDtypeStruct((nb * length, ATT_DIM), jnp.bfloat16),
        scratch_shapes=[pltpu.VMEM((length, ATT_HEAD_DIM), jnp.bfloat16),
                        pltpu.VMEM((length, ATT_HEAD_DIM), jnp.bfloat16)],
        compiler_params=_params("arbitrary", "arbitrary"),
        name="neighbourhood_attention",
    )(proj, proj, proj, cache_k, cache_v, bias_win)


def _merge_kernel(ca_ref, cb_ref, cc_ref, la_ref, lb_ref, lc_ref, wa_ref, wb_ref, wc_ref,
                  ga_ref, gb_ref, gc_ref, o_ref):
    f32 = jnp.float32
    is_ctx = pl.program_id(0) < N_CTX // ROW_TILE

    def merge(ya_ref, yb_ref, yc_ref):
        acc = jax.nn.sigmoid(ga_ref[...]) * jnp.dot(ya_ref[...], wa_ref[...], preferred_element_type=f32)
        acc = acc + jax.nn.sigmoid(gb_ref[...]) * jnp.dot(yb_ref[...], wb_ref[...], preferred_element_type=f32)
        acc = acc + jax.nn.sigmoid(gc_ref[...]) * jnp.dot(yc_ref[...], wc_ref[...], preferred_element_type=f32)
        o_ref[...] = acc.astype(o_ref.dtype)

    @pl.when(is_ctx)
    def _():
        merge(ca_ref, cb_ref, cc_ref)

    @pl.when(jnp.logical_not(is_ctx))
    def _():
        merge(la_ref, lb_ref, lc_ref)


def _branch_merge(ys_ctx, ys_lat, ws, proj):
    tm, tn = ROW_TILE, MERGE_COLS
    kdim = ws[0].shape[0]
    cg = OFF_GATES // tn
    per = D_MODEL // tn
    ctx_tiles = N_CTX // tm
    ctx_spec = pl.BlockSpec((tm, kdim), lambda i, j: (jnp.minimum(i, ctx_tiles - 1), 0))
    lat_spec = pl.BlockSpec((tm, kdim), lambda i, j: (jnp.maximum(i - ctx_tiles, 0), 0))
    w_spec = pl.BlockSpec((kdim, tn), lambda i, j: (0, j))
    g_spec = lambda b: pl.BlockSpec((tm, tn), lambda i, j, b=b: (i, cg + b * per + j))
    return pl.pallas_call(
        _merge_kernel,
        grid=(N_TOK // tm, D_MODEL // tn),
        in_specs=[ctx_spec] * N_BRANCH + [lat_spec] * N_BRANCH + [w_spec] * N_BRANCH
                 + [g_spec(b) for b in range(N_BRANCH)],
        out_specs=pl.BlockSpec((tm, tn), lambda i, j: (i, j)),
        out_shape=jax.ShapeDtypeStruct((N_TOK, D_MODEL), jnp.bfloat16),
        compiler_params=_params("arbitrary", "arbitrary"),
        name="branch_merge",
    )(*ys_ctx, *ys_lat, *ws, proj, proj, proj)


def _out_residual_kernel(m_ref, w_ref, x_ref, gate_ref, o_ref):
    o_ref[...] = x_ref[...] + gate_ref[...] * jnp.dot(m_ref[...], w_ref[...], preferred_element_type=jnp.float32)


def _out_residual(merged, w, x, gate):
    tm, tn = ROW_TILE, COL_TILE
    return pl.pallas_call(
        _out_residual_kernel,
        grid=(N_TOK // tm, D_MODEL // tn),
        in_specs=[pl.BlockSpec((tm, D_MODEL), lambda i, j: (i, 0)),
                  pl.BlockSpec((D_MODEL, tn), lambda i, j: (0, j)),
                  pl.BlockSpec((tm, tn), lambda i, j: (i, j)),
                  pl.BlockSpec((None, 1, tn), lambda i, j: (_mod_row(i), 0, j))],
        out_specs=pl.BlockSpec((tm, tn), lambda i, j: (i, j)),
        out_shape=jax.ShapeDtypeStruct((N_TOK, D_MODEL), jnp.float32),
        compiler_params=_params("arbitrary", "arbitrary"),
        name="out_residual",
    )(merged, w, x, gate)


def _first_max(vals):
    best, idx = vals[0], jnp.zeros(vals[0].shape, jnp.int32)
    for k in range(1, len(vals)):
        better = vals[k] > best
        best = jnp.where(better, vals[k], best)
        idx = jnp.where(better, k, idx)
    return best, idx


def _pick(idx, vals):
    out = vals[0]
    for k in range(1, len(vals)):
        out = jnp.where(idx == k, vals[k], out)
    return out


def _norm_router_kernel(x_ref, g_ref, sc_ref, sh_ref, wr_hi_ref, wr_lo_ref, rb_ref,
                        h_ref, ids_ref, wts_ref, cnt_ref, lo_ref, base_ref):
    f32, bf16 = jnp.float32, jnp.bfloat16
    t = x_ref.shape[0]

    @pl.when(pl.program_id(0) == 0)
    def _():
        base_ref[...] = jnp.zeros_like(base_ref)

    def store(rows, h):
        h_hi = h.astype(bf16)
        h_ref[rows, :] = h_hi
        lo_ref[rows, :] = (h - h_hi.astype(f32)).astype(bf16)
    _modulated_norm(x_ref, g_ref, sc_ref, sh_ref, store)

    logits = (jnp.dot(h_ref[...], wr_hi_ref[...], preferred_element_type=f32)
              + (jnp.dot(h_ref[...], wr_lo_ref[...], preferred_element_type=f32)
                 + jnp.dot(lo_ref[...], wr_hi_ref[...], preferred_element_type=f32)))
    logits = logits.T[:N_EXPERTS]
    scores = jax.nn.sigmoid(logits)
    sel = scores + rb_ref[...]
    row = lambda a, e: a[e:e + 1, :]

    group_scores = []
    for g in range(N_EXPERT_GROUPS):
        v = [row(sel, g * EXPERTS_PER_GROUP + k) for k in range(EXPERTS_PER_GROUP)]
        pair_sums = [v[a] + v[b] for a in range(EXPERTS_PER_GROUP) for b in range(a + 1, EXPERTS_PER_GROUP)]
        group_scores.append(functools.reduce(jnp.maximum, pair_sums))
    _, grp = _first_max(group_scores)

    in_sel = [_pick(grp, [row(sel, g * EXPERTS_PER_GROUP + k) for g in range(N_EXPERT_GROUPS)])
              for k in range(EXPERTS_PER_GROUP)]
    in_score = [_pick(grp, [row(scores, g * EXPERTS_PER_GROUP + k) for g in range(N_EXPERT_GROUPS)])
                for k in range(EXPERTS_PER_GROUP)]
    _, i1 = _first_max(in_sel)
    _, i2 = _first_max([jnp.where(i1 == k, -jnp.inf, in_sel[k]) for k in range(EXPERTS_PER_GROUP)])
    s1 = _pick(i1, in_score)
    s2 = _pick(i2, in_score)
    e1 = grp * EXPERTS_PER_GROUP + i1
    e2 = grp * EXPERTS_PER_GROUP + i2

    e_ids = lax.broadcasted_iota(jnp.int32, (N_EXPERTS, t), 0)
    hit1 = e_ids == e1
    hit2 = e_ids == e2
    cnt = jnp.where(hit1 | hit2, 1.0, 0.0).astype(bf16)
    before = (lax.broadcasted_iota(jnp.int32, (t, t), 0) < lax.broadcasted_iota(jnp.int32, (t, t), 1)).astype(bf16)
    prefix = jnp.dot(cnt, before, preferred_element_type=f32) + base_ref[:, 0:1]
    rank1 = jnp.sum(jnp.where(hit1, prefix, 0.0), axis=0, keepdims=True)
    rank2 = jnp.sum(jnp.where(hit2, prefix, 0.0), axis=0, keepdims=True)
    base_ref[...] = base_ref[...] + jnp.dot(cnt, jnp.ones((t, LANES), bf16), preferred_element_type=f32)
    cnt_ref[...] = base_ref[...]

    zeros = jnp.zeros((SUBLANES - 4, t), jnp.int32)
    ids_ref[...] = jnp.concatenate([e1, e2, rank1.astype(jnp.int32), rank2.astype(jnp.int32), zeros], axis=0)
    total = s1 + s2
    wts_ref[...] = jnp.concatenate([s1 / total, s2 / total, jnp.zeros((SUBLANES - 2, t), f32)], axis=0)


def _norm_router(x, gain, scale, shift, wr_hi, wr_lo, router_bias):
    t = ROUTER_TILE
    per = ROW_TILE // t
    mod_spec = pl.BlockSpec((None, 1, D_MODEL), lambda i: (_mod_row(i // per), 0, 0))
    return pl.pallas_call(
        _norm_router_kernel,
        grid=(N_TOK // t,),
        in_specs=[pl.BlockSpec((t, D_MODEL), lambda i: (i, 0)),
                  pl.BlockSpec((1, D_MODEL), lambda i: (0, 0)),
                  mod_spec, mod_spec,
                  pl.BlockSpec((D_MODEL, LANES), lambda i: (0, 0)),
                  pl.BlockSpec((D_MODEL, LANES), lambda i: (0, 0)),
                  pl.BlockSpec((N_EXPERTS, 1), lambda i: (0, 0))],
        out_specs=[pl.BlockSpec((t, D_MODEL), lambda i: (i, 0)),
                   pl.BlockSpec((SUBLANES, t), lambda i: (0, i)),
                   pl.BlockSpec((SUBLANES, t), lambda i: (0, i)),
                   pl.BlockSpec((N_EXPERTS, LANES), lambda i: (0, 0))],
        out_shape=[jax.ShapeDtypeStruct((N_TOK, D_MODEL), jnp.bfloat16),
                   jax.ShapeDtypeStruct((SUBLANES, N_TOK), jnp.int32),
                   jax.ShapeDtypeStruct((SUBLANES, N_TOK), jnp.float32),
                   jax.ShapeDtypeStruct((N_EXPERTS, LANES), jnp.float32)],
        scratch_shapes=[pltpu.VMEM((t, D_MODEL), jnp.bfloat16),
                        pltpu.VMEM((N_EXPERTS, LANES), jnp.float32)],
        compiler_params=_params("arbitrary"),
        name="norm_router",
    )(x, gain, scale, shift, wr_hi, wr_lo, router_bias)


def _expert_kernel(be_ref, na_ref, x_ref, w1_ref, w3_ref, w2_ref, o_ref):
    active = pl.program_id(0) < na_ref[0]

    @pl.when(active)
    def _():
        bf16 = jnp.bfloat16
        x = x_ref[...]
        a = jnp.dot(x, w1_ref[...].astype(bf16), preferred_element_type=jnp.float32)
        b = jnp.dot(x, w3_ref[...].astype(bf16), preferred_element_type=jnp.float32)
        hdn = (a * jax.nn.sigmoid(a)) * b
        o_ref[...] = jnp.dot(hdn.astype(bf16), w2_ref[...].astype(bf16),
                             preferred_element_type=jnp.float32).astype(o_ref.dtype)

    @pl.when(jnp.logical_not(active))
    def _():
        o_ref[...] = jnp.zeros_like(o_ref)


def _expert_blocks(buf, block_e, n_active, w1, w3, w2):
    n_blocks = buf.shape[0] // MOE_BLOCK
    w_spec = lambda r, c: pl.BlockSpec((None, r, c), lambda i, be, na: (be[i], 0, 0), pipeline_mode=pl.Buffered(1))
    grid_spec = pltpu.PrefetchScalarGridSpec(
        num_scalar_prefetch=2,
        grid=(n_blocks,),
        in_specs=[pl.BlockSpec((MOE_BLOCK, D_MODEL), lambda i, be, na: (i, 0)),
                  w_spec(D_MODEL, D_EXPERT), w_spec(D_MODEL, D_EXPERT), w_spec(D_EXPERT, D_MODEL)],
        out_specs=pl.BlockSpec((MOE_BLOCK, D_MODEL), lambda i, be, na: (i, 0)),
    )
    return pl.pallas_call(
        _expert_kernel,
        grid_spec=grid_spec,
        out_shape=jax.ShapeDtypeStruct((n_blocks * MOE_BLOCK, D_MODEL), jnp.float32),
        compiler_params=_params("arbitrary"),
        name="moe_experts",
    )(block_e, n_active, buf, w1, w3, w2)


def _row_copy(out_hbm, buf_ref, sem_ref, slot, k, t, row):
    return pltpu.make_async_copy(out_hbm.at[pl.ds(row, 1), :], buf_ref.at[slot, k, pl.ds(t, 1), :], sem_ref.at[slot])


def _combine_kernel(dcur_ref, dnext_ref, w_ref, x_ref, gate_ref, out_hbm, o_ref, buf_ref, sem_ref):
    t_rows = x_ref.shape[0]
    i = pl.program_id(0)
    n = pl.num_programs(0)
    slot = i % 2

    def issue(d_ref, s):
        for t in range(t_rows):
            for k in range(TOP_K):
                _row_copy(out_hbm, buf_ref, sem_ref, s, k, t, d_ref[k, t]).start()

    @pl.when(i == 0)
    def _():
        issue(dcur_ref, 0)

    @pl.when(i + 1 < n)
    def _():
        issue(dnext_ref, 1 - slot)

    def wait(t, carry):
        for k in range(TOP_K):
            _row_copy(out_hbm, buf_ref, sem_ref, slot, k, t, 0).wait()
        return carry
    lax.fori_loop(0, t_rows, wait, 0, unroll=COMBINE_UNROLL)

    w = w_ref[...]
    moe = w[:, 0:1] * buf_ref[slot, 0] + w[:, 1:2] * buf_ref[slot, 1]
    o_ref[...] = x_ref[...] + gate_ref[...] * moe


def _moe_combine(x, gate, out, dest, wts_t):
    t = COMBINE_ROWS
    steps = N_TOK // t
    per = ROW_TILE // t
    smem_spec = lambda shift: pl.BlockSpec((SUBLANES, t), lambda i: (0, jnp.minimum(i + shift, steps - 1)),
                                           memory_space=pltpu.SMEM)
    return pl.pallas_call(
        _combine_kernel,
        grid=(steps,),
        in_specs=[smem_spec(0), smem_spec(1),
                  pl.BlockSpec((t, LANES), lambda i: (i, 0)),
                  pl.BlockSpec((t, D_MODEL), lambda i: (i, 0)),
                  pl.BlockSpec((None, 1, D_MODEL), lambda i: (_mod_row(i // per), 0, 0)),
                  pl.BlockSpec(memory_space=pl.ANY)],
        out_specs=pl.BlockSpec((t, D_MODEL), lambda i: (i, 0)),
        out_shape=jax.ShapeDtypeStruct((N_TOK, D_MODEL), jnp.float32),
        scratch_shapes=[pltpu.VMEM((2, TOP_K, t, D_MODEL), jnp.float32),
                        pltpu.SemaphoreType.DMA((2,))],
        compiler_params=_params("arbitrary"),
        name="moe_combine",
    )(dest, dest, wts_t, x, gate, out)


def _moe(x, gate, h, ids, wts, counts, layer, w1, w3, w2):
    nk = N_TOK * TOP_K
    n_blocks = (nk + N_EXPERTS * (MOE_BLOCK - 1)) // MOE_BLOCK
    counts = counts[:, 0].astype(jnp.int32)
    padded = (counts + MOE_BLOCK - 1) // MOE_BLOCK * MOE_BLOCK
    pad_end = jnp.cumsum(padded)
    pad_start = pad_end - padded
    dest1 = pad_start[ids[0]] + ids[2]
    dest2 = pad_start[ids[1]] + ids[3]
    tok = jnp.arange(N_TOK, dtype=jnp.int32)
    src = jnp.zeros((n_blocks * MOE_BLOCK,), jnp.int32).at[jnp.concatenate([dest1, dest2])].set(
        jnp.concatenate([tok, tok]))
    block_e = jnp.minimum(jnp.searchsorted(pad_end, jnp.arange(n_blocks, dtype=jnp.int32) * MOE_BLOCK, side='right'),
                          N_EXPERTS - 1).astype(jnp.int32)
    n_active = (pad_end[-1:] // MOE_BLOCK).astype(jnp.int32)
    out = _expert_blocks(h[src], block_e + layer * N_EXPERTS, n_active, w1, w3, w2)
    dest = jnp.concatenate([dest1[None, :], dest2[None, :], jnp.zeros((SUBLANES - TOP_K, N_TOK), jnp.int32)], axis=0)
    wts_t = jnp.pad(wts[:TOP_K].T, ((0, 0), (0, LANES - TOP_K)))
    return _moe_combine(x, gate, out, dest, wts_t)


def _final_norm_kernel(x_ref, g_ref, ctx_ref, lat_ref):
    g = g_ref[...]

    def norm_into(o_ref):
        def body(r, carry):
            rows = pl.ds(pl.multiple_of(r * NORM_ROWS, NORM_ROWS), NORM_ROWS)
            x = x_ref[rows, :]
            o_ref[rows, :] = (x * lax.rsqrt(jnp.mean(x * x, axis=-1, keepdims=True) + NORM_EPS)) * g
            return carry
        lax.fori_loop(0, x_ref.shape[0] // NORM_ROWS, body, 0)

    is_ctx = pl.program_id(0) < N_CTX // ROW_TILE

    @pl.when(is_ctx)
    def _():
        norm_into(ctx_ref)

    @pl.when(jnp.logical_not(is_ctx))
    def _():
        norm_into(lat_ref)


def _final_norm(x, gain):
    ctx_tiles = N_CTX // ROW_TILE
    return pl.pallas_call(
        _final_norm_kernel,
        grid=(N_TOK // ROW_TILE,),
        in_specs=[pl.BlockSpec((ROW_TILE, D_MODEL), lambda i: (i, 0)),
                  pl.BlockSpec((1, D_MODEL), lambda i: (0, 0))],
        out_specs=[pl.BlockSpec((ROW_TILE, D_MODEL), lambda i: (jnp.minimum(i, ctx_tiles - 1), 0)),
                   pl.BlockSpec((ROW_TILE, D_MODEL), lambda i: (jnp.maximum(i - ctx_tiles, 0), 0))],
        out_shape=[jax.ShapeDtypeStruct((N_CTX, D_MODEL), jnp.float32),
                   jax.ShapeDtypeStruct((N_LAT, D_MODEL), jnp.float32)],
        compiler_params=_params("arbitrary"),
        name="final_norm",
    )(x, gain)


def _split_w_in(w):
    c0 = SSD_DIM + SSD_CONV_DIM
    c1 = c0 + 2 * SSD_HEADS
    dt_cols = jnp.pad(w[:, c0:c1], ((0, 0), (0, LANES - 2 * SSD_HEADS)))
    return (jnp.concatenate([w[:, :c0], w[:, c1:]], axis=1).astype(jnp.bfloat16), dt_cols.astype(jnp.bfloat16))


def kernel(x_prompt, x_sample, cache_k, cache_v, state_ssd, state_hgrn, c, c_ctx, w_ada, b_ada, norm_mix, norm_moe, w_in, ssd_conv_w, ssd_conv_b, ssd_dt_bias, ssd_a_log, ssd_d, ssd_norm, hg_lb_logits, hg_norm, att_rpb, w_br_ssd, w_br_hg, w_br_att, w_out, w_router, router_bias, moe_w1, moe_w3, moe_w2, final_norm):
    bf16, f32 = jnp.bfloat16, jnp.float32
    lb_cum = jnp.cumsum(jax.nn.softmax(hg_lb_logits.astype(f32), axis=1), axis=1)
    lower_bounds = lb_cum - lb_cum[:, :1]

    x = jnp.concatenate([x_prompt.reshape(N_CTX, D_MODEL), x_sample.reshape(N_LAT, D_MODEL)], axis=0)

    n_mod = 1 + DEC_BATCH
    cond = jnp.concatenate([c_ctx[None, :], c], axis=0)
    cond = jnp.pad(jax.nn.silu(cond), ((0, 2 * SUBLANES - n_mod), (0, 0)))

    wr = jnp.pad(w_router, ((0, 0), (0, LANES - N_EXPERTS)))
    wr_hi = wr.astype(bf16)
    wr_lo = (wr - wr_hi.astype(f32)).astype(bf16)
    cache_k = cache_k.reshape(DEC_BATCH, DEPTH, -1, ATT_DIM)
    cache_v = cache_v.reshape(DEC_BATCH, DEPTH, -1, ATT_DIM)
    lat_row0 = N_CTX // DEC_SEQ
    zero_ssd = jnp.zeros((BATCH, 2, SSD_HEADS, SSD_HEAD_DIM, SSD_STATE), f32)
    zero_hg = jnp.zeros((BATCH, 2, HG_HEADS, HG_KDIM, HG_VDIM), f32)

    expert_w = tuple(w.reshape((DEPTH * N_EXPERTS,) + w.shape[2:]) for w in (moe_w1, moe_w3, moe_w2))

    new_k = jnp.zeros((BATCH, DEPTH, SEQ, ATT_DIM), f32)
    new_v = jnp.zeros((BATCH, DEPTH, SEQ, ATT_DIM), f32)
    new_ssd, new_hg = [], []
    for l in range(DEPTH):
        mod = _matmul(cond, w_ada, l, 2 * SUBLANES, COL_TILE)[:n_mod] + b_ada[l]
        mod = mod.reshape(n_mod, 6, 1, D_MODEL)
        shift_m, scale_m, gate_m, shift_f, scale_f, gate_f = (mod[:, i] for i in range(6))

        proj, dt_logits = _norm_matmul(x, norm_mix[l][None, :], scale_m, shift_m, *_split_w_in(w_in[l]))

        conv_b = ssd_conv_b[l][None, :]
        a_neg = -jnp.exp(ssd_a_log[l].astype(f32))
        d_row = jnp.repeat(ssd_d[l], SSD_HEAD_DIM)[None, :]
        ssd_gain = ssd_norm[l][None, :]
        y_ssd = []
        for row0, nb, length, init in ((0, BATCH, SEQ, zero_ssd), (N_CTX, DEC_BATCH, DEC_SEQ, state_ssd[:, l])):
            xbc = _ssd_conv(proj, row0, nb * length, length, ssd_conv_w[l], conv_b)
            y_f, y_b, states = _ssd_scan(xbc, dt_logits, row0, nb, length, ssd_dt_bias[l], a_neg, init)
            y_ssd.append(_ssd_finish(y_f, y_b, xbc, proj, row0, d_row, ssd_gain))
            if row0 == 0:
                new_ssd.append(states)

        lb = lower_bounds[:, l]
        hg_gain = hg_norm[l].reshape(1, HG_DIM)
        y_hg_ctx, states = _hgrn_mixer(proj, 0, BATCH, SEQ, lb, hg_gain, zero_hg, HG_CTX_HEADS)
        new_hg.append(states)
        y_hg_lat, _ = _hgrn_mixer(proj, lat_row0, DEC_BATCH, DEC_SEQ, lb, hg_gain, state_hgrn[:, l], 1)

        y_att_ctx, new_k, new_v = _context_attention(proj, BATCH, SEQ, new_k, new_v, l)
        y_att_lat = _neighbourhood_attention(proj, lat_row0, DEC_BATCH, DEC_SEQ, cache_k, cache_v, l,
                                             _window_bias(att_rpb[l]))

        merged = _branch_merge((y_ssd[0], y_hg_ctx, y_att_ctx), (y_ssd[1], y_hg_lat, y_att_lat),
                               (w_br_ssd[l].astype(bf16), w_br_hg[l].astype(bf16), w_br_att[l].astype(bf16)), proj)
        x = _out_residual(merged, w_out[l].astype(bf16), x, gate_m)

        h2, ids, wts, counts = _norm_router(x, norm_moe[l][None, :], scale_f, shift_f, wr_hi, wr_lo,
                                            router_bias.astype(f32)[:, None])
        x = _moe(x, gate_f, h2, ids, wts, counts, l, *expert_w)

    y_ctx, y_lat = _final_norm(x, final_norm[None, :])
    y_prompt = y_ctx.reshape(BATCH, SEQ, D_MODEL)
    y_sample = y_lat.reshape(DEC_BATCH, DEC_SEQ, D_MODEL)
    cache_shape = (BATCH, DEPTH, SEQ, ATT_HEADS, ATT_HEAD_DIM)
    return (y_prompt, y_sample, new_k.reshape(cache_shape), new_v.reshape(cache_shape),
            jnp.stack(new_ssd, axis=1), jnp.stack(new_hg, axis=1))
```

```python
import functools

import jax
import jax.numpy as jnp
from jax import lax
from jax.experimental import pallas as pl
from jax.experimental.pallas import tpu as pltpu

D_MODEL = 2048
BATCH = 32
SEQ = 256
DEPTH = 2
DEC_BATCH = 8
DEC_SEQ = 4096
GRID_W = 64
NORM_EPS = 1e-6
SSD_HEADS = 16
SSD_HEAD_DIM = 64
SSD_DIM = SSD_HEADS * SSD_HEAD_DIM
SSD_STATE = 64
SSD_GROUPS = 4
SSD_CONV = 5
SSD_CONV_DIM = SSD_DIM + 2 * SSD_GROUPS * SSD_STATE
HG_HEADS = 8
HG_KDIM = 128
HG_VDIM = 128
HG_FDIM = HG_HEADS * HG_KDIM
HG_DIM = HG_HEADS * HG_VDIM
ATT_HEADS = 8
ATT_HEAD_DIM = 128
ATT_DIM = ATT_HEADS * ATT_HEAD_DIM
WIN_ROWS = 8
WIN_COLS = 16
N_BRANCH = 3
N_EXPERTS = 16
N_EXPERT_GROUPS = 4
EXPERTS_PER_GROUP = N_EXPERTS // N_EXPERT_GROUPS
TOP_K = 2
D_EXPERT = 1024

N_CTX = BATCH * SEQ
N_LAT = DEC_BATCH * DEC_SEQ
N_TOK = N_CTX + N_LAT

VMEM_LIMIT_BYTES = 56 * 1024 * 1024
LANES = 128
SUBLANES = 8

OFF_Z = 0
OFF_XBC = OFF_Z + SSD_DIM
OFF_HQ = OFF_XBC + SSD_CONV_DIM
OFF_HF = OFF_HQ + HG_FDIM
OFF_HI = OFF_HF + 2 * HG_FDIM
OFF_HG = OFF_HI + HG_DIM
OFF_AQ = OFF_HG + HG_DIM
OFF_AK = OFF_AQ + ATT_DIM
OFF_AV = OFF_AK + ATT_DIM
OFF_GATES = OFF_AV + ATT_DIM
PROJ_DIM = OFF_GATES + N_BRANCH * D_MODEL
PROJ_TILE = 1536

ROW_TILE = 1024
COL_TILE = 1024
MERGE_COLS = 512
NORM_ROWS = 64
ROUTER_TILE = 512
MOE_BLOCK = 512
CONV_ROWS = 1024
CONV_SUB = 256
CONV_COLS = 512
SSD_C = 128
FINISH_ROWS = 256
HG_C = 128
HG_LEVELS = (16, 32, 64, 128)
CAST_ROWS = 512
COMBINE_ROWS = 256
COMBINE_UNROLL = 8
DIAG_BATCH = 4
ROW_UNROLL = 8
HG_LOCKSTEP = 8
HG_CTX_HEADS = 2
MASKED = -1e30

_NT = (((1,), (1,)), ((), ()))
_TN = (((0,), (0,)), ((), ()))


def _params(*semantics):
    return pltpu.CompilerParams(dimension_semantics=semantics, vmem_limit_bytes=VMEM_LIMIT_BYTES)


def _mod_row(i):
    ctx_tiles = N_CTX // ROW_TILE
    tiles_per_req = DEC_SEQ // ROW_TILE
    return jnp.where(i < ctx_tiles, 0, 1 + (i - ctx_tiles) // tiles_per_req)


def _split3(x):
    bf16, f32 = jnp.bfloat16, jnp.float32
    x1 = x.astype(bf16)
    r = x - x1.astype(f32)
    x2 = r.astype(bf16)
    x3 = (r - x2.astype(f32)).astype(bf16)
    return x1, x2, x3


def _tri_cumsum(tri, x):
    x1, x2, x3 = _split3(x)
    f32 = jnp.float32
    return (jnp.dot(tri, x1, preferred_element_type=f32)
            + (jnp.dot(tri, x2, preferred_element_type=f32) + jnp.dot(tri, x3, preferred_element_type=f32)))


def _mm_kernel(x_ref, w_ref, o_ref):
    o_ref[...] = jnp.dot(x_ref[...].astype(jnp.bfloat16), w_ref[...].astype(jnp.bfloat16),
                         preferred_element_type=jnp.float32).astype(o_ref.dtype)


def _matmul(x, w, layer, tm, tn):
    m, k = x.shape
    n = w.shape[2]
    return pl.pallas_call(
        _mm_kernel,
        grid=(m // tm, n // tn),
        in_specs=[pl.BlockSpec((tm, k), lambda i, j: (i, 0)),
                  pl.BlockSpec((None, k, tn), lambda i, j: (layer, 0, j))],
        out_specs=pl.BlockSpec((tm, tn), lambda i, j: (i, j)),
        out_shape=jax.ShapeDtypeStruct((m, n), jnp.float32),
        compiler_params=_params("arbitrary", "arbitrary"),
        name="matmul",
    )(x, w)


def _modulated_norm(x_ref, g_ref, sc_ref, sh_ref, store):
    g = g_ref[...]
    sc = 1.0 + sc_ref[...]
    sh = sh_ref[...]

    def body(r, carry):
        rows = pl.ds(pl.multiple_of(r * NORM_ROWS, NORM_ROWS), NORM_ROWS)
        x = x_ref[rows, :]
        y = x * lax.rsqrt(jnp.mean(x * x, axis=-1, keepdims=True) + NORM_EPS)
        store(rows, (y * g) * sc + sh)
        return carry

    lax.fori_loop(0, x_ref.shape[0] // NORM_ROWS, body, 0)


def _norm_mm_kernel(x_ref, g_ref, sc_ref, sh_ref, w_ref, wdt_ref, o_ref, dt_ref, h_ref):
    @pl.when(pl.program_id(1) == 0)
    def _():
        def store(rows, h):
            h_ref[rows, :] = h.astype(jnp.bfloat16)
        _modulated_norm(x_ref, g_ref, sc_ref, sh_ref, store)
        dt_ref[...] = jnp.dot(h_ref[...], wdt_ref[...], preferred_element_type=jnp.float32)

    o_ref[...] = jnp.dot(h_ref[...], w_ref[...], preferred_element_type=jnp.float32)


def _norm_matmul(x, gain, scale, shift, w, w_dt):
    mod_spec = pl.BlockSpec((None, 1, D_MODEL), lambda i, j: (_mod_row(i), 0, 0))
    return pl.pallas_call(
        _norm_mm_kernel,
        grid=(N_TOK // ROW_TILE, PROJ_DIM // PROJ_TILE),
        in_specs=[pl.BlockSpec((ROW_TILE, D_MODEL), lambda i, j: (i, 0)),
                  pl.BlockSpec((1, D_MODEL), lambda i, j: (0, 0)),
                  mod_spec, mod_spec,
                  pl.BlockSpec((D_MODEL, PROJ_TILE), lambda i, j: (0, j)),
                  pl.BlockSpec((D_MODEL, LANES), lambda i, j: (0, 0))],
        out_specs=[pl.BlockSpec((ROW_TILE, PROJ_TILE), lambda i, j: (i, j)),
                   pl.BlockSpec((ROW_TILE, LANES), lambda i, j: (i, 0))],
        out_shape=[jax.ShapeDtypeStruct((N_TOK, PROJ_DIM), jnp.float32),
                   jax.ShapeDtypeStruct((N_TOK, LANES), jnp.float32)],
        scratch_shapes=[pltpu.VMEM((ROW_TILE, D_MODEL), jnp.bfloat16)],
        compiler_params=_params("arbitrary", "arbitrary"),
        name="norm_in_proj",
    )(x, gain, scale, shift, w, w_dt)


def _conv_kernel(prev_ref, x_ref, next_ref, w_ref, b_ref, o_ref, ext_ref, *, tiles_per_seq):
    i = pl.program_id(0)
    t = x_ref.shape[0]
    pad = SSD_CONV // 2
    first = (i % tiles_per_seq) == 0
    last = (i % tiles_per_seq) == tiles_per_seq - 1
    ext_ref[0:SUBLANES, :] = jnp.where(first, 0.0, prev_ref[...])
    ext_ref[SUBLANES:SUBLANES + t, :] = x_ref[...]
    ext_ref[SUBLANES + t:2 * SUBLANES + t, :] = jnp.where(last, 0.0, next_ref[...])
    for r0 in range(0, t, CONV_SUB):
        y = jnp.broadcast_to(b_ref[...], (CONV_SUB, CONV_COLS))
        for j in range(SSD_CONV):
            start = SUBLANES - pad + j + r0
            y = y + ext_ref[start:start + CONV_SUB, :] * w_ref[j:j + 1, :]
        o_ref[r0:r0 + CONV_SUB, :] = y * jax.nn.sigmoid(y)


def _ssd_conv(proj, row0, n_rows, seq_len, w, b):
    t = min(seq_len, CONV_ROWS)
    r0 = row0 // t
    c0 = OFF_XBC // CONV_COLS
    sub = t // SUBLANES
    n_sub = proj.shape[0] // SUBLANES
    return pl.pallas_call(
        functools.partial(_conv_kernel, tiles_per_seq=seq_len // t),
        grid=(n_rows // t, SSD_CONV_DIM // CONV_COLS),
        in_specs=[pl.BlockSpec((SUBLANES, CONV_COLS), lambda i, j: (jnp.maximum((r0 + i) * sub - 1, 0), c0 + j)),
                  pl.BlockSpec((t, CONV_COLS), lambda i, j: (r0 + i, c0 + j)),
                  pl.BlockSpec((SUBLANES, CONV_COLS),
                               lambda i, j: (jnp.minimum((r0 + i + 1) * sub, n_sub - 1), c0 + j)),
                  pl.BlockSpec((SSD_CONV, CONV_COLS), lambda i, j: (0, j)),
                  pl.BlockSpec((1, CONV_COLS), lambda i, j: (0, j))],
        out_specs=pl.BlockSpec((t, CONV_COLS), lambda i, j: (i, j)),
        out_shape=jax.ShapeDtypeStruct((n_rows, SSD_CONV_DIM), jnp.float32),
        scratch_shapes=[pltpu.VMEM((t + 2 * SUBLANES, CONV_COLS), jnp.float32)],
        compiler_params=_params("arbitrary", "arbitrary"),
        name="ssd_conv",
    )(proj, proj, proj, w, b)


def _softplus(x):
    return jnp.maximum(x, 0.0) + jnp.log1p(jnp.exp(-jnp.abs(x)))


def _expand(xs, sel, terms):
    parts = [jnp.concatenate(_split3(x)[:terms], axis=1) for x in xs]
    out = jnp.dot(jnp.concatenate(parts, axis=0), sel, preferred_element_type=jnp.float32)
    rows = xs[0].shape[0]
    return [out[i * rows:(i + 1) * rows] for i in range(len(xs))]


def _ssd_kernel(xf_ref, xb_ref, dtf_ref, dtb_ref, dtbias_ref, aneg_ref, init_ref, yf_ref, yb_ref, st_ref,
                s_ref, *, nc):
    bf16, f32 = jnp.bfloat16, jnp.float32
    c = SSD_C
    hd, ns = SSD_HEAD_DIM, SSD_STATE
    pair_w = 2 * hd
    j = pl.program_id(1)

    @pl.when(j == 0)
    def _():
        s_ref[...] = init_ref[...]

    t_ids = lax.broadcasted_iota(jnp.int32, (c, c), 0)
    s_ids = lax.broadcasted_iota(jnp.int32, (c, c), 1)
    eye = (lax.broadcasted_iota(jnp.int32, (SSD_HEADS, SSD_HEADS), 0)
           == lax.broadcasted_iota(jnp.int32, (SSD_HEADS, SSD_HEADS), 1)).astype(bf16)
    head_of = lambda terms, n, w: (lax.broadcasted_iota(jnp.int32, (terms * SSD_HEADS, n), 1) // w
                                   == lax.broadcasted_iota(jnp.int32, (terms * SSD_HEADS, n), 0) % SSD_HEADS
                                   ).astype(bf16)
    sel_x = head_of(2, SSD_DIM, hd)
    sel_c = head_of(3, SSD_HEADS * c, c)
    low_lanes = lax.broadcasted_iota(jnp.int32, (c, pair_w), 1) < hd
    low_rows = lax.broadcasted_iota(jnp.int32, (pair_w, ns), 0) < hd

    for d, (x_ref, dt_ref, y_ref) in enumerate(((xf_ref, dtf_ref, yf_ref), (xb_ref, dtb_ref, yb_ref))):
        rev = d == 1
        causal = (s_ids >= t_ids) if rev else (s_ids <= t_ids)
        tri = causal.astype(bf16)
        dt = _softplus(dt_ref[:, d * SSD_HEADS:(d + 1) * SSD_HEADS] + dtbias_ref[d:d + 1, :])
        acs = _tri_cumsum(tri, dt * aneg_ref[d:d + 1, :])
        acs_t = sum(lax.dot_general(eye, part, _NT, preferred_element_type=f32) for part in _split3(acs))
        end = acs[0:1, :] if rev else acs[c - 1:c, :]
        dt_x, out_x, in_x = _expand([dt, jnp.exp(end - acs), jnp.exp(acs)], sel_x, 2)
        acs_c, = _expand([acs], sel_c, 3)
        end_decay = jnp.exp(end)
        for g in range(SSD_GROUPS):
            bg = x_ref[:, SSD_DIM + g * ns:SSD_DIM + (g + 1) * ns].astype(bf16)
            cg = x_ref[:, SSD_DIM + (SSD_GROUPS + g) * ns:SSD_DIM + (SSD_GROUPS + g + 1) * ns].astype(bf16)
            cb = lax.dot_general(cg, bg, _NT, preferred_element_type=f32)
            for p in range(g * 2, g * 2 + 2):
                lanes = slice(p * pair_w, (p + 1) * pair_w)
                scores = []
                for h in (2 * p, 2 * p + 1):
                    decay = jnp.exp(jnp.minimum(acs_c[:, h * c:(h + 1) * c] - acs_t[h:h + 1, :], 0.0))
                    scores.append(jnp.where(causal, cb * decay, 0.0).astype(bf16))
                xdt = x_ref[:, lanes] * dt_x[:, lanes]
                rhs = jnp.concatenate([jnp.where(low_lanes, xdt, 0.0), jnp.where(low_lanes, 0.0, xdt)],
                                      axis=0).astype(bf16)
                s_p = s_ref[d, p]
                y = (jnp.dot(jnp.concatenate(scores, axis=1), rhs, preferred_element_type=f32)
                     + lax.dot_general(cg, s_p.astype(bf16), _NT, preferred_element_type=f32) * in_x[:, lanes])
                y_ref[:, lanes] = y
                keep = jnp.where(low_rows, end_decay[:, 2 * p:2 * p + 1], end_decay[:, 2 * p + 1:2 * p + 2])
                s_ref[d, p] = s_p * keep + lax.dot_general((xdt * out_x[:, lanes]).astype(bf16), bg, _TN,
                                                           preferred_element_type=f32)

    @pl.when(j == nc - 1)
    def _():
        st_ref[...] = s_ref[...]


def _ssd_scan(xbc, dt_logits, row0, nb, length, dt_bias, a_neg, init):
    c = SSD_C
    nc = length // c
    r0 = row0 // c
    pair_state = (nb, 2, SSD_HEADS // 2, 2 * SSD_HEAD_DIM, SSD_STATE)
    state_spec = pl.BlockSpec((None,) + pair_state[1:], lambda b, j: (b, 0, 0, 0, 0))
    y_f, y_b, states = pl.pallas_call(
        functools.partial(_ssd_kernel, nc=nc),
        grid=(nb, nc),
        in_specs=[pl.BlockSpec((c, SSD_CONV_DIM), lambda b, j: (b * nc + j, 0)),
                  pl.BlockSpec((c, SSD_CONV_DIM), lambda b, j: (b * nc + nc - 1 - j, 0)),
                  pl.BlockSpec((c, LANES), lambda b, j: (r0 + b * nc + j, 0)),
                  pl.BlockSpec((c, LANES), lambda b, j: (r0 + b * nc + nc - 1 - j, 0)),
                  pl.BlockSpec((2, SSD_HEADS), lambda b, j: (0, 0)),
                  pl.BlockSpec((2, SSD_HEADS), lambda b, j: (0, 0)),
                  state_spec],
        out_specs=[pl.BlockSpec((c, SSD_DIM), lambda b, j: (b * nc + j, 0)),
                   pl.BlockSpec((c, SSD_DIM), lambda b, j: (b * nc + nc - 1 - j, 0)),
                   state_spec],
        out_shape=[jax.ShapeDtypeStruct((nb * length, SSD_DIM), jnp.float32),
                   jax.ShapeDtypeStruct((nb * length, SSD_DIM), jnp.float32),
                   jax.ShapeDtypeStruct(pair_state, jnp.float32)],
        scratch_shapes=[pltpu.VMEM(pair_state[1:], jnp.float32)],
        compiler_params=_params("arbitrary", "arbitrary"),
        name="ssd_scan",
    )(xbc, xbc, dt_logits, dt_logits, dt_bias, a_neg, init.reshape(pair_state))
    return y_f, y_b, states.reshape(nb, 2, SSD_HEADS, SSD_HEAD_DIM, SSD_STATE)


def _ssd_finish_kernel(yf_ref, yb_ref, x_ref, z_ref, d_ref, g_ref, o_ref):
    z = z_ref[...]
    y = (yf_ref[...] + yb_ref[...] + d_ref[...] * x_ref[...]) * (z * jax.nn.sigmoid(z))
    o_ref[...] = (y * lax.rsqrt(jnp.mean(y * y, axis=-1, keepdims=True) + NORM_EPS) * g_ref[...]).astype(o_ref.dtype)


def _ssd_finish(y_f, y_b, xbc, proj, row0, d_row, gain):
    n = y_f.shape[0]
    t = FINISH_ROWS
    r0 = row0 // t
    row_spec = pl.BlockSpec((t, SSD_DIM), lambda i: (i, 0))
    vec_spec = pl.BlockSpec((1, SSD_DIM), lambda i: (0, 0))
    return pl.pallas_call(
        _ssd_finish_kernel,
        grid=(n // t,),
        in_specs=[row_spec, row_spec, row_spec,
                  pl.BlockSpec((t, SSD_DIM), lambda i: (r0 + i, OFF_Z // SSD_DIM)),
                  vec_spec, vec_spec],
        out_specs=row_spec,
        out_shape=jax.ShapeDtypeStruct((n, SSD_DIM), jnp.bfloat16),
        compiler_params=_params("arbitrary"),
        name="ssd_finish",
    )(y_f, y_b, xbc, proj, d_row, gain)


def _hgrn_chunk(q, g, kk, v, state, rev, tri, lane_mod, diag, level_masks, b_ref, kk_ref):
    bf16, f32 = jnp.bfloat16, jnp.float32
    c = q.shape[0]
    ng = c // SUBLANES
    b = _tri_cumsum(tri, g)
    yield
    b_ref[...] = b
    kk_ref[...] = kk
    row = lambda a_ref, r: a_ref[r:r + 1, :]
    grp = lambda a, i: a[i * SUBLANES:(i + 1) * SUBLANES, :]
    b_end = row(b_ref, 0) if rev else row(b_ref, c - 1)

    ones = jnp.ones((HG_KDIM, c), bf16)
    diag_rows = []
    for i0 in range(0, ng, DIAG_BATCH):
        tiles = []
        for i in range(i0, i0 + DIAG_BATCH):
            qg, bg = grp(q, i), grp(b, i)
            tiles += [qg * jnp.exp(bg - row(b_ref, i * SUBLANES + j)) * row(kk_ref, i * SUBLANES + j)
                      for j in range(SUBLANES)]
        sums = jnp.dot(jnp.concatenate(tiles, axis=0).astype(bf16), ones, preferred_element_type=f32)
        for n in range(DIAG_BATCH):
            base = n * SUBLANES * SUBLANES
            acc = sums[base:base + SUBLANES, :]
            for j in range(1, SUBLANES):
                acc = jnp.where(lane_mod[j], sums[base + j * SUBLANES:base + (j + 1) * SUBLANES, :], acc)
            diag_rows.append(acc)
        yield
    att = jnp.where(diag, jnp.concatenate(diag_rows, axis=0), 0.0)

    for m, mask in zip(HG_LEVELS, level_masks):
        half = m // 2
        q_side, k_side = [], []
        for i in range(ng):
            start = (i * SUBLANES) // m * m
            later = (i * SUBLANES) % m >= half
            ref = row(b_ref, start + half if rev else start + half - 1)
            if later != rev:
                q_side.append(grp(q, i) * jnp.exp(grp(b, i) - ref))
                k_side.append(jnp.zeros((SUBLANES, HG_KDIM), f32))
            else:
                q_side.append(jnp.zeros((SUBLANES, HG_KDIM), f32))
                k_side.append(grp(kk, i) * jnp.exp(ref - grp(b, i)))
        a_m = lax.dot_general(jnp.concatenate(q_side, axis=0).astype(bf16),
                              jnp.concatenate(k_side, axis=0).astype(bf16), _NT, preferred_element_type=f32)
        att = att + (a_m if m == c else jnp.where(mask, a_m, 0.0))
        yield

    q_in = (q * jnp.exp(b)).astype(bf16)
    k_out = (kk * jnp.exp(b_end - b)).astype(bf16)
    yield
    s_t = state[0]
    o = (lax.dot_general(q_in, s_t.astype(bf16), _NT, preferred_element_type=f32)
         + jnp.dot(att.astype(bf16), v.astype(bf16), preferred_element_type=f32))
    state[0] = s_t * jnp.exp(b_end) + lax.dot_general(v.astype(bf16), k_out, _TN, preferred_element_type=f32)
    return o


def _in_lockstep(gens):
    results = [None] * len(gens)
    live = list(range(len(gens)))
    while live:
        for i in list(live):
            try:
                next(gens[i])
            except StopIteration as stop:
                results[i] = stop.value
                live.remove(i)
    return results


def _hgrn_kernel(q_ref, ff_ref, fb_ref, v_ref, gate_ref, lb_ref, gn_ref, init_ref, o_ref, st_ref,
                 acc_ref, s_ref, b_ref, kk_ref, *, nc):
    c = HG_C
    t_ids = lax.broadcasted_iota(jnp.int32, (c, c), 0)
    s_ids = lax.broadcasted_iota(jnp.int32, (c, c), 1)
    causal = (s_ids <= t_ids, s_ids >= t_ids)
    tri = tuple(m.astype(jnp.bfloat16) for m in causal)
    same_group = (t_ids // SUBLANES) == (s_ids // SUBLANES)
    diag = tuple(m & same_group for m in causal)
    lane_mod = [(s_ids[:SUBLANES] % SUBLANES) == j for j in range(SUBLANES)]

    def level_mask(m, rev):
        same = (t_ids // m) == (s_ids // m)
        t_late = (t_ids % m) >= m // 2
        s_late = (s_ids % m) >= m // 2
        return same & (t_late != s_late) & (t_late != rev)
    masks = tuple([level_mask(m, rev) for m in HG_LEVELS] for rev in (False, True))

    heads = init_ref.shape[1]
    for a in range(heads):
        for d in (0, 1):
            s_ref[2 * a + d] = init_ref[d, a].T

    per_step = HG_LOCKSTEP // (2 * heads)
    if (nc // 2) % per_step:
        per_step = 1
    half_steps = nc // 2 // per_step

    def run_step(j):
        states = [[s_ref[i]] for i in range(2 * heads)]
        where, gens = [], []
        for u in range(per_step):
            for a in range(heads):
                lanes = slice(a * HG_KDIM, (a + 1) * HG_KDIM)
                for d in (0, 1):
                    chunk = j * per_step + u if d == 0 else nc - 1 - (j * per_step + u)
                    r = pl.ds(pl.multiple_of(chunk * c, c), c)
                    x = q_ref[r, lanes]
                    q = x * jax.nn.sigmoid(x)
                    lb = lb_ref[d:d + 1, lanes]
                    f = lb + (1.0 - lb) * jax.nn.sigmoid((ff_ref, fb_ref)[d][r, lanes])
                    slot = (u * heads + a) * 2 + d
                    where.append((r, lanes))
                    gens.append(_hgrn_chunk(q, jnp.log(f), 1.0 - f, v_ref[r, lanes], states[2 * a + d], d == 1,
                                            tri[d], lane_mod, diag[d], masks[d], b_ref.at[slot], kk_ref.at[slot]))
        outs = _in_lockstep(gens)
        for i in range(2 * heads):
            s_ref[i] = states[i][0]
        return where, outs

    def first_half(j, carry):
        where, outs = run_step(j)
        for (r, lanes), o in zip(where, outs):
            acc_ref[r, lanes] = o
        return carry

    def second_half(j, carry):
        where, outs = run_step(j)
        for (r, lanes), o in zip(where, outs):
            o = o + acc_ref[r, lanes]
            y = o * lax.rsqrt(jnp.mean(o * o, axis=-1, keepdims=True) + NORM_EPS) * gn_ref[:, lanes]
            gate = gate_ref[r, lanes]
            o_ref[r, lanes] = (y * (gate * jax.nn.sigmoid(gate))).astype(o_ref.dtype)
        return carry

    lax.fori_loop(0, half_steps, first_half, 0)
    lax.fori_loop(half_steps, 2 * half_steps, second_half, 0)
    for a in range(heads):
        for d in (0, 1):
            st_ref[d, a] = s_ref[2 * a + d].T


def _hgrn_mixer(proj, row0, nb, length, lb, gain, init, heads):
    nc = length // HG_C
    assert nc % 2 == 0
    width = heads * HG_KDIM
    seq = lambda off: pl.BlockSpec((length, width), lambda b, h, col=off // width: (row0 + b, col + h))
    state_spec = pl.BlockSpec((None, 2, heads, HG_KDIM, HG_VDIM), lambda b, h: (b, 0, h, 0, 0))
    return pl.pallas_call(
        functools.partial(_hgrn_kernel, nc=nc),
        grid=(nb, HG_HEADS // heads),
        in_specs=[seq(OFF_HQ), seq(OFF_HF), seq(OFF_HF + HG_FDIM), seq(OFF_HI), seq(OFF_HG),
                  pl.BlockSpec((2, width), lambda b, h: (0, h)),
                  pl.BlockSpec((1, width), lambda b, h: (0, h)),
                  state_spec],
        out_specs=[pl.BlockSpec((length, width), lambda b, h: (b, h)), state_spec],
        out_shape=[jax.ShapeDtypeStruct((nb * length, HG_DIM), jnp.bfloat16),
                   jax.ShapeDtypeStruct((nb, 2, HG_HEADS, HG_KDIM, HG_VDIM), jnp.float32)],
        scratch_shapes=[pltpu.VMEM((length, width), jnp.float32),
                        pltpu.VMEM((2 * heads, HG_VDIM, HG_KDIM), jnp.float32),
                        pltpu.VMEM((HG_LOCKSTEP, HG_C, HG_KDIM), jnp.float32),
                        pltpu.VMEM((HG_LOCKSTEP, HG_C, HG_KDIM), jnp.float32)],
        compiler_params=_params("arbitrary", "arbitrary"),
        name="hgrn_mixer",
    )(proj, proj, proj, proj, proj, lb, gain, init)


def _softmax_av(scores, values):
    f32, bf16 = jnp.float32, jnp.bfloat16
    m = functools.reduce(jnp.maximum, [jnp.max(s, axis=-1, keepdims=True) for s in scores])
    ps = [jnp.exp(s - m) for s in scores]
    denom = functools.reduce(jnp.add, [jnp.sum(p, axis=-1, keepdims=True) for p in ps])
    acc = functools.reduce(jnp.add, [jnp.dot(p.astype(bf16), v, preferred_element_type=f32)
                                     for p, v in zip(ps, values)])
    return acc / denom


def _ctx_attn_kernel(q_ref, k_ref, v_ref, nk_in_ref, nv_in_ref, o_ref, nk_ref, nv_ref):
    bf16 = jnp.bfloat16
    scale = ATT_HEAD_DIM ** -0.5
    k = k_ref[...]
    v = v_ref[...]
    nk_ref[...] = k
    nv_ref[...] = v
    s = lax.dot_general(q_ref[...].astype(bf16), k.astype(bf16), _NT, preferred_element_type=jnp.float32) * scale
    o_ref[...] = _softmax_av([s], [v.astype(bf16)]).astype(o_ref.dtype)


def _context_attention(proj, nb, length, new_k, new_v, layer):
    spec = lambda off: pl.BlockSpec((length, ATT_HEAD_DIM), lambda b, h, col=off // ATT_HEAD_DIM: (b, col + h))
    cache_spec = pl.BlockSpec((None, None, length, ATT_HEAD_DIM), lambda b, h: (b, layer, 0, h))
    any_spec = pl.BlockSpec(memory_space=pl.ANY)
    return pl.pallas_call(
        _ctx_attn_kernel,
        grid=(nb, ATT_HEADS),
        in_specs=[spec(OFF_AQ), spec(OFF_AK), spec(OFF_AV), any_spec, any_spec],
        out_specs=[pl.BlockSpec((length, ATT_HEAD_DIM), lambda b, h: (b, h)), cache_spec, cache_spec],
        out_shape=[jax.ShapeDtypeStruct((nb * length, ATT_DIM), jnp.bfloat16),
                   jax.ShapeDtypeStruct(new_k.shape, new_k.dtype),
                   jax.ShapeDtypeStruct(new_v.shape, new_v.dtype)],
        input_output_aliases={3: 1, 4: 2},
        compiler_params=_params("arbitrary", "arbitrary"),
        name="context_attention",
    )(proj, proj, proj, new_k, new_v)


def _window_bias(rpb):
    col = jnp.arange(GRID_W)
    cs = jnp.clip(col - WIN_COLS // 2, 0, GRID_W - WIN_COLS)
    col_mask = (col[None, :] >= cs[:, None]) & (col[None, :] < cs[:, None] + WIN_COLS)
    dc_idx = jnp.clip(col[None, :] - col[:, None] + WIN_COLS - 1, 0, 2 * WIN_COLS - 2)
    bias = jnp.where(col_mask, rpb[:, :, dc_idx].astype(jnp.float32), MASKED)
    wins = [bias[:, d0:d0 + WIN_ROWS].transpose(0, 2, 1, 3).reshape(rpb.shape[0], GRID_W, WIN_ROWS * GRID_W)
            for d0 in range(WIN_ROWS)]
    return jnp.stack(wins, axis=1)


def _natten_kernel(q_ref, k_ref, v_ref, kc_ref, vc_ref, bias_ref, o_ref, kb_ref, vb_ref, *, rows):
    bf16, f32 = jnp.bfloat16, jnp.float32
    scale = ATT_HEAD_DIM ** -0.5
    win = WIN_ROWS * GRID_W

    def cast(i, carry):
        sl = pl.ds(pl.multiple_of(i * CAST_ROWS, CAST_ROWS), CAST_ROWS)
        kb_ref[sl, :] = k_ref[sl, :].astype(bf16)
        vb_ref[sl, :] = v_ref[sl, :].astype(bf16)
        return carry
    lax.fori_loop(0, rows * GRID_W // CAST_ROWS, cast, 0)

    kc = kc_ref[...].astype(bf16)
    vc = vc_ref[...].astype(bf16)

    def row_block(r):
        rs = jnp.clip(r - WIN_ROWS // 2, 0, rows - WIN_ROWS)
        d0 = rs - r + WIN_ROWS - 1
        q = q_ref[pl.ds(pl.multiple_of(r * GRID_W, GRID_W), GRID_W), :].astype(bf16)
        keys = pl.ds(pl.multiple_of(rs * GRID_W, GRID_W), win)
        s_lat = lax.dot_general(q, kb_ref[keys, :], _NT, preferred_element_type=f32) * scale + bias_ref[d0]
        s_ctx = lax.dot_general(q, kc, _NT, preferred_element_type=f32) * scale
        yield
        m = jnp.maximum(jnp.max(s_lat, axis=-1, keepdims=True), jnp.max(s_ctx, axis=-1, keepdims=True))
        yield
        p_lat = jnp.exp(s_lat - m)
        p_ctx = jnp.exp(s_ctx - m)
        denom = jnp.sum(p_lat, axis=-1, keepdims=True) + jnp.sum(p_ctx, axis=-1, keepdims=True)
        acc = (jnp.dot(p_lat.astype(bf16), vb_ref[keys, :], preferred_element_type=f32)
               + jnp.dot(p_ctx.astype(bf16), vc, preferred_element_type=f32))
        yield
        o_ref[pl.ds(pl.multiple_of(r * GRID_W, GRID_W), GRID_W), :] = (acc / denom).astype(o_ref.dtype)

    def row_group(g, carry):
        _in_lockstep([row_block(g * ROW_UNROLL + u) for u in range(ROW_UNROLL)])
        return carry
    lax.fori_loop(0, rows // ROW_UNROLL, row_group, 0)


def _neighbourhood_attention(proj, row0, nb, length, cache_k, cache_v, layer, bias_win):
    rows = length // GRID_W
    past = cache_k.shape[2]
    spec = lambda off: pl.BlockSpec((length, ATT_HEAD_DIM),
                                    lambda b, h, col=off // ATT_HEAD_DIM: (row0 + b, col + h))
    cache_spec = pl.BlockSpec((None, None, past, ATT_HEAD_DIM), lambda b, h: (b, layer, 0, h))
    return pl.pallas_call(
        functools.partial(_natten_kernel, rows=rows),
        grid=(nb, ATT_HEADS),
        in_specs=[spec(OFF_AQ), spec(OFF_AK), spec(OFF_AV), cache_spec, cache_spec,
                  pl.BlockSpec((None, WIN_ROWS, GRID_W, WIN_ROWS * GRID_W), lambda b, h: (h, 0, 0, 0))],
        out_specs=pl.BlockSpec((length, ATT_HEAD_DIM), lambda b, h: (b, h)),
        out_shape=jax.ShapeDtypeStruct((nb * length, ATT_DIM), jnp.bfloat16),
        scratch_shapes=[pltpu.VMEM((length, ATT_HEAD_DIM), jnp.bfloat16),
                        pltpu.VMEM((length, ATT_HEAD_DIM), jnp.bfloat16)],
        compiler_params=_params("arbitrary", "arbitrary"),
        name="neighbourhood_attention",
    )(proj, proj, proj, cache_k, cache_v, bias_win)


def _merge_kernel(ca_ref, cb_ref, cc_ref, la_ref, lb_ref, lc_ref, wa_ref, wb_ref, wc_ref,
                  ga_ref, gb_ref, gc_ref, o_ref):
    f32 = jnp.float32
    is_ctx = pl.program_id(0) < N_CTX // ROW_TILE

    def merge(ya_ref, yb_ref, yc_ref):
        acc = jax.nn.sigmoid(ga_ref[...]) * jnp.dot(ya_ref[...], wa_ref[...], preferred_element_type=f32)
        acc = acc + jax.nn.sigmoid(gb_ref[...]) * jnp.dot(yb_ref[...], wb_ref[...], preferred_element_type=f32)
        acc = acc + jax.nn.sigmoid(gc_ref[...]) * jnp.dot(yc_ref[...], wc_ref[...], preferred_element_type=f32)
        o_ref[...] = acc.astype(o_ref.dtype)

    @pl.when(is_ctx)
    def _():
        merge(ca_ref, cb_ref, cc_ref)

    @pl.when(jnp.logical_not(is_ctx))
    def _():
        merge(la_ref, lb_ref, lc_ref)


def _branch_merge(ys_ctx, ys_lat, ws, proj):
    tm, tn = ROW_TILE, MERGE_COLS
    kdim = ws[0].shape[0]
    cg = OFF_GATES // tn
    per = D_MODEL // tn
    ctx_tiles = N_CTX // tm
    ctx_spec = pl.BlockSpec((tm, kdim), lambda i, j: (jnp.minimum(i, ctx_tiles - 1), 0))
    lat_spec = pl.BlockSpec((tm, kdim), lambda i, j: (jnp.maximum(i - ctx_tiles, 0), 0))
    w_spec = pl.BlockSpec((kdim, tn), lambda i, j: (0, j))
    g_spec = lambda b: pl.BlockSpec((tm, tn), lambda i, j, b=b: (i, cg + b * per + j))
    return pl.pallas_call(
        _merge_kernel,
        grid=(N_TOK // tm, D_MODEL // tn),
        in_specs=[ctx_spec] * N_BRANCH + [lat_spec] * N_BRANCH + [w_spec] * N_BRANCH
                 + [g_spec(b) for b in range(N_BRANCH)],
        out_specs=pl.BlockSpec((tm, tn), lambda i, j: (i, j)),
        out_shape=jax.ShapeDtypeStruct((N_TOK, D_MODEL), jnp.bfloat16),
        compiler_params=_params("arbitrary", "arbitrary"),
        name="branch_merge",
    )(*ys_ctx, *ys_lat, *ws, proj, proj, proj)


def _out_residual_kernel(m_ref, w_ref, x_ref, gate_ref, o_ref):
    o_ref[...] = x_ref[...] + gate_ref[...] * jnp.dot(m_ref[...], w_ref[...], preferred_element_type=jnp.float32)


def _out_residual(merged, w, x, gate):
    tm, tn = ROW_TILE, COL_TILE
    return pl.pallas_call(
        _out_residual_kernel,
        grid=(N_TOK // tm, D_MODEL // tn),
        in_specs=[pl.BlockSpec((tm, D_MODEL), lambda i, j: (i, 0)),
                  pl.BlockSpec((D_MODEL, tn), lambda i, j: (0, j)),
                  pl.BlockSpec((tm, tn), lambda i, j: (i, j)),
                  pl.BlockSpec((None, 1, tn), lambda i, j: (_mod_row(i), 0, j))],
        out_specs=pl.BlockSpec((tm, tn), lambda i, j: (i, j)),
        out_shape=jax.ShapeDtypeStruct((N_TOK, D_MODEL), jnp.float32),
        compiler_params=_params("arbitrary", "arbitrary"),
        name="out_residual",
    )(merged, w, x, gate)


def _first_max(vals):
    best, idx = vals[0], jnp.zeros(vals[0].shape, jnp.int32)
    for k in range(1, len(vals)):
        better = vals[k] > best
        best = jnp.where(better, vals[k], best)
        idx = jnp.where(better, k, idx)
    return best, idx


def _pick(idx, vals):
    out = vals[0]
    for k in range(1, len(vals)):
        out = jnp.where(idx == k, vals[k], out)
    return out


def _norm_router_kernel(x_ref, g_ref, sc_ref, sh_ref, wr_hi_ref, wr_lo_ref, rb_ref,
                        h_ref, ids_ref, wts_ref, cnt_ref, lo_ref, base_ref):
    f32, bf16 = jnp.float32, jnp.bfloat16
    t = x_ref.shape[0]

    @pl.when(pl.program_id(0) == 0)
    def _():
        base_ref[...] = jnp.zeros_like(base_ref)

    def store(rows, h):
        h_hi = h.astype(bf16)
        h_ref[rows, :] = h_hi
        lo_ref[rows, :] = (h - h_hi.astype(f32)).astype(bf16)
    _modulated_norm(x_ref, g_ref, sc_ref, sh_ref, store)

    logits = (jnp.dot(h_ref[...], wr_hi_ref[...], preferred_element_type=f32)
              + (jnp.dot(h_ref[...], wr_lo_ref[...], preferred_element_type=f32)
                 + jnp.dot(lo_ref[...], wr_hi_ref[...], preferred_element_type=f32)))
    logits = logits.T[:N_EXPERTS]
    scores = jax.nn.sigmoid(logits)
    sel = scores + rb_ref[...]
    row = lambda a, e: a[e:e + 1, :]

    group_scores = []
    for g in range(N_EXPERT_GROUPS):
        v = [row(sel, g * EXPERTS_PER_GROUP + k) for k in range(EXPERTS_PER_GROUP)]
        pair_sums = [v[a] + v[b] for a in range(EXPERTS_PER_GROUP) for b in range(a + 1, EXPERTS_PER_GROUP)]
        group_scores.append(functools.reduce(jnp.maximum, pair_sums))
    _, grp = _first_max(group_scores)

    in_sel = [_pick(grp, [row(sel, g * EXPERTS_PER_GROUP + k) for g in range(N_EXPERT_GROUPS)])
              for k in range(EXPERTS_PER_GROUP)]
    in_score = [_pick(grp, [row(scores, g * EXPERTS_PER_GROUP + k) for g in range(N_EXPERT_GROUPS)])
                for k in range(EXPERTS_PER_GROUP)]
    _, i1 = _first_max(in_sel)
    _, i2 = _first_max([jnp.where(i1 == k, -jnp.inf, in_sel[k]) for k in range(EXPERTS_PER_GROUP)])
    s1 = _pick(i1, in_score)
    s2 = _pick(i2, in_score)
    e1 = grp * EXPERTS_PER_GROUP + i1
    e2 = grp * EXPERTS_PER_GROUP + i2

    e_ids = lax.broadcasted_iota(jnp.int32, (N_EXPERTS, t), 0)
    hit1 = e_ids == e1
    hit2 = e_ids == e2
    cnt = jnp.where(hit1 | hit2, 1.0, 0.0).astype(bf16)
    before = (lax.broadcasted_iota(jnp.int32, (t, t), 0) < lax.broadcasted_iota(jnp.int32, (t, t), 1)).astype(bf16)
    prefix = jnp.dot(cnt, before, preferred_element_type=f32) + base_ref[:, 0:1]
    rank1 = jnp.sum(jnp.where(hit1, prefix, 0.0), axis=0, keepdims=True)
    rank2 = jnp.sum(jnp.where(hit2, prefix, 0.0), axis=0, keepdims=True)
    base_ref[...] = base_ref[...] + jnp.dot(cnt, jnp.ones((t, LANES), bf16), preferred_element_type=f32)
    cnt_ref[...] = base_ref[...]

    zeros = jnp.zeros((SUBLANES - 4, t), jnp.int32)
    ids_ref[...] = jnp.concatenate([e1, e2, rank1.astype(jnp.int32), rank2.astype(jnp.int32), zeros], axis=0)
    total = s1 + s2
    wts_ref[...] = jnp.concatenate([s1 / total, s2 / total, jnp.zeros((SUBLANES - 2, t), f32)], axis=0)


def _norm_router(x, gain, scale, shift, wr_hi, wr_lo, router_bias):
    t = ROUTER_TILE
    per = ROW_TILE // t
    mod_spec = pl.BlockSpec((None, 1, D_MODEL), lambda i: (_mod_row(i // per), 0, 0))
    return pl.pallas_call(
        _norm_router_kernel,
        grid=(N_TOK // t,),
        in_specs=[pl.BlockSpec((t, D_MODEL), lambda i: (i, 0)),
                  pl.BlockSpec((1, D_MODEL), lambda i: (0, 0)),
                  mod_spec, mod_spec,
                  pl.BlockSpec((D_MODEL, LANES), lambda i: (0, 0)),
                  pl.BlockSpec((D_MODEL, LANES), lambda i: (0, 0)),
                  pl.BlockSpec((N_EXPERTS, 1), lambda i: (0, 0))],
        out_specs=[pl.BlockSpec((t, D_MODEL), lambda i: (i, 0)),
                   pl.BlockSpec((SUBLANES, t), lambda i: (0, i)),
                   pl.BlockSpec((SUBLANES, t), lambda i: (0, i)),
                   pl.BlockSpec((N_EXPERTS, LANES), lambda i: (0, 0))],
        out_shape=[jax.ShapeDtypeStruct((N_TOK, D_MODEL), jnp.bfloat16),
                   jax.ShapeDtypeStruct((SUBLANES, N_TOK), jnp.int32),
                   jax.ShapeDtypeStruct((SUBLANES, N_TOK), jnp.float32),
                   jax.ShapeDtypeStruct((N_EXPERTS, LANES), jnp.float32)],
        scratch_shapes=[pltpu.VMEM((t, D_MODEL), jnp.bfloat16),
                        pltpu.VMEM((N_EXPERTS, LANES), jnp.float32)],
        compiler_params=_params("arbitrary"),
        name="norm_router",
    )(x, gain, scale, shift, wr_hi, wr_lo, router_bias)


def _expert_kernel(be_ref, na_ref, x_ref, w1_ref, w3_ref, w2_ref, o_ref):
    active = pl.program_id(0) < na_ref[0]

    @pl.when(active)
    def _():
        bf16 = jnp.bfloat16
        x = x_ref[...]
        a = jnp.dot(x, w1_ref[...].astype(bf16), preferred_element_type=jnp.float32)
        b = jnp.dot(x, w3_ref[...].astype(bf16), preferred_element_type=jnp.float32)
        hdn = (a * jax.nn.sigmoid(a)) * b
        o_ref[...] = jnp.dot(hdn.astype(bf16), w2_ref[...].astype(bf16),
                             preferred_element_type=jnp.float32).astype(o_ref.dtype)

    @pl.when(jnp.logical_not(active))
    def _():
        o_ref[...] = jnp.zeros_like(o_ref)


def _expert_blocks(buf, block_e, n_active, w1, w3, w2):
    n_blocks = buf.shape[0] // MOE_BLOCK
    w_spec = lambda r, c: pl.BlockSpec((None, r, c), lambda i, be, na: (be[i], 0, 0), pipeline_mode=pl.Buffered(1))
    grid_spec = pltpu.PrefetchScalarGridSpec(
        num_scalar_prefetch=2,
        grid=(n_blocks,),
        in_specs=[pl.BlockSpec((MOE_BLOCK, D_MODEL), lambda i, be, na: (i, 0)),
                  w_spec(D_MODEL, D_EXPERT), w_spec(D_MODEL, D_EXPERT), w_spec(D_EXPERT, D_MODEL)],
        out_specs=pl.BlockSpec((MOE_BLOCK, D_MODEL), lambda i, be, na: (i, 0)),
    )
    return pl.pallas_call(
        _expert_kernel,
        grid_spec=grid_spec,
        out_shape=jax.ShapeDtypeStruct((n_blocks * MOE_BLOCK, D_MODEL), jnp.float32),
        compiler_params=_params("arbitrary"),
        name="moe_experts",
    )(block_e, n_active, buf, w1, w3, w2)


def _row_copy(out_hbm, buf_ref, sem_ref, slot, k, t, row):
    return pltpu.make_async_copy(out_hbm.at[pl.ds(row, 1), :], buf_ref.at[slot, k, pl.ds(t, 1), :], sem_ref.at[slot])


def _combine_kernel(dcur_ref, dnext_ref, w_ref, x_ref, gate_ref, out_hbm, o_ref, buf_ref, sem_ref):
    t_rows = x_ref.shape[0]
    i = pl.program_id(0)
    n = pl.num_programs(0)
    slot = i % 2

    def issue(d_ref, s):
        for t in range(t_rows):
            for k in range(TOP_K):
                _row_copy(out_hbm, buf_ref, sem_ref, s, k, t, d_ref[k, t]).start()

    @pl.when(i == 0)
    def _():
        issue(dcur_ref, 0)

    @pl.when(i + 1 < n)
    def _():
        issue(dnext_ref, 1 - slot)

    def wait(t, carry):
        for k in range(TOP_K):
            _row_copy(out_hbm, buf_ref, sem_ref, slot, k, t, 0).wait()
        return carry
    lax.fori_loop(0, t_rows, wait, 0, unroll=COMBINE_UNROLL)

    w = w_ref[...]
    moe = w[:, 0:1] * buf_ref[slot, 0] + w[:, 1:2] * buf_ref[slot, 1]
    o_ref[...] = x_ref[...] + gate_ref[...] * moe


def _moe_combine(x, gate, out, dest, wts_t):
    t = COMBINE_ROWS
    steps = N_TOK // t
    per = ROW_TILE // t
    smem_spec = lambda shift: pl.BlockSpec((SUBLANES, t), lambda i: (0, jnp.minimum(i + shift, steps - 1)),
                                           memory_space=pltpu.SMEM)
    return pl.pallas_call(
        _combine_kernel,
        grid=(steps,),
        in_specs=[smem_spec(0), smem_spec(1),
                  pl.BlockSpec((t, LANES), lambda i: (i, 0)),
                  pl.BlockSpec((t, D_MODEL), lambda i: (i, 0)),
                  pl.BlockSpec((None, 1, D_MODEL), lambda i: (_mod_row(i // per), 0, 0)),
                  pl.BlockSpec(memory_space=pl.ANY)],
        out_specs=pl.BlockSpec((t, D_MODEL), lambda i: (i, 0)),
        out_shape=jax.ShapeDtypeStruct((N_TOK, D_MODEL), jnp.float32),
        scratch_shapes=[pltpu.VMEM((2, TOP_K, t, D_MODEL), jnp.float32),
                        pltpu.SemaphoreType.DMA((2,))],
        compiler_params=_params("arbitrary"),
        name="moe_combine",
    )(dest, dest, wts_t, x, gate, out)


def _moe(x, gate, h, ids, wts, counts, layer, w1, w3, w2):
    nk = N_TOK * TOP_K
    n_blocks = (nk + N_EXPERTS * (MOE_BLOCK - 1)) // MOE_BLOCK
    counts = counts[:, 0].astype(jnp.int32)
    padded = (counts + MOE_BLOCK - 1) // MOE_BLOCK * MOE_BLOCK
    pad_end = jnp.cumsum(padded)
    pad_start = pad_end - padded
    dest1 = pad_start[ids[0]] + ids[2]
    dest2 = pad_start[ids[1]] + ids[3]
    tok = jnp.arange(N_TOK, dtype=jnp.int32)
    src = jnp.zeros((n_blocks * MOE_BLOCK,), jnp.int32).at[jnp.concatenate([dest1, dest2])].set(
        jnp.concatenate([tok, tok]))
    block_e = jnp.minimum(jnp.searchsorted(pad_end, jnp.arange(n_blocks, dtype=jnp.int32) * MOE_BLOCK, side='right'),
                          N_EXPERTS - 1).astype(jnp.int32)
    n_active = (pad_end[-1:] // MOE_BLOCK).astype(jnp.int32)
    out = _expert_blocks(h[src], block_e + layer * N_EXPERTS, n_active, w1, w3, w2)
    dest = jnp.concatenate([dest1[None, :], dest2[None, :], jnp.zeros((SUBLANES - TOP_K, N_TOK), jnp.int32)], axis=0)
    wts_t = jnp.pad(wts[:TOP_K].T, ((0, 0), (0, LANES - TOP_K)))
    return _moe_combine(x, gate, out, dest, wts_t)


def _final_norm_kernel(x_ref, g_ref, ctx_ref, lat_ref):
    g = g_ref[...]

    def norm_into(o_ref):
        def body(r, carry):
            rows = pl.ds(pl.multiple_of(r * NORM_ROWS, NORM_ROWS), NORM_ROWS)
            x = x_ref[rows, :]
            o_ref[rows, :] = (x * lax.rsqrt(jnp.mean(x * x, axis=-1, keepdims=True) + NORM_EPS)) * g
            return carry
        lax.fori_loop(0, x_ref.shape[0] // NORM_ROWS, body, 0)

    is_ctx = pl.program_id(0) < N_CTX // ROW_TILE

    @pl.when(is_ctx)
    def _():
        norm_into(ctx_ref)

    @pl.when(jnp.logical_not(is_ctx))
    def _():
        norm_into(lat_ref)


def _final_norm(x, gain):
    ctx_tiles = N_CTX // ROW_TILE
    return pl.pallas_call(
        _final_norm_kernel,
        grid=(N_TOK // ROW_TILE,),
        in_specs=[pl.BlockSpec((ROW_TILE, D_MODEL), lambda i: (i, 0)),
                  pl.BlockSpec((1, D_MODEL), lambda i: (0, 0))],
        out_specs=[pl.BlockSpec((ROW_TILE, D_MODEL), lambda i: (jnp.minimum(i, ctx_tiles - 1), 0)),
                   pl.BlockSpec((ROW_TILE, D_MODEL), lambda i: (jnp.maximum(i - ctx_tiles, 0), 0))],
        out_shape=[jax.ShapeDtypeStruct((N_CTX, D_MODEL), jnp.float32),
                   jax.ShapeDtypeStruct((N_LAT, D_MODEL), jnp.float32)],
        compiler_params=_params("arbitrary"),
        name="final_norm",
    )(x, gain)


def _split_w_in(w):
    c0 = SSD_DIM + SSD_CONV_DIM
    c1 = c0 + 2 * SSD_HEADS
    dt_cols = jnp.pad(w[:, c0:c1], ((0, 0), (0, LANES - 2 * SSD_HEADS)))
    return (jnp.concatenate([w[:, :c0], w[:, c1:]], axis=1).astype(jnp.bfloat16), dt_cols.astype(jnp.bfloat16))


def kernel(x_prompt, x_sample, cache_k, cache_v, state_ssd, state_hgrn, c, c_ctx, w_ada, b_ada, norm_mix, norm_moe, w_in, ssd_conv_w, ssd_conv_b, ssd_dt_bias, ssd_a_log, ssd_d, ssd_norm, hg_lb_logits, hg_norm, att_rpb, w_br_ssd, w_br_hg, w_br_att, w_out, w_router, router_bias, moe_w1, moe_w3, moe_w2, final_norm):
    bf16, f32 = jnp.bfloat16, jnp.float32
    lb_cum = jnp.cumsum(jax.nn.softmax(hg_lb_logits.astype(f32), axis=1), axis=1)
    lower_bounds = lb_cum - lb_cum[:, :1]

    x = jnp.concatenate([x_prompt.reshape(N_CTX, D_MODEL), x_sample.reshape(N_LAT, D_MODEL)], axis=0)

    n_mod = 1 + DEC_BATCH
    cond = jnp.concatenate([c_ctx[None, :], c], axis=0)
    cond = jnp.pad(jax.nn.silu(cond), ((0, 2 * SUBLANES - n_mod), (0, 0)))

    wr = jnp.pad(w_router, ((0, 0), (0, LANES - N_EXPERTS)))
    wr_hi = wr.astype(bf16)
    wr_lo = (wr - wr_hi.astype(f32)).astype(bf16)
    cache_k = cache_k.reshape(DEC_BATCH, DEPTH, -1, ATT_DIM)
    cache_v = cache_v.reshape(DEC_BATCH, DEPTH, -1, ATT_DIM)
    lat_row0 = N_CTX // DEC_SEQ
    zero_ssd = jnp.zeros((BATCH, 2, SSD_HEADS, SSD_HEAD_DIM, SSD_STATE), f32)
    zero_hg = jnp.zeros((BATCH, 2, HG_HEADS, HG_KDIM, HG_VDIM), f32)

    expert_w = tuple(w.reshape((DEPTH * N_EXPERTS,) + w.shape[2:]) for w in (moe_w1, moe_w3, moe_w2))

    new_k = jnp.zeros((BATCH, DEPTH, SEQ, ATT_DIM), f32)
    new_v = jnp.zeros((BATCH, DEPTH, SEQ, ATT_DIM), f32)
    new_ssd, new_hg = [], []
    for l in range(DEPTH):
        mod = _matmul(cond, w_ada, l, 2 * SUBLANES, COL_TILE)[:n_mod] + b_ada[l]
        mod = mod.reshape(n_mod, 6, 1, D_MODEL)
        shift_m, scale_m, gate_m, shift_f, scale_f, gate_f = (mod[:, i] for i in range(6))

        proj, dt_logits = _norm_matmul(x, norm_mix[l][None, :], scale_m, shift_m, *_split_w_in(w_in[l]))

        conv_b = ssd_conv_b[l][None, :]
        a_neg = -jnp.exp(ssd_a_log[l].astype(f32))
        d_row = jnp.repeat(ssd_d[l], SSD_HEAD_DIM)[None, :]
        ssd_gain = ssd_norm[l][None, :]
        y_ssd = []
        for row0, nb, length, init in ((0, BATCH, SEQ, zero_ssd), (N_CTX, DEC_BATCH, DEC_SEQ, state_ssd[:, l])):
            xbc = _ssd_conv(proj, row0, nb * length, length, ssd_conv_w[l], conv_b)
            y_f, y_b, states = _ssd_scan(xbc, dt_logits, row0, nb, length, ssd_dt_bias[l], a_neg, init)
            y_ssd.append(_ssd_finish(y_f, y_b, xbc, proj, row0, d_row, ssd_gain))
            if row0 == 0:
                new_ssd.append(states)

        lb = lower_bounds[:, l]
        hg_gain = hg_norm[l].reshape(1, HG_DIM)
        y_hg_ctx, states = _hgrn_mixer(proj, 0, BATCH, SEQ, lb, hg_gain, zero_hg, HG_CTX_HEADS)
        new_hg.append(states)
        y_hg_lat, _ = _hgrn_mixer(proj, lat_row0, DEC_BATCH, DEC_SEQ, lb, hg_gain, state_hgrn[:, l], 1)

        y_att_ctx, new_k, new_v = _context_attention(proj, BATCH, SEQ, new_k, new_v, l)
        y_att_lat = _neighbourhood_attention(proj, lat_row0, DEC_BATCH, DEC_SEQ, cache_k, cache_v, l,
                                             _window_bias(att_rpb[l]))

        merged = _branch_merge((y_ssd[0], y_hg_ctx, y_att_ctx), (y_ssd[1], y_hg_lat, y_att_lat),
                               (w_br_ssd[l].astype(bf16), w_br_hg[l].astype(bf16), w_br_att[l].astype(bf16)), proj)
        x = _out_residual(merged, w_out[l].astype(bf16), x, gate_m)

        h2, ids, wts, counts = _norm_router(x, norm_moe[l][None, :], scale_f, shift_f, wr_hi, wr_lo,
                                            router_bias.astype(f32)[:, None])
        x = _moe(x, gate_f, h2, ids, wts, counts, l, *expert_w)

    y_ctx, y_lat = _final_norm(x, final_norm[None, :])
    y_prompt = y_ctx.reshape(BATCH, SEQ, D_MODEL)
    y_sample = y_lat.reshape(DEC_BATCH, DEC_SEQ, D_MODEL)
    cache_shape = (BATCH, DEPTH, SEQ, ATT_HEADS, ATT_HEAD_DIM)
    return (y_prompt, y_sample, new_k.reshape(cache_shape), new_v.reshape(cache_shape),
            jnp.stack(new_ssd, axis=1), jnp.stack(new_hg, axis=1))
```

```python
import functools

import jax
import jax.numpy as jnp
from jax import lax
from jax.experimental import pallas as pl
from jax.experimental.pallas import tpu as pltpu

D_MODEL = 2048
BATCH = 32
SEQ = 256
DEPTH = 2
DEC_BATCH = 8
DEC_SEQ = 4096
GRID_W = 64
NORM_EPS = 1e-6
SSD_HEADS = 16
SSD_HEAD_DIM = 64
SSD_DIM = SSD_HEADS * SSD_HEAD_DIM
SSD_STATE = 64
SSD_GROUPS = 4
SSD_CONV = 5
SSD_CONV_DIM = SSD_DIM + 2 * SSD_GROUPS * SSD_STATE
HG_HEADS = 8
HG_KDIM = 128
HG_VDIM = 128
HG_FDIM = HG_HEADS * HG_KDIM
HG_DIM = HG_HEADS * HG_VDIM
ATT_HEADS = 8
ATT_HEAD_DIM = 128
ATT_DIM = ATT_HEADS * ATT_HEAD_DIM
WIN_ROWS = 8
WIN_COLS = 16
N_BRANCH = 3
N_EXPERTS = 16
N_EXPERT_GROUPS = 4
EXPERTS_PER_GROUP = N_EXPERTS // N_EXPERT_GROUPS
TOP_K = 2
D_EXPERT = 1024

N_CTX = BATCH * SEQ
N_LAT = DEC_BATCH * DEC_SEQ
N_TOK = N_CTX + N_LAT

VMEM_LIMIT_BYTES = 56 * 1024 * 1024
LANES = 128
SUBLANES = 8

OFF_Z = 0
OFF_XBC = OFF_Z + SSD_DIM
OFF_HQ = OFF_XBC + SSD_CONV_DIM
OFF_HF = OFF_HQ + HG_FDIM
OFF_HI = OFF_HF + 2 * HG_FDIM
OFF_HG = OFF_HI + HG_DIM
OFF_AQ = OFF_HG + HG_DIM
OFF_AK = OFF_AQ + ATT_DIM
OFF_AV = OFF_AK + ATT_DIM
OFF_GATES = OFF_AV + ATT_DIM
PROJ_DIM = OFF_GATES + N_BRANCH * D_MODEL
PROJ_TILE = 1536

ROW_TILE = 1024
COL_TILE = 1024
MERGE_COLS = 512
NORM_ROWS = 64
ROUTER_TILE = 512
MOE_BLOCK = 512
CONV_ROWS = 1024
CONV_SUB = 256
CONV_COLS = 512
SSD_C = 128
FINISH_ROWS = 256
HG_C = 128
HG_LEVELS = (16, 32, 64, 128)
CAST_ROWS = 512
COMBINE_ROWS = 256
COMBINE_UNROLL = 8
DIAG_BATCH = 4
ROW_UNROLL = 8
HG_LOCKSTEP = 8
HG_CTX_HEADS = 4
MASKED = -1e30

_NT = (((1,), (1,)), ((), ()))
_TN = (((0,), (0,)), ((), ()))


def _params(*semantics):
    return pltpu.CompilerParams(dimension_semantics=semantics, vmem_limit_bytes=VMEM_LIMIT_BYTES)


def _mod_row(i):
    ctx_tiles = N_CTX // ROW_TILE
    tiles_per_req = DEC_SEQ // ROW_TILE
    return jnp.where(i < ctx_tiles, 0, 1 + (i - ctx_tiles) // tiles_per_req)


def _split3(x):
    bf16, f32 = jnp.bfloat16, jnp.float32
    x1 = x.astype(bf16)
    r = x - x1.astype(f32)
    x2 = r.astype(bf16)
    x3 = (r - x2.astype(f32)).astype(bf16)
    return x1, x2, x3


def _tri_cumsum(tri, x):
    x1, x2, x3 = _split3(x)
    f32 = jnp.float32
    return (jnp.dot(tri, x1, preferred_element_type=f32)
            + (jnp.dot(tri, x2, preferred_element_type=f32) + jnp.dot(tri, x3, preferred_element_type=f32)))


def _mm_kernel(x_ref, w_ref, o_ref):
    o_ref[...] = jnp.dot(x_ref[...].astype(jnp.bfloat16), w_ref[...].astype(jnp.bfloat16),
                         preferred_element_type=jnp.float32).astype(o_ref.dtype)


def _matmul(x, w, layer, tm, tn):
    m, k = x.shape
    n = w.shape[2]
    return pl.pallas_call(
        _mm_kernel,
        grid=(m // tm, n // tn),
        in_specs=[pl.BlockSpec((tm, k), lambda i, j: (i, 0)),
                  pl.BlockSpec((None, k, tn), lambda i, j: (layer, 0, j))],
        out_specs=pl.BlockSpec((tm, tn), lambda i, j: (i, j)),
        out_shape=jax.ShapeDtypeStruct((m, n), jnp.float32),
        compiler_params=_params("arbitrary", "arbitrary"),
        name="matmul",
    )(x, w)


def _modulated_norm(x_ref, g_ref, sc_ref, sh_ref, store):
    g = g_ref[...]
    sc = 1.0 + sc_ref[...]
    sh = sh_ref[...]

    def body(r, carry):
        rows = pl.ds(pl.multiple_of(r * NORM_ROWS, NORM_ROWS), NORM_ROWS)
        x = x_ref[rows, :]
        y = x * lax.rsqrt(jnp.mean(x * x, axis=-1, keepdims=True) + NORM_EPS)
        store(rows, (y * g) * sc + sh)
        return carry

    lax.fori_loop(0, x_ref.shape[0] // NORM_ROWS, body, 0)


def _norm_mm_kernel(x_ref, g_ref, sc_ref, sh_ref, w_ref, wdt_ref, o_ref, dt_ref, h_ref):
    @pl.when(pl.program_id(1) == 0)
    def _():
        def store(rows, h):
            h_ref[rows, :] = h.astype(jnp.bfloat16)
        _modulated_norm(x_ref, g_ref, sc_ref, sh_ref, store)
        dt_ref[...] = jnp.dot(h_ref[...], wdt_ref[...], preferred_element_type=jnp.float32)

    o_ref[...] = jnp.dot(h_ref[...], w_ref[...], preferred_element_type=jnp.float32)


def _norm_matmul(x, gain, scale, shift, w, w_dt):
    mod_spec = pl.BlockSpec((None, 1, D_MODEL), lambda i, j: (_mod_row(i), 0, 0))
    return pl.pallas_call(
        _norm_mm_kernel,
        grid=(N_TOK // ROW_TILE, PROJ_DIM // PROJ_TILE),
        in_specs=[pl.BlockSpec((ROW_TILE, D_MODEL), lambda i, j: (i, 0)),
                  pl.BlockSpec((1, D_MODEL), lambda i, j: (0, 0)),
                  mod_spec, mod_spec,
                  pl.BlockSpec((D_MODEL, PROJ_TILE), lambda i, j: (0, j)),
                  pl.BlockSpec((D_MODEL, LANES), lambda i, j: (0, 0))],
        out_specs=[pl.BlockSpec((ROW_TILE, PROJ_TILE), lambda i, j: (i, j)),
                   pl.BlockSpec((ROW_TILE, LANES), lambda i, j: (i, 0))],
        out_shape=[jax.ShapeDtypeStruct((N_TOK, PROJ_DIM), jnp.float32),
                   jax.ShapeDtypeStruct((N_TOK, LANES), jnp.float32)],
        scratch_shapes=[pltpu.VMEM((ROW_TILE, D_MODEL), jnp.bfloat16)],
        compiler_params=_params("arbitrary", "arbitrary"),
        name="norm_in_proj",
    )(x, gain, scale, shift, w, w_dt)


def _conv_kernel(prev_ref, x_ref, next_ref, w_ref, b_ref, o_ref, ext_ref, *, tiles_per_seq):
    i = pl.program_id(0)
    t = x_ref.shape[0]
    pad = SSD_CONV // 2
    first = (i % tiles_per_seq) == 0
    last = (i % tiles_per_seq) == tiles_per_seq - 1
    ext_ref[0:SUBLANES, :] = jnp.where(first, 0.0, prev_ref[...])
    ext_ref[SUBLANES:SUBLANES + t, :] = x_ref[...]
    ext_ref[SUBLANES + t:2 * SUBLANES + t, :] = jnp.where(last, 0.0, next_ref[...])
    for r0 in range(0, t, CONV_SUB):
        y = jnp.broadcast_to(b_ref[...], (CONV_SUB, CONV_COLS))
        for j in range(SSD_CONV):
            start = SUBLANES - pad + j + r0
            y = y + ext_ref[start:start + CONV_SUB, :] * w_ref[j:j + 1, :]
        o_ref[r0:r0 + CONV_SUB, :] = y * jax.nn.sigmoid(y)


def _ssd_conv(proj, row0, n_rows, seq_len, w, b):
    t = min(seq_len, CONV_ROWS)
    r0 = row0 // t
    c0 = OFF_XBC // CONV_COLS
    sub = t // SUBLANES
    n_sub = proj.shape[0] // SUBLANES
    return pl.pallas_call(
        functools.partial(_conv_kernel, tiles_per_seq=seq_len // t),
        grid=(n_rows // t, SSD_CONV_DIM // CONV_COLS),
        in_specs=[pl.BlockSpec((SUBLANES, CONV_COLS), lambda i, j: (jnp.maximum((r0 + i) * sub - 1, 0), c0 + j)),
                  pl.BlockSpec((t, CONV_COLS), lambda i, j: (r0 + i, c0 + j)),
                  pl.BlockSpec((SUBLANES, CONV_COLS),
                               lambda i, j: (jnp.minimum((r0 + i + 1) * sub, n_sub - 1), c0 + j)),
                  pl.BlockSpec((SSD_CONV, CONV_COLS), lambda i, j: (0, j)),
                  pl.BlockSpec((1, CONV_COLS), lambda i, j: (0, j))],
        out_specs=pl.BlockSpec((t, CONV_COLS), lambda i, j: (i, j)),
        out_shape=jax.ShapeDtypeStruct((n_rows, SSD_CONV_DIM), jnp.float32),
        scratch_shapes=[pltpu.VMEM((t + 2 * SUBLANES, CONV_COLS), jnp.float32)],
        compiler_params=_params("arbitrary", "arbitrary"),
        name="ssd_conv",
    )(proj, proj, proj, w, b)


def _softplus(x):
    return jnp.maximum(x, 0.0) + jnp.log1p(jnp.exp(-jnp.abs(x)))


def _expand(xs, sel, terms):
    parts = [jnp.concatenate(_split3(x)[:terms], axis=1) for x in xs]
    out = jnp.dot(jnp.concatenate(parts, axis=0), sel, preferred_element_type=jnp.float32)
    rows = xs[0].shape[0]
    return [out[i * rows:(i + 1) * rows] for i in range(len(xs))]


def _ssd_kernel(xf_ref, xb_ref, dtf_ref, dtb_ref, dtbias_ref, aneg_ref, init_ref, yf_ref, yb_ref, st_ref,
                s_ref, *, nc):
    bf16, f32 = jnp.bfloat16, jnp.float32
    c = SSD_C
    hd, ns = SSD_HEAD_DIM, SSD_STATE
    pair_w = 2 * hd
    j = pl.program_id(1)

    @pl.when(j == 0)
    def _():
        s_ref[...] = init_ref[...]

    t_ids = lax.broadcasted_iota(jnp.int32, (c, c), 0)
    s_ids = lax.broadcasted_iota(jnp.int32, (c, c), 1)
    eye = (lax.broadcasted_iota(jnp.int32, (SSD_HEADS, SSD_HEADS), 0)
           == lax.broadcasted_iota(jnp.int32, (SSD_HEADS, SSD_HEADS), 1)).astype(bf16)
    head_of = lambda terms, n, w: (lax.broadcasted_iota(jnp.int32, (terms * SSD_HEADS, n), 1) // w
                                   == lax.broadcasted_iota(jnp.int32, (terms * SSD_HEADS, n), 0) % SSD_HEADS
                                   ).astype(bf16)
    sel_x = head_of(2, SSD_DIM, hd)
    sel_c = head_of(3, SSD_HEADS * c, c)
    low_lanes = lax.broadcasted_iota(jnp.int32, (c, pair_w), 1) < hd
    low_rows = lax.broadcasted_iota(jnp.int32, (pair_w, ns), 0) < hd

    for d, (x_ref, dt_ref, y_ref) in enumerate(((xf_ref, dtf_ref, yf_ref), (xb_ref, dtb_ref, yb_ref))):
        rev = d == 1
        causal = (s_ids >= t_ids) if rev else (s_ids <= t_ids)
        tri = causal.astype(bf16)
        dt = _softplus(dt_ref[:, d * SSD_HEADS:(d + 1) * SSD_HEADS] + dtbias_ref[d:d + 1, :])
        acs = _tri_cumsum(tri, dt * aneg_ref[d:d + 1, :])
        acs_t = sum(lax.dot_general(eye, part, _NT, preferred_element_type=f32) for part in _split3(acs))
        end = acs[0:1, :] if rev else acs[c - 1:c, :]
        dt_x, out_x, in_x = _expand([dt, jnp.exp(end - acs), jnp.exp(acs)], sel_x, 2)
        acs_c, = _expand([acs], sel_c, 3)
        end_decay = jnp.exp(end)
        for g in range(SSD_GROUPS):
            bg = x_ref[:, SSD_DIM + g * ns:SSD_DIM + (g + 1) * ns].astype(bf16)
            cg = x_ref[:, SSD_DIM + (SSD_GROUPS + g) * ns:SSD_DIM + (SSD_GROUPS + g + 1) * ns].astype(bf16)
            cb = lax.dot_general(cg, bg, _NT, preferred_element_type=f32)
            for p in range(g * 2, g * 2 + 2):
                lanes = slice(p * pair_w, (p + 1) * pair_w)
                scores = []
                for h in (2 * p, 2 * p + 1):
                    decay = jnp.exp(jnp.minimum(acs_c[:, h * c:(h + 1) * c] - acs_t[h:h + 1, :], 0.0))
                    scores.append(jnp.where(causal, cb * decay, 0.0).astype(bf16))
                xdt = x_ref[:, lanes] * dt_x[:, lanes]
                rhs = jnp.concatenate([jnp.where(low_lanes, xdt, 0.0), jnp.where(low_lanes, 0.0, xdt)],
                                      axis=0).astype(bf16)
                s_p = s_ref[d, p]
                y = (jnp.dot(jnp.concatenate(scores, axis=1), rhs, preferred_element_type=f32)
                     + lax.dot_general(cg, s_p.astype(bf16), _NT, preferred_element_type=f32) * in_x[:, lanes])
                y_ref[:, lanes] = y
                keep = jnp.where(low_rows, end_decay[:, 2 * p:2 * p + 1], end_decay[:, 2 * p + 1:2 * p + 2])
                s_ref[d, p] = s_p * keep + lax.dot_general((xdt * out_x[:, lanes]).astype(bf16), bg, _TN,
                                                           preferred_element_type=f32)

    @pl.when(j == nc - 1)
    def _():
        st_ref[...] = s_ref[...]


def _ssd_scan(xbc, dt_logits, row0, nb, length, dt_bias, a_neg, init):
    c = SSD_C
    nc = length // c
    r0 = row0 // c
    pair_state = (nb, 2, SSD_HEADS // 2, 2 * SSD_HEAD_DIM, SSD_STATE)
    state_spec = pl.BlockSpec((None,) + pair_state[1:], lambda b, j: (b, 0, 0, 0, 0))
    y_f, y_b, states = pl.pallas_call(
        functools.partial(_ssd_kernel, nc=nc),
        grid=(nb, nc),
        in_specs=[pl.BlockSpec((c, SSD_CONV_DIM), lambda b, j: (b * nc + j, 0)),
                  pl.BlockSpec((c, SSD_CONV_DIM), lambda b, j: (b * nc + nc - 1 - j, 0)),
                  pl.BlockSpec((c, LANES), lambda b, j: (r0 + b * nc + j, 0)),
                  pl.BlockSpec((c, LANES), lambda b, j: (r0 + b * nc + nc - 1 - j, 0)),
                  pl.BlockSpec((2, SSD_HEADS), lambda b, j: (0, 0)),
                  pl.BlockSpec((2, SSD_HEADS), lambda b, j: (0, 0)),
                  state_spec],
        out_specs=[pl.BlockSpec((c, SSD_DIM), lambda b, j: (b * nc + j, 0)),
                   pl.BlockSpec((c, SSD_DIM), lambda b, j: (b * nc + nc - 1 - j, 0)),
                   state_spec],
        out_shape=[jax.ShapeDtypeStruct((nb * length, SSD_DIM), jnp.float32),
                   jax.ShapeDtypeStruct((nb * length, SSD_DIM), jnp.float32),
                   jax.ShapeDtypeStruct(pair_state, jnp.float32)],
        scratch_shapes=[pltpu.VMEM(pair_state[1:], jnp.float32)],
        compiler_params=_params("arbitrary", "arbitrary"),
        name="ssd_scan",
    )(xbc, xbc, dt_logits, dt_logits, dt_bias, a_neg, init.reshape(pair_state))
    return y_f, y_b, states.reshape(nb, 2, SSD_HEADS, SSD_HEAD_DIM, SSD_STATE)


def _ssd_finish_kernel(yf_ref, yb_ref, x_ref, z_ref, d_ref, g_ref, o_ref):
    z = z_ref[...]
    y = (yf_ref[...] + yb_ref[...] + d_ref[...] * x_ref[...]) * (z * jax.nn.sigmoid(z))
    o_ref[...] = (y * lax.rsqrt(jnp.mean(y * y, axis=-1, keepdims=True) + NORM_EPS) * g_ref[...]).astype(o_ref.dtype)


def _ssd_finish(y_f, y_b, xbc, proj, row0, d_row, gain):
    n = y_f.shape[0]
    t = FINISH_ROWS
    r0 = row0 // t
    row_spec = pl.BlockSpec((t, SSD_DIM), lambda i: (i, 0))
    vec_spec = pl.BlockSpec((1, SSD_DIM), lambda i: (0, 0))
    return pl.pallas_call(
        _ssd_finish_kernel,
        grid=(n // t,),
        in_specs=[row_spec, row_spec, row_spec,
                  pl.BlockSpec((t, SSD_DIM), lambda i: (r0 + i, OFF_Z // SSD_DIM)),
                  vec_spec, vec_spec],
        out_specs=row_spec,
        out_shape=jax.ShapeDtypeStruct((n, SSD_DIM), jnp.bfloat16),
        compiler_params=_params("arbitrary"),
        name="ssd_finish",
    )(y_f, y_b, xbc, proj, d_row, gain)


def _hgrn_chunk(q, g, kk, v, state, rev, tri, lane_mod, diag, level_masks, b_ref, kk_ref):
    bf16, f32 = jnp.bfloat16, jnp.float32
    c = q.shape[0]
    ng = c // SUBLANES
    b = _tri_cumsum(tri, g)
    yield
    b_ref[...] = b
    kk_ref[...] = kk
    row = lambda a_ref, r: a_ref[r:r + 1, :]
    grp = lambda a, i: a[i * SUBLANES:(i + 1) * SUBLANES, :]
    b_end = row(b_ref, 0) if rev else row(b_ref, c - 1)

    ones = jnp.ones((HG_KDIM, c), bf16)
    diag_rows = []
    for i0 in range(0, ng, DIAG_BATCH):
        tiles = []
        for i in range(i0, i0 + DIAG_BATCH):
            qg, bg = grp(q, i), grp(b, i)
            tiles += [qg * jnp.exp(bg - row(b_ref, i * SUBLANES + j)) * row(kk_ref, i * SUBLANES + j)
                      for j in range(SUBLANES)]
        sums = jnp.dot(jnp.concatenate(tiles, axis=0).astype(bf16), ones, preferred_element_type=f32)
        for n in range(DIAG_BATCH):
            base = n * SUBLANES * SUBLANES
            acc = sums[base:base + SUBLANES, :]
            for j in range(1, SUBLANES):
                acc = jnp.where(lane_mod[j], sums[base + j * SUBLANES:base + (j + 1) * SUBLANES, :], acc)
            diag_rows.append(acc)
        yield
    att = jnp.where(diag, jnp.concatenate(diag_rows, axis=0), 0.0)

    for m, mask in zip(HG_LEVELS, level_masks):
        half = m // 2
        q_side, k_side = [], []
        for i in range(ng):
            start = (i * SUBLANES) // m * m
            later = (i * SUBLANES) % m >= half
            ref = row(b_ref, start + half if rev else start + half - 1)
            if later != rev:
                q_side.append(grp(q, i) * jnp.exp(grp(b, i) - ref))
                k_side.append(jnp.zeros((SUBLANES, HG_KDIM), f32))
            else:
                q_side.append(jnp.zeros((SUBLANES, HG_KDIM), f32))
                k_side.append(grp(kk, i) * jnp.exp(ref - grp(b, i)))
        a_m = lax.dot_general(jnp.concatenate(q_side, axis=0).astype(bf16),
                              jnp.concatenate(k_side, axis=0).astype(bf16), _NT, preferred_element_type=f32)
        att = att + (a_m if m == c else jnp.where(mask, a_m, 0.0))
        yield

    q_in = (q * jnp.exp(b)).astype(bf16)
    k_out = (kk * jnp.exp(b_end - b)).astype(bf16)
    yield
    s_t = state[0]
    o = (lax.dot_general(q_in, s_t.astype(bf16), _NT, preferred_element_type=f32)
         + jnp.dot(att.astype(bf16), v.astype(bf16), preferred_element_type=f32))
    state[0] = s_t * jnp.exp(b_end) + lax.dot_general(v.astype(bf16), k_out, _TN, preferred_element_type=f32)
    return o


def _in_lockstep(gens):
    results = [None] * len(gens)
    live = list(range(len(gens)))
    while live:
        for i in list(live):
            try:
                next(gens[i])
            except StopIteration as stop:
                results[i] = stop.value
                live.remove(i)
    return results


def _hgrn_kernel(q_ref, ff_ref, fb_ref, v_ref, gate_ref, lb_ref, gn_ref, init_ref, o_ref, st_ref,
                 acc_ref, s_ref, b_ref, kk_ref, *, nc):
    c = HG_C
    t_ids = lax.broadcasted_iota(jnp.int32, (c, c), 0)
    s_ids = lax.broadcasted_iota(jnp.int32, (c, c), 1)
    causal = (s_ids <= t_ids, s_ids >= t_ids)
    tri = tuple(m.astype(jnp.bfloat16) for m in causal)
    same_group = (t_ids // SUBLANES) == (s_ids // SUBLANES)
    diag = tuple(m & same_group for m in causal)
    lane_mod = [(s_ids[:SUBLANES] % SUBLANES) == j for j in range(SUBLANES)]

    def level_mask(m, rev):
        same = (t_ids // m) == (s_ids // m)
        t_late = (t_ids % m) >= m // 2
        s_late = (s_ids % m) >= m // 2
        return same & (t_late != s_late) & (t_late != rev)
    masks = tuple([level_mask(m, rev) for m in HG_LEVELS] for rev in (False, True))

    heads = init_ref.shape[1]
    for a in range(heads):
        for d in (0, 1):
            s_ref[2 * a + d] = init_ref[d, a].T

    per_step = HG_LOCKSTEP // (2 * heads)
    if (nc // 2) % per_step:
        per_step = 1
    half_steps = nc // 2 // per_step

    def run_step(j):
        states = [[s_ref[i]] for i in range(2 * heads)]
        where, gens = [], []
        for u in range(per_step):
            for a in range(heads):
                lanes = slice(a * HG_KDIM, (a + 1) * HG_KDIM)
                for d in (0, 1):
                    chunk = j * per_step + u if d == 0 else nc - 1 - (j * per_step + u)
                    r = pl.ds(pl.multiple_of(chunk * c, c), c)
                    x = q_ref[r, lanes]
                    q = x * jax.nn.sigmoid(x)
                    lb = lb_ref[d:d + 1, lanes]
                    f = lb + (1.0 - lb) * jax.nn.sigmoid((ff_ref, fb_ref)[d][r, lanes])
                    slot = (u * heads + a) * 2 + d
                    where.append((r, lanes))
                    gens.append(_hgrn_chunk(q, jnp.log(f), 1.0 - f, v_ref[r, lanes], states[2 * a + d], d == 1,
                                            tri[d], lane_mod, diag[d], masks[d], b_ref.at[slot], kk_ref.at[slot]))
        outs = _in_lockstep(gens)
        for i in range(2 * heads):
            s_ref[i] = states[i][0]
        return where, outs

    def first_half(j, carry):
        where, outs = run_step(j)
        for (r, lanes), o in zip(where, outs):
            acc_ref[r, lanes] = o
        return carry

    def second_half(j, carry):
        where, outs = run_step(j)
        for (r, lanes), o in zip(where, outs):
            o = o + acc_ref[r, lanes]
            y = o * lax.rsqrt(jnp.mean(o * o, axis=-1, keepdims=True) + NORM_EPS) * gn_ref[:, lanes]
            gate = gate_ref[r, lanes]
            o_ref[r, lanes] = (y * (gate * jax.nn.sigmoid(gate))).astype(o_ref.dtype)
        return carry

    lax.fori_loop(0, half_steps, first_half, 0)
    lax.fori_loop(half_steps, 2 * half_steps, second_half, 0)
    for a in range(heads):
        for d in (0, 1):
            st_ref[d, a] = s_ref[2 * a + d].T


def _hgrn_mixer(proj, row0, nb, length, lb, gain, init, heads):
    nc = length // HG_C
    assert nc % 2 == 0
    width = heads * HG_KDIM
    seq = lambda off: pl.BlockSpec((length, width), lambda b, h, col=off // width: (row0 + b, col + h))
    state_spec = pl.BlockSpec((None, 2, heads, HG_KDIM, HG_VDIM), lambda b, h: (b, 0, h, 0, 0))
    return pl.pallas_call(
        functools.partial(_hgrn_kernel, nc=nc),
        grid=(nb, HG_HEADS // heads),
        in_specs=[seq(OFF_HQ), seq(OFF_HF), seq(OFF_HF + HG_FDIM), seq(OFF_HI), seq(OFF_HG),
                  pl.BlockSpec((2, width), lambda b, h: (0, h)),
                  pl.BlockSpec((1, width), lambda b, h: (0, h)),
                  state_spec],
        out_specs=[pl.BlockSpec((length, width), lambda b, h: (b, h)), state_spec],
        out_shape=[jax.ShapeDtypeStruct((nb * length, HG_DIM), jnp.bfloat16),
                   jax.ShapeDtypeStruct((nb, 2, HG_HEADS, HG_KDIM, HG_VDIM), jnp.float32)],
        scratch_shapes=[pltpu.VMEM((length, width), jnp.float32),
                        pltpu.VMEM((2 * heads, HG_VDIM, HG_KDIM), jnp.float32),
                        pltpu.VMEM((HG_LOCKSTEP, HG_C, HG_KDIM), jnp.float32),
                        pltpu.VMEM((HG_LOCKSTEP, HG_C, HG_KDIM), jnp.float32)],
        compiler_params=_params("arbitrary", "arbitrary"),
        name="hgrn_mixer",
    )(proj, proj, proj, proj, proj, lb, gain, init)


def _softmax_av(scores, values):
    f32, bf16 = jnp.float32, jnp.bfloat16
    m = functools.reduce(jnp.maximum, [jnp.max(s, axis=-1, keepdims=True) for s in scores])
    ps = [jnp.exp(s - m) for s in scores]
    denom = functools.reduce(jnp.add, [jnp.sum(p, axis=-1, keepdims=True) for p in ps])
    acc = functools.reduce(jnp.add, [jnp.dot(p.astype(bf16), v, preferred_element_type=f32)
                                     for p, v in zip(ps, values)])
    return acc / denom


def _ctx_attn_kernel(q_ref, k_ref, v_ref, nk_in_ref, nv_in_ref, o_ref, nk_ref, nv_ref):
    bf16 = jnp.bfloat16
    scale = ATT_HEAD_DIM ** -0.5
    k = k_ref[...]
    v = v_ref[...]
    nk_ref[...] = k
    nv_ref[...] = v
    s = lax.dot_general(q_ref[...].astype(bf16), k.astype(bf16), _NT, preferred_element_type=jnp.float32) * scale
    o_ref[...] = _softmax_av([s], [v.astype(bf16)]).astype(o_ref.dtype)


def _context_attention(proj, nb, length, new_k, new_v, layer):
    spec = lambda off: pl.BlockSpec((length, ATT_HEAD_DIM), lambda b, h, col=off // ATT_HEAD_DIM: (b, col + h))
    cache_spec = pl.BlockSpec((None, None, length, ATT_HEAD_DIM), lambda b, h: (b, layer, 0, h))
    any_spec = pl.BlockSpec(memory_space=pl.ANY)
    return pl.pallas_call(
        _ctx_attn_kernel,
        grid=(nb, ATT_HEADS),
        in_specs=[spec(OFF_AQ), spec(OFF_AK), spec(OFF_AV), any_spec, any_spec],
        out_specs=[pl.BlockSpec((length, ATT_HEAD_DIM), lambda b, h: (b, h)), cache_spec, cache_spec],
        out_shape=[jax.ShapeDtypeStruct((nb * length, ATT_DIM), jnp.bfloat16),
                   jax.ShapeDtypeStruct(new_k.shape, new_k.dtype),
                   jax.ShapeDtypeStruct(new_v.shape, new_v.dtype)],
        input_output_aliases={3: 1, 4: 2},
        compiler_params=_params("arbitrary", "arbitrary"),
        name="context_attention",
    )(proj, proj, proj, new_k, new_v)


def _window_bias(rpb):
    col = jnp.arange(GRID_W)
    cs = jnp.clip(col - WIN_COLS // 2, 0, GRID_W - WIN_COLS)
    col_mask = (col[None, :] >= cs[:, None]) & (col[None, :] < cs[:, None] + WIN_COLS)
    dc_idx = jnp.clip(col[None, :] - col[:, None] + WIN_COLS - 1, 0, 2 * WIN_COLS - 2)
    bias = jnp.where(col_mask, rpb[:, :, dc_idx].astype(jnp.float32), MASKED)
    wins = [bias[:, d0:d0 + WIN_ROWS].transpose(0, 2, 1, 3).reshape(rpb.shape[0], GRID_W, WIN_ROWS * GRID_W)
            for d0 in range(WIN_ROWS)]
    return jnp.stack(wins, axis=1)


def _natten_kernel(q_ref, k_ref, v_ref, kc_ref, vc_ref, bias_ref, o_ref, kb_ref, vb_ref, *, rows):
    bf16, f32 = jnp.bfloat16, jnp.float32
    scale = ATT_HEAD_DIM ** -0.5
    win = WIN_ROWS * GRID_W

    def cast(i, carry):
        sl = pl.ds(pl.multiple_of(i * CAST_ROWS, CAST_ROWS), CAST_ROWS)
        kb_ref[sl, :] = k_ref[sl, :].astype(bf16)
        vb_ref[sl, :] = v_ref[sl, :].astype(bf16)
        return carry
    lax.fori_loop(0, rows * GRID_W // CAST_ROWS, cast, 0)

    kc = kc_ref[...].astype(bf16)
    vc = vc_ref[...].astype(bf16)

    def row_block(r):
        rs = jnp.clip(r - WIN_ROWS // 2, 0, rows - WIN_ROWS)
        d0 = rs - r + WIN_ROWS - 1
        q = q_ref[pl.ds(pl.multiple_of(r * GRID_W, GRID_W), GRID_W), :].astype(bf16)
        keys = pl.ds(pl.multiple_of(rs * GRID_W, GRID_W), win)
        s_lat = lax.dot_general(q, kb_ref[keys, :], _NT, preferred_element_type=f32) * scale + bias_ref[d0]
        s_ctx = lax.dot_general(q, kc, _NT, preferred_element_type=f32) * scale
        yield
        m = jnp.maximum(jnp.max(s_lat, axis=-1, keepdims=True), jnp.max(s_ctx, axis=-1, keepdims=True))
        yield
        p_lat = jnp.exp(s_lat - m)
        p_ctx = jnp.exp(s_ctx - m)
        denom = jnp.sum(p_lat, axis=-1, keepdims=True) + jnp.sum(p_ctx, axis=-1, keepdims=True)
        acc = (jnp.dot(p_lat.astype(bf16), vb_ref[keys, :], preferred_element_type=f32)
               + jnp.dot(p_ctx.astype(bf16), vc, preferred_element_type=f32))
        yield
        o_ref[pl.ds(pl.multiple_of(r * GRID_W, GRID_W), GRID_W), :] = (acc / denom).astype(o_ref.dtype)

    def row_group(g, carry):
        _in_lockstep([row_block(g * ROW_UNROLL + u) for u in range(ROW_UNROLL)])
        return carry
    lax.fori_loop(0, rows // ROW_UNROLL, row_group, 0)


def _neighbourhood_attention(proj, row0, nb, length, cache_k, cache_v, layer, bias_win):
    rows = length // GRID_W
    past = cache_k.shape[2]
    spec = lambda off: pl.BlockSpec((length, ATT_HEAD_DIM),
                                    lambda b, h, col=off // ATT_HEAD_DIM: (row0 + b, col + h))
    cache_spec = pl.BlockSpec((None, None, past, ATT_HEAD_DIM), lambda b, h: (b, layer, 0, h))
    return pl.pallas_call(
        functools.partial(_natten_kernel, rows=rows),
        grid=(nb, ATT_HEADS),
        in_specs=[spec(OFF_AQ), spec(OFF_AK), spec(OFF_AV), cache_spec, cache_spec,
                  pl.BlockSpec((None, WIN_ROWS, GRID_W, WIN_ROWS * GRID_W), lambda b, h: (h, 0, 0, 0))],
        out_specs=pl.BlockSpec((length, ATT_HEAD_DIM), lambda b, h: (b, h)),
        out_shape=jax.ShapeDtypeStruct((nb * length, ATT_DIM), jnp.bfloat16),
        scratch_shapes=[pltpu.VMEM((length, ATT_HEAD_DIM), jnp.bfloat16),
                        pltpu.VMEM((length, ATT_HEAD_DIM), jnp.bfloat16)],
        compiler_params=_params("arbitrary", "arbitrary"),
        name="neighbourhood_attention",
    )(proj, proj, proj, cache_k, cache_v, bias_win)


def _merge_kernel(ca_ref, cb_ref, cc_ref, la_ref, lb_ref, lc_ref, wa_ref, wb_ref, wc_ref,
                  ga_ref, gb_ref, gc_ref, o_ref):
    f32 = jnp.float32
    is_ctx = pl.program_id(0) < N_CTX // ROW_TILE

    def merge(ya_ref, yb_ref, yc_ref):
        acc = jax.nn.sigmoid(ga_ref[...]) * jnp.dot(ya_ref[...], wa_ref[...], preferred_element_type=f32)
        acc = acc + jax.nn.sigmoid(gb_ref[...]) * jnp.dot(yb_ref[...], wb_ref[...], preferred_element_type=f32)
        acc = acc + jax.nn.sigmoid(gc_ref[...]) * jnp.dot(yc_ref[...], wc_ref[...], preferred_element_type=f32)
        o_ref[...] = acc.astype(o_ref.dtype)

    @pl.when(is_ctx)
    def _():
        merge(ca_ref, cb_ref, cc_ref)

    @pl.when(jnp.logical_not(is_ctx))
    def _():
        merge(la_ref, lb_ref, lc_ref)


def _branch_merge(ys_ctx, ys_lat, ws, proj):
    tm, tn = ROW_TILE, MERGE_COLS
    kdim = ws[0].shape[0]
    cg = OFF_GATES // tn
    per = D_MODEL // tn
    ctx_tiles = N_CTX // tm
    ctx_spec = pl.BlockSpec((tm, kdim), lambda i, j: (jnp.minimum(i, ctx_tiles - 1), 0))
    lat_spec = pl.BlockSpec((tm, kdim), lambda i, j: (jnp.maximum(i - ctx_tiles, 0), 0))
    w_spec = pl.BlockSpec((kdim, tn), lambda i, j: (0, j))
    g_spec = lambda b: pl.BlockSpec((tm, tn), lambda i, j, b=b: (i, cg + b * per + j))
    return pl.pallas_call(
        _merge_kernel,
        grid=(N_TOK // tm, D_MODEL // tn),
        in_specs=[ctx_spec] * N_BRANCH + [lat_spec] * N_BRANCH + [w_spec] * N_BRANCH
                 + [g_spec(b) for b in range(N_BRANCH)],
        out_specs=pl.BlockSpec((tm, tn), lambda i, j: (i, j)),
        out_shape=jax.ShapeDtypeStruct((N_TOK, D_MODEL), jnp.bfloat16),
        compiler_params=_params("arbitrary", "arbitrary"),
        name="branch_merge",
    )(*ys_ctx, *ys_lat, *ws, proj, proj, proj)


def _out_residual_kernel(m_ref, w_ref, x_ref, gate_ref, o_ref):
    o_ref[...] = x_ref[...] + gate_ref[...] * jnp.dot(m_ref[...], w_ref[...], preferred_element_type=jnp.float32)


def _out_residual(merged, w, x, gate):
    tm, tn = ROW_TILE, COL_TILE
    return pl.pallas_call(
        _out_residual_kernel,
        grid=(N_TOK // tm, D_MODEL // tn),
        in_specs=[pl.BlockSpec((tm, D_MODEL), lambda i, j: (i, 0)),
                  pl.BlockSpec((D_MODEL, tn), lambda i, j: (0, j)),
                  pl.BlockSpec((tm, tn), lambda i, j: (i, j)),
                  pl.BlockSpec((None, 1, tn), lambda i, j: (_mod_row(i), 0, j))],
        out_specs=pl.BlockSpec((tm, tn), lambda i, j: (i, j)),
        out_shape=jax.ShapeDtypeStruct((N_TOK, D_MODEL), jnp.float32),
        compiler_params=_params("arbitrary", "arbitrary"),
        name="out_residual",
    )(merged, w, x, gate)


def _first_max(vals):
    best, idx = vals[0], jnp.zeros(vals[0].shape, jnp.int32)
    for k in range(1, len(vals)):
        better = vals[k] > best
        best = jnp.where(better, vals[k], best)
        idx = jnp.where(better, k, idx)
    return best, idx


def _pick(idx, vals):
    out = vals[0]
    for k in range(1, len(vals)):
        out = jnp.where(idx == k, vals[k], out)
    return out


def _norm_router_kernel(x_ref, g_ref, sc_ref, sh_ref, wr_hi_ref, wr_lo_ref, rb_ref,
                        h_ref, ids_ref, wts_ref, cnt_ref, lo_ref, base_ref):
    f32, bf16 = jnp.float32, jnp.bfloat16
    t = x_ref.shape[0]

    @pl.when(pl.program_id(0) == 0)
    def _():
        base_ref[...] = jnp.zeros_like(base_ref)

    def store(rows, h):
        h_hi = h.astype(bf16)
        h_ref[rows, :] = h_hi
        lo_ref[rows, :] = (h - h_hi.astype(f32)).astype(bf16)
    _modulated_norm(x_ref, g_ref, sc_ref, sh_ref, store)

    logits = (jnp.dot(h_ref[...], wr_hi_ref[...], preferred_element_type=f32)
              + (jnp.dot(h_ref[...], wr_lo_ref[...], preferred_element_type=f32)
                 + jnp.dot(lo_ref[...], wr_hi_ref[...], preferred_element_type=f32)))
    logits = logits.T[:N_EXPERTS]
    scores = jax.nn.sigmoid(logits)
    sel = scores + rb_ref[...]
    row = lambda a, e: a[e:e + 1, :]

    group_scores = []
    for g in range(N_EXPERT_GROUPS):
        v = [row(sel, g * EXPERTS_PER_GROUP + k) for k in range(EXPERTS_PER_GROUP)]
        pair_sums = [v[a] + v[b] for a in range(EXPERTS_PER_GROUP) for b in range(a + 1, EXPERTS_PER_GROUP)]
        group_scores.append(functools.reduce(jnp.maximum, pair_sums))
    _, grp = _first_max(group_scores)

    in_sel = [_pick(grp, [row(sel, g * EXPERTS_PER_GROUP + k) for g in range(N_EXPERT_GROUPS)])
              for k in range(EXPERTS_PER_GROUP)]
    in_score = [_pick(grp, [row(scores, g * EXPERTS_PER_GROUP + k) for g in range(N_EXPERT_GROUPS)])
                for k in range(EXPERTS_PER_GROUP)]
    _, i1 = _first_max(in_sel)
    _, i2 = _first_max([jnp.where(i1 == k, -jnp.inf, in_sel[k]) for k in range(EXPERTS_PER_GROUP)])
    s1 = _pick(i1, in_score)
    s2 = _pick(i2, in_score)
    e1 = grp * EXPERTS_PER_GROUP + i1
    e2 = grp * EXPERTS_PER_GROUP + i2

    e_ids = lax.broadcasted_iota(jnp.int32, (N_EXPERTS, t), 0)
    hit1 = e_ids == e1
    hit2 = e_ids == e2
    cnt = jnp.where(hit1 | hit2, 1.0, 0.0).astype(bf16)
    before = (lax.broadcasted_iota(jnp.int32, (t, t), 0) < lax.broadcasted_iota(jnp.int32, (t, t), 1)).astype(bf16)
    prefix = jnp.dot(cnt, before, preferred_element_type=f32) + base_ref[:, 0:1]
    rank1 = jnp.sum(jnp.where(hit1, prefix, 0.0), axis=0, keepdims=True)
    rank2 = jnp.sum(jnp.where(hit2, prefix, 0.0), axis=0, keepdims=True)
    base_ref[...] = base_ref[...] + jnp.dot(cnt, jnp.ones((t, LANES), bf16), preferred_element_type=f32)
    cnt_ref[...] = base_ref[...]

    zeros = jnp.zeros((SUBLANES - 4, t), jnp.int32)
    ids_ref[...] = jnp.concatenate([e1, e2, rank1.astype(jnp.int32), rank2.astype(jnp.int32), zeros], axis=0)
    total = s1 + s2
    wts_ref[...] = jnp.concatenate([s1 / total, s2 / total, jnp.zeros((SUBLANES - 2, t), f32)], axis=0)


def _norm_router(x, gain, scale, shift, wr_hi, wr_lo, router_bias):
    t = ROUTER_TILE
    per = ROW_TILE // t
    mod_spec = pl.BlockSpec((None, 1, D_MODEL), lambda i: (_mod_row(i // per), 0, 0))
    return pl.pallas_call(
        _norm_router_kernel,
        grid=(N_TOK // t,),
        in_specs=[pl.BlockSpec((t, D_MODEL), lambda i: (i, 0)),
                  pl.BlockSpec((1, D_MODEL), lambda i: (0, 0)),
                  mod_spec, mod_spec,
                  pl.BlockSpec((D_MODEL, LANES), lambda i: (0, 0)),
                  pl.BlockSpec((D_MODEL, LANES), lambda i: (0, 0)),
                  pl.BlockSpec((N_EXPERTS, 1), lambda i: (0, 0))],
        out_specs=[pl.BlockSpec((t, D_MODEL), lambda i: (i, 0)),
                   pl.BlockSpec((SUBLANES, t), lambda i: (0, i)),
                   pl.BlockSpec((SUBLANES, t), lambda i: (0, i)),
                   pl.BlockSpec((N_EXPERTS, LANES), lambda i: (0, 0))],
        out_shape=[jax.ShapeDtypeStruct((N_TOK, D_MODEL), jnp.bfloat16),
                   jax.ShapeDtypeStruct((SUBLANES, N_TOK), jnp.int32),
                   jax.ShapeDtypeStruct((SUBLANES, N_TOK), jnp.float32),
                   jax.ShapeDtypeStruct((N_EXPERTS, LANES), jnp.float32)],
        scratch_shapes=[pltpu.VMEM((t, D_MODEL), jnp.bfloat16),
                        pltpu.VMEM((N_EXPERTS, LANES), jnp.float32)],
        compiler_params=_params("arbitrary"),
        name="norm_router",
    )(x, gain, scale, shift, wr_hi, wr_lo, router_bias)


def _expert_kernel(be_ref, na_ref, x_ref, w1_ref, w3_ref, w2_ref, o_ref):
    active = pl.program_id(0) < na_ref[0]

    @pl.when(active)
    def _():
        bf16 = jnp.bfloat16
        x = x_ref[...]
        a = jnp.dot(x, w1_ref[...].astype(bf16), preferred_element_type=jnp.float32)
        b = jnp.dot(x, w3_ref[...].astype(bf16), preferred_element_type=jnp.float32)
        hdn = (a * jax.nn.sigmoid(a)) * b
        o_ref[...] = jnp.dot(hdn.astype(bf16), w2_ref[...].astype(bf16),
                             preferred_element_type=jnp.float32).astype(o_ref.dtype)

    @pl.when(jnp.logical_not(active))
    def _():
        o_ref[...] = jnp.zeros_like(o_ref)


def _expert_blocks(buf, block_e, n_active, w1, w3, w2):
    n_blocks = buf.shape[0] // MOE_BLOCK
    w_spec = lambda r, c: pl.BlockSpec((None, r, c), lambda i, be, na: (be[i], 0, 0), pipeline_mode=pl.Buffered(1))
    grid_spec = pltpu.PrefetchScalarGridSpec(
        num_scalar_prefetch=2,
        grid=(n_blocks,),
        in_specs=[pl.BlockSpec((MOE_BLOCK, D_MODEL), lambda i, be, na: (i, 0)),
                  w_spec(D_MODEL, D_EXPERT), w_spec(D_MODEL, D_EXPERT), w_spec(D_EXPERT, D_MODEL)],
        out_specs=pl.BlockSpec((MOE_BLOCK, D_MODEL), lambda i, be, na: (i, 0)),
    )
    return pl.pallas_call(
        _expert_kernel,
        grid_spec=grid_spec,
        out_shape=jax.ShapeDtypeStruct((n_blocks * MOE_BLOCK, D_MODEL), jnp.float32),
        compiler_params=_params("arbitrary"),
        name="moe_experts",
    )(block_e, n_active, buf, w1, w3, w2)


def _row_copy(out_hbm, buf_ref, sem_ref, slot, k, t, row):
    return pltpu.make_async_copy(out_hbm.at[pl.ds(row, 1), :], buf_ref.at[slot, k, pl.ds(t, 1), :], sem_ref.at[slot])


def _combine_kernel(dcur_ref, dnext_ref, w_ref, x_ref, gate_ref, out_hbm, o_ref, buf_ref, sem_ref):
    t_rows = x_ref.shape[0]
    i = pl.program_id(0)
    n = pl.num_programs(0)
    slot = i % 2

    def issue(d_ref, s):
        for t in range(t_rows):
            for k in range(TOP_K):
                _row_copy(out_hbm, buf_ref, sem_ref, s, k, t, d_ref[k, t]).start()

    @pl.when(i == 0)
    def _():
        issue(dcur_ref, 0)

    @pl.when(i + 1 < n)
    def _():
        issue(dnext_ref, 1 - slot)

    def wait(t, carry):
        for k in range(TOP_K):
            _row_copy(out_hbm, buf_ref, sem_ref, slot, k, t, 0).wait()
        return carry
    lax.fori_loop(0, t_rows, wait, 0, unroll=COMBINE_UNROLL)

    w = w_ref[...]
    moe = w[:, 0:1] * buf_ref[slot, 0] + w[:, 1:2] * buf_ref[slot, 1]
    o_ref[...] = x_ref[...] + gate_ref[...] * moe


def _moe_combine(x, gate, out, dest, wts_t):
    t = COMBINE_ROWS
    steps = N_TOK // t
    per = ROW_TILE // t
    smem_spec = lambda shift: pl.BlockSpec((SUBLANES, t), lambda i: (0, jnp.minimum(i + shift, steps - 1)),
                                           memory_space=pltpu.SMEM)
    return pl.pallas_call(
        _combine_kernel,
        grid=(steps,),
        in_specs=[smem_spec(0), smem_spec(1),
                  pl.BlockSpec((t, LANES), lambda i: (i, 0)),
                  pl.BlockSpec((t, D_MODEL), lambda i: (i, 0)),
                  pl.BlockSpec((None, 1, D_MODEL), lambda i: (_mod_row(i // per), 0, 0)),
                  pl.BlockSpec(memory_space=pl.ANY)],
        out_specs=pl.BlockSpec((t, D_MODEL), lambda i: (i, 0)),
        out_shape=jax.ShapeDtypeStruct((N_TOK, D_MODEL), jnp.float32),
        scratch_shapes=[pltpu.VMEM((2, TOP_K, t, D_MODEL), jnp.float32),
                        pltpu.SemaphoreType.DMA((2,))],
        compiler_params=_params("arbitrary"),
        name="moe_combine",
    )(dest, dest, wts_t, x, gate, out)


def _moe(x, gate, h, ids, wts, counts, layer, w1, w3, w2):
    nk = N_TOK * TOP_K
    n_blocks = (nk + N_EXPERTS * (MOE_BLOCK - 1)) // MOE_BLOCK
    counts = counts[:, 0].astype(jnp.int32)
    padded = (counts + MOE_BLOCK - 1) // MOE_BLOCK * MOE_BLOCK
    pad_end = jnp.cumsum(padded)
    pad_start = pad_end - padded
    dest1 = pad_start[ids[0]] + ids[2]
    dest2 = pad_start[ids[1]] + ids[3]
    tok = jnp.arange(N_TOK, dtype=jnp.int32)
    src = jnp.zeros((n_blocks * MOE_BLOCK,), jnp.int32).at[jnp.concatenate([dest1, dest2])].set(
        jnp.concatenate([tok, tok]))
    block_e = jnp.minimum(jnp.searchsorted(pad_end, jnp.arange(n_blocks, dtype=jnp.int32) * MOE_BLOCK, side='right'),
                          N_EXPERTS - 1).astype(jnp.int32)
    n_active = (pad_end[-1:] // MOE_BLOCK).astype(jnp.int32)
    out = _expert_blocks(h[src], block_e + layer * N_EXPERTS, n_active, w1, w3, w2)
    dest = jnp.concatenate([dest1[None, :], dest2[None, :], jnp.zeros((SUBLANES - TOP_K, N_TOK), jnp.int32)], axis=0)
    wts_t = jnp.pad(wts[:TOP_K].T, ((0, 0), (0, LANES - TOP_K)))
    return _moe_combine(x, gate, out, dest, wts_t)


def _final_norm_kernel(x_ref, g_ref, ctx_ref, lat_ref):
    g = g_ref[...]

    def norm_into(o_ref):
        def body(r, carry):
            rows = pl.ds(pl.multiple_of(r * NORM_ROWS, NORM_ROWS), NORM_ROWS)
            x = x_ref[rows, :]
            o_ref[rows, :] = (x * lax.rsqrt(jnp.mean(x * x, axis=-1, keepdims=True) + NORM_EPS)) * g
            return carry
        lax.fori_loop(0, x_ref.shape[0] // NORM_ROWS, body, 0)

    is_ctx = pl.program_id(0) < N_CTX // ROW_TILE

    @pl.when(is_ctx)
    def _():
        norm_into(ctx_ref)

    @pl.when(jnp.logical_not(is_ctx))
    def _():
        norm_into(lat_ref)


def _final_norm(x, gain):
    ctx_tiles = N_CTX // ROW_TILE
    return pl.pallas_call(
        _final_norm_kernel,
        grid=(N_TOK // ROW_TILE,),
        in_specs=[pl.BlockSpec((ROW_TILE, D_MODEL), lambda i: (i, 0)),
                  pl.BlockSpec((1, D_MODEL), lambda i: (0, 0))],
        out_specs=[pl.BlockSpec((ROW_TILE, D_MODEL), lambda i: (jnp.minimum(i, ctx_tiles - 1), 0)),
                   pl.BlockSpec((ROW_TILE, D_MODEL), lambda i: (jnp.maximum(i - ctx_tiles, 0), 0))],
        out_shape=[jax.ShapeDtypeStruct((N_CTX, D_MODEL), jnp.float32),
                   jax.ShapeDtypeStruct((N_LAT, D_MODEL), jnp.float32)],
        compiler_params=_params("arbitrary"),
        name="final_norm",
    )(x, gain)


def _split_w_in(w):
    c0 = SSD_DIM + SSD_CONV_DIM
    c1 = c0 + 2 * SSD_HEADS
    dt_cols = jnp.pad(w[:, c0:c1], ((0, 0), (0, LANES - 2 * SSD_HEADS)))
    return (jnp.concatenate([w[:, :c0], w[:, c1:]], axis=1).astype(jnp.bfloat16), dt_cols.astype(jnp.bfloat16))


def kernel(x_prompt, x_sample, cache_k, cache_v, state_ssd, state_hgrn, c, c_ctx, w_ada, b_ada, norm_mix, norm_moe, w_in, ssd_conv_w, ssd_conv_b, ssd_dt_bias, ssd_a_log, ssd_d, ssd_norm, hg_lb_logits, hg_norm, att_rpb, w_br_ssd, w_br_hg, w_br_att, w_out, w_router, router_bias, moe_w1, moe_w3, moe_w2, final_norm):
    bf16, f32 = jnp.bfloat16, jnp.float32
    lb_cum = jnp.cumsum(jax.nn.softmax(hg_lb_logits.astype(f32), axis=1), axis=1)
    lower_bounds = lb_cum - lb_cum[:, :1]

    x = jnp.concatenate([x_prompt.reshape(N_CTX, D_MODEL), x_sample.reshape(N_LAT, D_MODEL)], axis=0)

    n_mod = 1 + DEC_BATCH
    cond = jnp.concatenate([c_ctx[None, :], c], axis=0)
    cond = jnp.pad(jax.nn.silu(cond), ((0, 2 * SUBLANES - n_mod), (0, 0)))

    wr = jnp.pad(w_router, ((0, 0), (0, LANES - N_EXPERTS)))
    wr_hi = wr.astype(bf16)
    wr_lo = (wr - wr_hi.astype(f32)).astype(bf16)
    cache_k = cache_k.reshape(DEC_BATCH, DEPTH, -1, ATT_DIM)
    cache_v = cache_v.reshape(DEC_BATCH, DEPTH, -1, ATT_DIM)
    lat_row0 = N_CTX // DEC_SEQ
    zero_ssd = jnp.zeros((BATCH, 2, SSD_HEADS, SSD_HEAD_DIM, SSD_STATE), f32)
    zero_hg = jnp.zeros((BATCH, 2, HG_HEADS, HG_KDIM, HG_VDIM), f32)

    expert_w = tuple(w.reshape((DEPTH * N_EXPERTS,) + w.shape[2:]) for w in (moe_w1, moe_w3, moe_w2))

    new_k = jnp.zeros((BATCH, DEPTH, SEQ, ATT_DIM), f32)
    new_v = jnp.zeros((BATCH, DEPTH, SEQ, ATT_DIM), f32)
    new_ssd, new_hg = [], []
    for l in range(DEPTH):
        mod = _matmul(cond, w_ada, l, 2 * SUBLANES, COL_TILE)[:n_mod] + b_ada[l]
        mod = mod.reshape(n_mod, 6, 1, D_MODEL)
        shift_m, scale_m, gate_m, shift_f, scale_f, gate_f = (mod[:, i] for i in range(6))

        proj, dt_logits = _norm_matmul(x, norm_mix[l][None, :], scale_m, shift_m, *_split_w_in(w_in[l]))

        conv_b = ssd_conv_b[l][None, :]
        a_neg = -jnp.exp(ssd_a_log[l].astype(f32))
        d_row = jnp.repeat(ssd_d[l], SSD_HEAD_DIM)[None, :]
        ssd_gain = ssd_norm[l][None, :]
        y_ssd = []
        for row0, nb, length, init in ((0, BATCH, SEQ, zero_ssd), (N_CTX, DEC_BATCH, DEC_SEQ, state_ssd[:, l])):
            xbc = _ssd_conv(proj, row0, nb * length, length, ssd_conv_w[l], conv_b)
            y_f, y_b, states = _ssd_scan(xbc, dt_logits, row0, nb, length, ssd_dt_bias[l], a_neg, init)
            y_ssd.append(_ssd_finish(y_f, y_b, xbc, proj, row0, d_row, ssd_gain))
            if row0 == 0:
                new_ssd.append(states)

        lb = lower_bounds[:, l]
        hg_gain = hg_norm[l].reshape(1, HG_DIM)
        y_hg_ctx, states = _hgrn_mixer(proj, 0, BATCH, SEQ, lb, hg_gain, zero_hg, HG_CTX_HEADS)
        new_hg.append(states)
        y_hg_lat, _ = _hgrn_mixer(proj, lat_row0, DEC_BATCH, DEC_SEQ, lb, hg_gain, state_hgrn[:, l], 1)

        y_att_ctx, new_k, new_v = _context_attention(proj, BATCH, SEQ, new_k, new_v, l)
        y_att_lat = _neighbourhood_attention(proj, lat_row0, DEC_BATCH, DEC_SEQ, cache_k, cache_v, l,
                                             _window_bias(att_rpb[l]))

        merged = _branch_merge((y_ssd[0], y_hg_ctx, y_att_ctx), (y_ssd[1], y_hg_lat, y_att_lat),
                               (w_br_ssd[l].astype(bf16), w_br_hg[l].astype(bf16), w_br_att[l].astype(bf16)), proj)
        x = _out_residual(merged, w_out[l].astype(bf16), x, gate_m)

        h2, ids, wts, counts = _norm_router(x, norm_moe[l][None, :], scale_f, shift_f, wr_hi, wr_lo,
                                            router_bias.astype(f32)[:, None])
        x = _moe(x, gate_f, h2, ids, wts, counts, l, *expert_w)

    y_ctx, y_lat = _final_norm(x, final_norm[None, :])
    y_prompt = y_ctx.reshape(BATCH, SEQ, D_MODEL)
    y_sample = y_lat.reshape(DEC_BATCH, DEC_SEQ, D_MODEL)
    cache_shape = (BATCH, DEPTH, SEQ, ATT_HEADS, ATT_HEAD_DIM)
    return (y_prompt, y_sample, new_k.reshape(cache_shape), new_v.reshape(cache_shape),
            jnp.stack(new_ssd, axis=1), jnp.stack(new_hg, axis=1))
```
